```python
import math
import jax
import jax.numpy as jnp
from jax import lax
import numpy as np

D_MODEL = 1024
BATCH = 4
SEQ = 8192
DEPTH = 1

MOBA_HEADS = 8
MOBA_HEAD_DIM = 64
MOBA_BLOCK = 256
MOBA_TOPK = 3
MOBA_Q_CHUNK = 64
DIFF_HEADS = 4
DIFF_HEAD_DIM = 64
ATTN_Q_BLOCK = 128
ROPE_THETA = 10000.0
N_GROUPS = 4
EXPERTS_PER_GROUP = 8
N_EXPERTS = N_GROUPS * EXPERTS_PER_GROUP
EXPERT_TOPK = 2
D_EXPERT = 512
NORM_EPS = 1e-6
NEG_INF = -1e30
MOBA_WIDTH = MOBA_HEADS * MOBA_HEAD_DIM
DIFF_QK_WIDTH = DIFF_HEADS * 2 * DIFF_HEAD_DIM
DIFF_V_WIDTH = DIFF_HEADS * 2 * DIFF_HEAD_DIM
_SEGMENTS = (MOBA_WIDTH, MOBA_WIDTH, MOBA_WIDTH, DIFF_QK_WIDTH, DIFF_QK_WIDTH, DIFF_V_WIDTH, D_MODEL, D_MODEL)
IN_WIDTH = sum(_SEGMENTS)
IN_SPLITS = tuple(int(s) for s in np.cumsum(_SEGMENTS)[:-1])

kernel_name = 'hybrid_moba_diffattn_hmoe_adaln'


def rmsnorm(x, g):
    xf = x.astype(jnp.float32)
    y = xf * lax.rsqrt(jnp.mean(xf * xf, axis=-1, keepdims=True) + NORM_EPS)
    return (y * g.astype(jnp.float32)).astype(x.dtype)


def rope_tables(seq, dim):
    inv = 1.0 / (ROPE_THETA ** (jnp.arange(0, dim, 2, dtype=jnp.float32) / dim))
    ang = jnp.arange(seq, dtype=jnp.float32)[:, None] * inv[None, :]
    return jnp.cos(ang), jnp.sin(ang)


def apply_rope(x, cos, sin):
    xf = x.astype(jnp.float32)
    half = x.shape[-1] // 2
    x1, x2 = xf[..., :half], xf[..., half:]
    return jnp.concatenate([x1 * cos - x2 * sin, x2 * cos + x1 * sin], axis=-1).astype(x.dtype)


def to_heads(t, n, d):
    b, s = t.shape[:2]
    return t.reshape(b, s, n, d).transpose(0, 2, 1, 3)


def moba_attention(q, k, v):
    B, H, S, dh = q.shape
    nb = -(-S // MOBA_BLOCK)
    topk = min(MOBA_TOPK, nb)
    pad = nb * MOBA_BLOCK - S
    kp = jnp.pad(k, ((0, 0), (0, 0), (0, pad), (0, 0)))
    vp = jnp.pad(v, ((0, 0), (0, 0), (0, pad), (0, 0)))
    k_blocks = kp.reshape(B, H, nb, MOBA_BLOCK, dh)
    v_blocks = vp.reshape(B, H, nb, MOBA_BLOCK, dh)
    k_mean = jnp.mean(k_blocks.astype(jnp.float32), axis=3)
    scale = dh ** -0.5
    nc = S // MOBA_Q_CHUNK
    q_chunks = q.reshape(B, H, nc, MOBA_Q_CHUNK, dh).transpose(2, 0, 1, 3, 4)
    starts = jnp.arange(nc, dtype=jnp.int32) * MOBA_Q_CHUNK
    b_idx = jnp.arange(B)[:, None, None, None]
    h_idx = jnp.arange(H)[None, :, None, None]
    block_ids = jnp.arange(nb, dtype=jnp.int32)
    own_offsets = jnp.arange(MOBA_BLOCK, dtype=jnp.int32)

    def chunk(args):
        qb, start = args
        blk = start // MOBA_BLOCK
        qpos = start + jnp.arange(MOBA_Q_CHUNK, dtype=jnp.int32)
        gate = jnp.einsum('bhqd,bhnd->bhqn', qb.astype(jnp.float32), k_mean)
        gate = jnp.where(block_ids < blk, gate, NEG_INF)
        _, sel = lax.top_k(gate, topk)
        valid = sel < blk
        ks = k_blocks[b_idx, h_idx, sel]
        vs = v_blocks[b_idx, h_idx, sel]
        s_sel = jnp.einsum('bhqd,bhqjnd->bhqjn', qb, ks).astype(jnp.float32) * scale
        s_sel = jnp.where(valid[..., None], s_sel, NEG_INF)
        s_sel = s_sel.reshape(B, H, MOBA_Q_CHUNK, topk * MOBA_BLOCK)
        k_own = lax.dynamic_slice_in_dim(kp, blk * MOBA_BLOCK, MOBA_BLOCK, axis=2)
        v_own = lax.dynamic_slice_in_dim(vp, blk * MOBA_BLOCK, MOBA_BLOCK, axis=2)
        s_own = jnp.einsum('bhqd,bhnd->bhqn', qb, k_own).astype(jnp.float32) * scale
        kpos = blk * MOBA_BLOCK + own_offsets
        s_own = jnp.where(kpos[None, :] <= qpos[:, None], s_own, NEG_INF)
        p = jax.nn.softmax(jnp.concatenate([s_sel, s_own], axis=-1), axis=-1).astype(v.dtype)
        p_sel = p[..., :topk * MOBA_BLOCK].reshape(B, H, MOBA_Q_CHUNK, topk, MOBA_BLOCK)
        p_own = p[..., topk * MOBA_BLOCK:]
        return (jnp.einsum('bhqjn,bhqjnd->bhqd', p_sel, vs)
                + jnp.einsum('bhqn,bhnd->bhqd', p_own, v_own))

    out = lax.map(chunk, (q_chunks, starts))
    return out.transpose(1, 2, 0, 3, 4).reshape(B, H, S, dh)


def diff_attention(q, k, v, lam):
    B, H2, S, dk = q.shape
    H = H2 // 2
    scale = dk ** -0.5
    nq = S // ATTN_Q_BLOCK
    q_blocks = q.reshape(B, H2, nq, ATTN_Q_BLOCK, dk).transpose(2, 0, 1, 3, 4)
    starts = jnp.arange(nq, dtype=jnp.int32) * ATTN_Q_BLOCK
    kpos = jnp.arange(S, dtype=jnp.int32)

    def block(args):
        qb, start = args
        s = jnp.einsum('bhqd,bhkd->bhqk', qb, k).astype(jnp.float32) * scale
        qpos = start + jnp.arange(ATTN_Q_BLOCK, dtype=jnp.int32)
        s = jnp.where(kpos[None, :] <= qpos[:, None], s, NEG_INF)
        p = jax.nn.softmax(s, axis=-1).reshape(B, H, 2, ATTN_Q_BLOCK, S)
        a = (p[:, :, 0] - lam * p[:, :, 1]).astype(v.dtype)
        return jnp.einsum('bhqk,bhkd->bhqd', a, v)

    out = lax.map(block, (q_blocks, starts))
    return out.transpose(1, 2, 0, 3, 4).reshape(B, H, S, 2 * dk)


def hierarchical_moe(h, w_group, b_group, w_expert, b_expert, w_gate, w_up, w_down):
    B, S, D = h.shape
    t = h.reshape(B * S, D)
    T = t.shape[0]
    g_prob = jax.nn.softmax((t @ w_group).astype(jnp.float32) + b_group, axis=-1)
    g_w, g_idx = lax.top_k(g_prob, 1)
    e_logits = ((t @ w_expert).astype(jnp.float32) + b_expert).reshape(T, N_GROUPS, EXPERTS_PER_GROUP)
    e_in_group = jnp.take_along_axis(e_logits, g_idx[:, :, None], axis=1)[:, 0]
    e_prob = jax.nn.softmax(e_in_group, axis=-1)
    e_w, e_idx = lax.top_k(e_prob, EXPERT_TOPK)
    e_w = e_w / jnp.sum(e_w, axis=-1, keepdims=True)
    weights = (g_w * e_w).reshape(-1)
    expert = (g_idx * EXPERTS_PER_GROUP + e_idx).reshape(-1)
    order = jnp.argsort(expert)
    tok = order // EXPERT_TOPK
    xs = t[tok]
    sizes = jnp.bincount(expert, length=N_EXPERTS).astype(jnp.int32)
    hid = jax.nn.silu(lax.ragged_dot(xs, w_gate, sizes)) * lax.ragged_dot(xs, w_up, sizes)
    out = lax.ragged_dot(hid.astype(xs.dtype), w_down, sizes)
    out = out * weights[order][:, None].astype(out.dtype)
    y = jnp.zeros_like(t).at[tok].add(out.astype(t.dtype))
    return y.reshape(B, S, D)


def setup_inputs(seed: int = 0) -> dict:
    key = jax.random.key(seed)
    ks = jax.random.split(key, 24)
    D = D_MODEL

    def nrm(k, shape, scale):
        return jax.random.normal(k, shape, jnp.float32) * scale

    return {
        'x': nrm(ks[0], (BATCH, SEQ, D), 1.0),
        'c': nrm(ks[1], (BATCH, D), 1.0),
        'w_ada': nrm(ks[2], (DEPTH, D, 6 * D), 0.5 * D ** -0.5),
        'b_ada': nrm(ks[3], (DEPTH, 6 * D), 0.02),
        'norm1_g': 1.0 + nrm(ks[4], (DEPTH, D), 0.02),
        'w_in': nrm(ks[5], (DEPTH, D, IN_WIDTH), D ** -0.5),
        'lambda_q1': nrm(ks[6], (DEPTH, DIFF_HEAD_DIM), 0.1),
        'lambda_k1': nrm(ks[7], (DEPTH, DIFF_HEAD_DIM), 0.1),
        'lambda_q2': nrm(ks[8], (DEPTH, DIFF_HEAD_DIM), 0.1),
        'lambda_k2': nrm(ks[9], (DEPTH, DIFF_HEAD_DIM), 0.1),
        'diff_subln_g': 1.0 + nrm(ks[10], (DEPTH, 2 * DIFF_HEAD_DIM), 0.02),
        'w_proj_moba': nrm(ks[11], (DEPTH, MOBA_WIDTH, D), MOBA_WIDTH ** -0.5),
        'w_proj_diff': nrm(ks[12], (DEPTH, DIFF_V_WIDTH, D), DIFF_V_WIDTH ** -0.5),
        'w_out': nrm(ks[13], (DEPTH, D, D), D ** -0.5),
        'norm2_g': 1.0 + nrm(ks[14], (DEPTH, D), 0.02),
        'w_group': nrm(ks[15], (DEPTH, D, N_GROUPS), D ** -0.5),
        'b_group': nrm(ks[16], (DEPTH, N_GROUPS), 0.01),
        'w_expert': nrm(ks[17], (DEPTH, D, N_EXPERTS), D ** -0.5),
        'b_expert': nrm(ks[18], (DEPTH, N_EXPERTS), 0.01),
        'w_gate': nrm(ks[19], (DEPTH, N_EXPERTS, D, D_EXPERT), D ** -0.5),
        'w_up': nrm(ks[20], (DEPTH, N_EXPERTS, D, D_EXPERT), D ** -0.5),
        'w_down': nrm(ks[21], (DEPTH, N_EXPERTS, D_EXPERT, D), D_EXPERT ** -0.5),
        'final_g': 1.0 + nrm(ks[22], (D,), 0.02),
    }


def reference(x, c, w_ada, b_ada, norm1_g, w_in, lambda_q1, lambda_k1, lambda_q2, lambda_k2,
              diff_subln_g, w_proj_moba, w_proj_diff, w_out, norm2_g, w_group, b_group,
              w_expert, b_expert, w_gate, w_up, w_down, final_g):
    S = x.shape[1]
    cos_m, sin_m = rope_tables(S, MOBA_HEAD_DIM)
    cos_d, sin_d = rope_tables(S, DIFF_HEAD_DIM)
    c_act = jax.nn.silu(c)
    for l in range(DEPTH):
        mod = c_act @ w_ada[l] + b_ada[l]
        sh1, sc1, g1, sh2, sc2, g2 = jnp.split(mod, 6, axis=-1)

        h = rmsnorm(x, norm1_g[l]) * (1.0 + sc1[:, None, :]) + sh1[:, None, :]
        proj = h @ w_in[l]
        q_m, k_m, v_m, q_d, k_d, v_d, gate_m, gate_d = jnp.split(proj, IN_SPLITS, axis=-1)

        qm = apply_rope(to_heads(q_m, MOBA_HEADS, MOBA_HEAD_DIM), cos_m, sin_m)
        km = apply_rope(to_heads(k_m, MOBA_HEADS, MOBA_HEAD_DIM), cos_m, sin_m)
        vm = to_heads(v_m, MOBA_HEADS, MOBA_HEAD_DIM)
        o_m = moba_attention(qm, km, vm)
        y_m = o_m.transpose(0, 2, 1, 3).reshape(x.shape[0], S, MOBA_WIDTH) @ w_proj_moba[l]

        lam_init = 0.8 - 0.6 * math.exp(-0.3 * l)
        lam = (jnp.exp(jnp.sum(lambda_q1[l].astype(jnp.float32) * lambda_k1[l].astype(jnp.float32)))
               - jnp.exp(jnp.sum(lambda_q2[l].astype(jnp.float32) * lambda_k2[l].astype(jnp.float32)))
               + lam_init)
        qd = apply_rope(to_heads(q_d, 2 * DIFF_HEADS, DIFF_HEAD_DIM), cos_d, sin_d)
        kd = apply_rope(to_heads(k_d, 2 * DIFF_HEADS, DIFF_HEAD_DIM), cos_d, sin_d)
        vd = to_heads(v_d, DIFF_HEADS, 2 * DIFF_HEAD_DIM)
        o_d = diff_attention(qd, kd, vd, lam)
        o_d = rmsnorm(o_d, diff_subln_g[l]) * (1.0 - lam_init)
        y_d = o_d.transpose(0, 2, 1, 3).reshape(x.shape[0], S, DIFF_V_WIDTH) @ w_proj_diff[l]

        merged = jax.nn.sigmoid(gate_m) * y_m + jax.nn.sigmoid(gate_d) * y_d
        x = x + g1[:, None, :] * (merged @ w_out[l])

        h2 = rmsnorm(x, norm2_g[l]) * (1.0 + sc2[:, None, :]) + sh2[:, None, :]
        y_ffn = hierarchical_moe(h2, w_group[l], b_group[l], w_expert[l], b_expert[l],
                                 w_gate[l], w_up[l], w_down[l])
        x = x + g2[:, None, :] * y_ffn
    return rmsnorm(x, final_g)
```

```python
import functools
import math

import jax
import jax.numpy as jnp
from jax import lax
from jax.experimental import pallas as pl
from jax.experimental.pallas import tpu as pltpu

F32 = jnp.float32
BF16 = jnp.bfloat16
I32 = jnp.int32

LANES = 128
HEAD_DIM = 64
HEADS_PER_VREG = LANES // HEAD_DIM
MOBA_HEADS = 8
MOBA_BLOCK = 256
MOBA_TOPK = 3
DIFF_HEADS = 4
ROPE_THETA = 10000.0
N_GROUPS = 4
EXPERTS_PER_GROUP = 8
N_EXPERTS = N_GROUPS * EXPERTS_PER_GROUP
NORM_EPS = 1e-6
NEG_INF = -1e30
SEG = MOBA_HEADS * HEAD_DIM
N_QKV_SEGS = 6

PROJ_ROWS = 512
ATTN_ROWS = 256
MERGE_ROWS = 256
SORT_ROWS = 512
EXPERT_ROWS = 256
MOVE_ROWS = 256
VMEM_LIMIT = 56 * 1024 * 1024


def _params(n_axes, vmem=VMEM_LIMIT):
    return pltpu.CompilerParams(dimension_semantics=("arbitrary",) * n_axes,
                                vmem_limit_bytes=vmem)


def _dot(a, b):
    return jnp.dot(a, b, preferred_element_type=F32)


def _dot_nt(a, b):
    return lax.dot_general(a, b, (((1,), (1,)), ((), ())), preferred_element_type=F32)


def _rms(x, g):
    return x * lax.rsqrt(jnp.mean(x * x, axis=-1, keepdims=True) + NORM_EPS) * g


def _ada_kernel(c_ref, w_ref, b_ref, o_ref):
    c = c_ref[...]
    o_ref[...] = _dot(c * jax.nn.sigmoid(c), w_ref[...]) + b_ref[...]


def _ada(c_pad, w, b):
    rows, d = c_pad.shape
    n = w.shape[1]
    tn = 1536
    return pl.pallas_call(
        _ada_kernel,
        grid=(n // tn,),
        in_specs=[pl.BlockSpec((rows, d), lambda j: (0, 0)),
                  pl.BlockSpec((d, tn), lambda j: (0, j)),
                  pl.BlockSpec((1, tn), lambda j: (0, j))],
        out_specs=pl.BlockSpec((rows, tn), lambda j: (0, j)),
        out_shape=jax.ShapeDtypeStruct((rows, n), F32),
        compiler_params=_params(1),
        name="ada",
    )(c_pad, w, b)


def _rope(x, cos, sin_signed):
    half = HEAD_DIM // 2
    width = x.shape[1]
    lane = lax.broadcasted_iota(I32, x.shape, 1)
    first = (lane & (HEAD_DIM - 1)) < half
    partner = jnp.where(first, pltpu.roll(x, width - half, 1), pltpu.roll(x, half, 1))
    return x * cos + partner * sin_signed


def _proj_kernel(x_ref, mod_ref, g_ref, w_ref, cos_ref, sin_ref,
                 qm_ref, km_ref, vm_ref, qd_ref, kd_ref, vd_ref, kmean_ref, *, d):
    x = x_ref[...]
    sh = mod_ref[0, :, 0:d]
    sc = mod_ref[0, :, d:2 * d]
    h = (_rms(x, g_ref[...]) * (1.0 + sc) + sh).astype(BF16)
    cos = cos_ref[...]
    sin = sin_ref[...]
    scale = HEAD_DIM ** -0.5
    outs = (qm_ref, km_ref, vm_ref, qd_ref, kd_ref, vd_ref)
    for seg, o_ref in enumerate(outs):
        y = _dot(h, w_ref[:, seg * SEG:(seg + 1) * SEG])
        if seg in (0, 1, 3, 4):
            y = _rope(y, cos, sin)
        if seg in (0, 3):
            y = y * scale
        if seg == 1:
            for blk in range(y.shape[0] // MOBA_BLOCK):
                rows = y[blk * MOBA_BLOCK:(blk + 1) * MOBA_BLOCK]
                kmean_ref[0, blk:blk + 1, :] = jnp.mean(rows, axis=0, keepdims=True)
        o_ref[...] = y.astype(BF16)


def _proj(x2, mod3, g, w_qkv, cos, sin, seq):
    t, d = x2.shape
    tm = PROJ_ROWS
    per_batch = seq // tm
    row_spec = pl.BlockSpec((tm, SEG), lambda i: (i, 0))
    tab_spec = pl.BlockSpec((tm, SEG), lambda i: (i % per_batch, 0))
    act = jax.ShapeDtypeStruct((t, SEG), BF16)
    return pl.pallas_call(
        functools.partial(_proj_kernel, d=d),
        grid=(t // tm,),
        in_specs=[pl.BlockSpec((tm, d), lambda i: (i, 0)),
                  pl.BlockSpec((1, 1, mod3.shape[2]), lambda i: (i // per_batch, 0, 0)),
                  pl.BlockSpec((1, d), lambda i: (0, 0)),
                  pl.BlockSpec(w_qkv.shape, lambda i: (0, 0)),
                  tab_spec, tab_spec],
        out_specs=[row_spec] * N_QKV_SEGS + [
            pl.BlockSpec((1, tm // MOBA_BLOCK, SEG), lambda i: (i, 0, 0))],
        out_shape=[act] * N_QKV_SEGS + [
            jax.ShapeDtypeStruct((t // tm, tm // MOBA_BLOCK, SEG), F32)],
        compiler_params=_params(1),
        name="proj",
    )(x2, mod3, g, w_qkv, cos, sin)


def _flash_first(s, v):
    m = jnp.max(s, axis=1, keepdims=True)
    p = jnp.exp(s - m)
    l = jnp.sum(p, axis=1, keepdims=True)
    return m, l, _dot(p.astype(BF16), v)


def _flash_step(s, v, m, l, acc):
    m_new = jnp.maximum(m, jnp.max(s, axis=1, keepdims=True))
    alpha = jnp.exp(m - m_new)
    p = jnp.exp(s - m_new)
    l = alpha * l + jnp.sum(p, axis=1, keepdims=True)
    return m_new, l, alpha * acc + _dot(p.astype(BF16), v)


def _causal_mask(rows):
    r = lax.broadcasted_iota(I32, (rows, rows), 0)
    c = lax.broadcasted_iota(I32, (rows, rows), 1)
    return c <= r


def _moba_kernel(q_ref, k_ref, v_ref, kmean_ref, o_ref, *, n_blocks):
    tq = ATTN_ROWS
    blk = pl.program_id(2)
    q2 = q_ref[...]
    lane = lax.broadcasted_iota(I32, (tq, LANES), 1)
    head_of_lane = lane >> 6
    kmean = kmean_ref[0]
    km_lane = lax.broadcasted_iota(I32, kmean.shape, 1) >> 6
    pad = jnp.zeros((LANES - n_blocks, LANES), F32)
    own = pl.multiple_of(blk * tq, tq)
    k_own = k_ref[pl.ds(own, tq), :]
    v_own = v_ref[pl.ds(own, tq), :]
    causal = _causal_mask(tq)
    lane_k = lax.broadcasted_iota(I32, (tq, LANES), 1)
    out = jnp.zeros((tq, LANES), F32)
    for head in range(HEADS_PER_VREG):
        qh = jnp.where(head_of_lane == head, q2, jnp.zeros_like(q2))
        km = jnp.concatenate([jnp.where(km_lane == head, kmean, 0.0), pad], axis=0)
        km_hi = km.astype(BF16)
        km_lo = (km - km_hi.astype(F32)).astype(BF16)
        gate = _dot_nt(qh, km_hi) + _dot_nt(qh, km_lo)
        gate = jnp.where(lane < blk, gate, NEG_INF)
        bias = jnp.full((tq, LANES), NEG_INF, F32)
        for _ in range(min(MOBA_TOPK, n_blocks)):
            top = jnp.max(gate, axis=1, keepdims=True)
            first = jnp.min(jnp.where(gate == top, lane, LANES), axis=1, keepdims=True)
            picked = lane == first
            bias = jnp.where(picked & (lane < blk), 0.0, bias)
            gate = jnp.where(picked, -jnp.inf, gate)
        lhs = jnp.concatenate([qh, bias.astype(BF16)], axis=1)

        s = jnp.where(causal, _dot_nt(qh, k_own), NEG_INF)
        m, l, acc = _flash_first(s, v_own)

        def body(j, carry, lhs=lhs):
            start = pl.multiple_of(j * tq, tq)
            kj = k_ref[pl.ds(start, tq), :]
            ej = jnp.where(lane_k == j, 1.0, 0.0).astype(BF16)
            s = _dot_nt(lhs, jnp.concatenate([kj, ej], axis=1))
            return _flash_step(s, v_ref[pl.ds(start, tq), :], *carry)

        m, l, acc = lax.fori_loop(0, blk, body, (m, l, acc))
        out = jnp.where(head_of_lane == head, acc / l, out)
    o_ref[...] = out.astype(BF16)


def _moba(qm, km, vm, kmean, batch, seq):
    t = qm.shape[0]
    tq = ATTN_ROWS
    nq = seq // tq
    n_blocks = seq // MOBA_BLOCK
    pairs = MOBA_HEADS // HEADS_PER_VREG
    kv_spec = pl.BlockSpec((seq, LANES), lambda b, p, i: (b, p))
    return pl.pallas_call(
        functools.partial(_moba_kernel, n_blocks=n_blocks),
        grid=(batch, pairs, nq),
        in_specs=[pl.BlockSpec((tq, LANES), lambda b, p, i: (b * nq + i, p)),
                  kv_spec, kv_spec,
                  pl.BlockSpec((1, n_blocks, LANES), lambda b, p, i: (b, 0, p))],
        out_specs=pl.BlockSpec((tq, LANES), lambda b, p, i: (b * nq + i, p)),
        out_shape=jax.ShapeDtypeStruct((t, SEG), BF16),
        compiler_params=_params(3),
        name="moba",
    )(qm, km, vm, kmean)


def _diff_kernel(q_ref, k_ref, v_ref, lq1_ref, lk1_ref, lq2_ref, lk2_ref, g_ref, o_ref, *, lam_init):
    tq = ATTN_ROWS
    i = pl.program_id(2)
    q2 = q_ref[...]
    comp_of_lane = lax.broadcasted_iota(I32, (tq, LANES), 1) >> 6
    qs = [jnp.where(comp_of_lane == c, q2, jnp.zeros_like(q2)) for c in range(2)]
    own = pl.multiple_of(i * tq, tq)
    k_own = k_ref[pl.ds(own, tq), :]
    v_own = v_ref[pl.ds(own, tq), :]
    causal = _causal_mask(tq)
    carry = ()
    for qc in qs:
        s = jnp.where(causal, _dot_nt(qc, k_own), NEG_INF)
        carry += _flash_first(s, v_own)

    def body(j, carry):
        start = pl.multiple_of(j * tq, tq)
        kj = k_ref[pl.ds(start, tq), :]
        vj = v_ref[pl.ds(start, tq), :]
        new = ()
        for c, qc in enumerate(qs):
            new += _flash_step(_dot_nt(qc, kj), vj, *carry[3 * c:3 * c + 3])
        return new

    m1, l1, a1, m2, l2, a2 = lax.fori_loop(0, i, body, carry)
    lam = (jnp.exp(jnp.sum(lq1_ref[...] * lk1_ref[...], axis=1, keepdims=True))
           - jnp.exp(jnp.sum(lq2_ref[...] * lk2_ref[...], axis=1, keepdims=True)) + lam_init)
    o = a1 / l1 - lam * (a2 / l2)
    o_ref[...] = (_rms(o, g_ref[...]) * (1.0 - lam_init)).astype(BF16)


def _diff(qd, kd, vd, lq1, lk1, lq2, lk2, g, batch, seq, lam_init):
    t = qd.shape[0]
    tq = ATTN_ROWS
    nq = seq // tq
    kv_spec = pl.BlockSpec((seq, LANES), lambda b, h, i: (b, h))
    vec = lambda a: pl.BlockSpec(a.shape, lambda b, h, i: (0, 0))
    return pl.pallas_call(
        functools.partial(_diff_kernel, lam_init=lam_init),
        grid=(batch, DIFF_HEADS, nq),
        in_specs=[pl.BlockSpec((tq, LANES), lambda b, h, i: (b * nq + i, h)),
                  kv_spec, kv_spec, vec(lq1), vec(lk1), vec(lq2), vec(lk2), vec(g)],
        out_specs=pl.BlockSpec((tq, LANES), lambda b, h, i: (b * nq + i, h)),
        out_shape=jax.ShapeDtypeStruct((t, SEG), BF16),
        compiler_params=_params(3),
        name="diff",
    )(qd, kd, vd, lq1, lk1, lq2, lk2, g)


def _merge_kernel(x_ref, om_ref, od_ref, mod_ref, g1n_ref, g2n_ref, wg_ref, wpm_ref, wpd_ref,
                  wout_ref, wr_ref, br_ref, x1_ref, h2_ref, eid_ref, ew_ref, merged, *, d):
    x = x_ref[...]
    sh1 = mod_ref[0, :, 0:d]
    sc1 = mod_ref[0, :, d:2 * d]
    g1 = mod_ref[0, :, 2 * d:3 * d]
    sh2 = mod_ref[0, :, 3 * d:4 * d]
    sc2 = mod_ref[0, :, 4 * d:5 * d]
    h = (_rms(x, g1n_ref[...]) * (1.0 + sc1) + sh1).astype(BF16)
    om = om_ref[...]
    od = od_ref[...]
    chunk = SEG
    for c in range(d // chunk):
        cols = slice(c * chunk, (c + 1) * chunk)
        gm = _dot(h, wg_ref[:, c * chunk:(c + 1) * chunk])
        gd = _dot(h, wg_ref[:, d + c * chunk:d + (c + 1) * chunk])
        ym = _dot(om, wpm_ref[:, cols])
        yd = _dot(od, wpd_ref[:, cols])
        merged[:, cols] = (jax.nn.sigmoid(gm) * ym + jax.nn.sigmoid(gd) * yd).astype(BF16)
    x1 = x + g1 * _dot(merged[...], wout_ref[...])
    x1_ref[...] = x1
    h2 = _rms(x1, g2n_ref[...]) * (1.0 + sc2) + sh2
    h2_ref[...] = h2

    logits = jnp.dot(h2, wr_ref[...], preferred_element_type=F32,
                     precision=lax.Precision.HIGHEST) + br_ref[...]
    lane = lax.broadcasted_iota(I32, logits.shape, 1)
    is_group = lane < N_GROUPS
    gl = jnp.where(is_group, logits, -jnp.inf)
    gmax = jnp.max(gl, axis=1, keepdims=True)
    gexp = jnp.exp(gl - gmax)
    g_w = 1.0 / jnp.sum(gexp, axis=1, keepdims=True)
    g_idx = jnp.min(jnp.where(gl == gmax, lane, LANES), axis=1, keepdims=True)
    e_lane = lane - N_GROUPS
    in_group = (e_lane >= g_idx * EXPERTS_PER_GROUP) & (e_lane < (g_idx + 1) * EXPERTS_PER_GROUP)
    el = jnp.where(in_group, logits, -jnp.inf)
    emax = jnp.max(el, axis=1, keepdims=True)
    eexp = jnp.exp(el - emax)
    prob = eexp / jnp.sum(eexp, axis=1, keepdims=True)
    prob = jnp.where(in_group, prob, -1.0)
    p1 = jnp.max(prob, axis=1, keepdims=True)
    i1 = jnp.min(jnp.where(prob == p1, lane, LANES), axis=1, keepdims=True)
    prob2 = jnp.where(lane == i1, -1.0, prob)
    p2 = jnp.max(prob2, axis=1, keepdims=True)
    i2 = jnp.min(jnp.where(prob2 == p2, lane, LANES), axis=1, keepdims=True)
    tot = p1 + p2
    eid_ref[...] = jnp.where(lane == 0, i1 - N_GROUPS, jnp.where(lane == 1, i2 - N_GROUPS, 0))
    ew_ref[...] = jnp.where(lane == 0, g_w * (p1 / tot), jnp.where(lane == 1, g_w * (p2 / tot), 0.0))


def _merge(x2, om, od, mod3, g1n, g2n, wg, wpm, wpd, wout, wr, br, seq):
    t, d = x2.shape
    tm = MERGE_ROWS
    per_batch = seq // tm
    full = lambda a: pl.BlockSpec(a.shape, lambda i: (0,) * a.ndim)
    row = lambda w: pl.BlockSpec((tm, w), lambda i: (i, 0))
    return pl.pallas_call(
        functools.partial(_merge_kernel, d=d),
        grid=(t // tm,),
        in_specs=[row(d), row(SEG), row(SEG),
                  pl.BlockSpec((1, 1, mod3.shape[2]), lambda i: (i // per_batch, 0, 0)),
                  full(g1n), full(g2n), full(wg), full(wpm), full(wpd), full(wout), full(wr), full(br)],
        out_specs=[row(d), row(d), row(LANES), row(LANES)],
        out_shape=[jax.ShapeDtypeStruct((t, d), F32), jax.ShapeDtypeStruct((t, d), F32),
                   jax.ShapeDtypeStruct((t, LANES), I32), jax.ShapeDtypeStruct((t, LANES), F32)],
        scratch_shapes=[pltpu.VMEM((tm, d), BF16)],
        compiler_params=_params(1),
        name="merge",
    )(x2, om, od, mod3, g1n, g2n, wg, wpm, wpd, wout, wr, br)


def _onehots(eid):
    lane = lax.broadcasted_iota(I32, eid.shape, 1)
    return lane, lane == eid[:, 0:1], lane == eid[:, 1:2]


def _rank_kernel(eid_ref, rank_ref, counts_ref, carry):
    tm = eid_ref.shape[0]

    @pl.when(pl.program_id(0) == 0)
    def _():
        carry[...] = jnp.zeros_like(carry)

    lane, oh0, oh1 = _onehots(eid_ref[...])
    used = jnp.where(oh0 | oh1, 1.0, 0.0)
    r = lax.broadcasted_iota(I32, (tm, tm), 0)
    c = lax.broadcasted_iota(I32, (tm, tm), 1)
    before = jnp.where(c < r, 1.0, 0.0).astype(BF16)
    base = carry[0:1, :] + _dot(before, used.astype(BF16))
    r0 = jnp.sum(jnp.where(oh0, base, 0.0), axis=1, keepdims=True)
    r1 = jnp.sum(jnp.where(oh1, base, 0.0), axis=1, keepdims=True)
    rank_ref[...] = jnp.where(lane == 0, r0, jnp.where(lane == 1, r1, 0.0)).astype(I32)
    carry[...] = carry[...] + jnp.sum(used, axis=0, keepdims=True)
    counts_ref[...] = carry[...].astype(I32)


def _rank(eid):
    t = eid.shape[0]
    tm = SORT_ROWS
    return pl.pallas_call(
        _rank_kernel,
        grid=(t // tm,),
        in_specs=[pl.BlockSpec((tm, LANES), lambda i: (i, 0))],
        out_specs=[pl.BlockSpec((tm, LANES), lambda i: (i, 0)),
                   pl.BlockSpec((8, LANES), lambda i: (0, 0))],
        out_shape=[jax.ShapeDtypeStruct((t, LANES), I32), jax.ShapeDtypeStruct((8, LANES), I32)],
        scratch_shapes=[pltpu.VMEM((8, LANES), F32)],
        compiler_params=_params(1),
        name="rank",
    )(eid)


def _segment_ends(counts):
    lane = lax.broadcasted_iota(I32, counts.shape, 1)
    padded = (counts + (EXPERT_ROWS - 1)) & (-EXPERT_ROWS)
    padded = jnp.where(lane < N_EXPERTS, padded, 0)
    ends = padded
    shift = 1
    while shift < N_EXPERTS:
        ends = ends + jnp.where(lane >= shift, pltpu.roll(ends, shift, 1), 0)
        shift *= 2
    return padded, ends


def _pos_kernel(eid_ref, rank_ref, counts_ref, pos_ref, tile_ref, seg_ref):
    padded, ends = _segment_ends(counts_ref[...])
    starts = (ends - padded)[0:1, :]
    lane, oh0, oh1 = _onehots(eid_ref[...])
    rank = rank_ref[...]
    p0 = rank[:, 0:1] + jnp.sum(jnp.where(oh0, starts, 0), axis=1, keepdims=True)
    p1 = rank[:, 1:2] + jnp.sum(jnp.where(oh1, starts, 0), axis=1, keepdims=True)
    pos_ref[...] = jnp.where(lane == 0, p0, jnp.where(lane == 1, p1, 0))

    @pl.when(pl.program_id(0) == 0)
    def _():
        n_tiles = tile_ref.shape[0]
        first_row = lax.broadcasted_iota(I32, (n_tiles, LANES), 0) * EXPERT_ROWS
        elane = lax.broadcasted_iota(I32, (n_tiles, LANES), 1)
        done = jnp.where((ends[0:1, :] <= first_row) & (elane < N_EXPERTS), 1, 0)
        expert = jnp.minimum(jnp.sum(done, axis=1, keepdims=True), N_EXPERTS - 1)
        total = jnp.max(ends[0:1, :], axis=1, keepdims=True)
        live = jnp.where(first_row < total, 1, 0)
        tile_ref[...] = jnp.where(elane == 0, expert, jnp.where(elane == 1, live, 0))
        seg_ref[...] = jnp.concatenate([ends, padded], axis=0)


def _pos(eid, rank, counts, n_tiles):
    t = eid.shape[0]
    tm = SORT_ROWS
    row = pl.BlockSpec((tm, LANES), lambda i: (i, 0))
    return pl.pallas_call(
        _pos_kernel,
        grid=(t // tm,),
        in_specs=[row, row, pl.BlockSpec((8, LANES), lambda i: (0, 0))],
        out_specs=[row, pl.BlockSpec((n_tiles, LANES), lambda i: (0, 0)),
                   pl.BlockSpec((16, LANES), lambda i: (0, 0))],
        out_shape=[jax.ShapeDtypeStruct((t, LANES), I32),
                   jax.ShapeDtypeStruct((n_tiles, LANES), I32),
                   jax.ShapeDtypeStruct((16, LANES), I32)],
        compiler_params=_params(1),
        name="pos",
    )(eid, rank, counts)


def _row_copy(src, src_row, dst, dst_row, sem):
    return pltpu.make_async_copy(src.at[pl.ds(src_row, 1)], dst.at[pl.ds(dst_row, 1)], sem)


def _dispatch_kernel(seg_end_ref, seg_len_ref, pos_ref, h2_ref, xs_ref, zeros, sem):
    i = pl.program_id(0)
    tm = MOVE_ROWS

    @pl.when(i == 0)
    def _():
        zeros[...] = jnp.zeros_like(zeros)
        for e in range(N_EXPERTS):
            @pl.when(seg_len_ref[e] > 0)
            def _():
                start = pl.multiple_of(seg_end_ref[e] - EXPERT_ROWS, EXPERT_ROWS)
                pltpu.make_async_copy(zeros, xs_ref.at[pl.ds(start, EXPERT_ROWS)], sem).start()
        for e in range(N_EXPERTS):
            @pl.when(seg_len_ref[e] > 0)
            def _():
                pltpu.make_async_copy(zeros, xs_ref.at[pl.ds(0, EXPERT_ROWS)], sem).wait()

        first_unused = seg_end_ref[N_EXPERTS - 1] // EXPERT_ROWS
        n_tiles = xs_ref.shape[0] // EXPERT_ROWS

        def clear(tile, _):
            start = pl.multiple_of(tile * EXPERT_ROWS, EXPERT_ROWS)
            pltpu.make_async_copy(zeros, xs_ref.at[pl.ds(start, EXPERT_ROWS)], sem).start()
            return 0

        def clear_done(tile, _):
            pltpu.make_async_copy(zeros, xs_ref.at[pl.ds(0, EXPERT_ROWS)], sem).wait()
            return 0

        lax.fori_loop(first_unused, n_tiles, clear, 0)
        lax.fori_loop(first_unused, n_tiles, clear_done, 0)

    def issue(r, _):
        tok = i * tm + r
        _row_copy(h2_ref, tok, xs_ref, pos_ref[2 * r], sem).start()
        _row_copy(h2_ref, tok, xs_ref, pos_ref[2 * r + 1], sem).start()
        return 0

    lax.fori_loop(0, tm, issue, 0, unroll=8)

    def drain(r, _):
        _row_copy(h2_ref, 0, xs_ref, 0, sem).wait()
        return 0

    lax.fori_loop(0, 2 * tm, drain, 0, unroll=8)


def _dispatch(seg_end, seg_len, pos_flat, h2, n_rows):
    t, d = h2.shape
    tm = MOVE_ROWS
    grid_spec = pltpu.PrefetchScalarGridSpec(
        num_scalar_prefetch=2,
        grid=(t // tm,),
        in_specs=[pl.BlockSpec((2 * tm,), lambda i, *_: (i,), memory_space=pltpu.SMEM),
                  pl.BlockSpec(memory_space=pl.ANY)],
        out_specs=pl.BlockSpec(memory_space=pl.ANY),
        scratch_shapes=[pltpu.VMEM((EXPERT_ROWS, d), F32), pltpu.SemaphoreType.DMA(())],
    )
    return pl.pallas_call(
        _dispatch_kernel,
        grid_spec=grid_spec,
        out_shape=jax.ShapeDtypeStruct((n_rows, d), F32),
        compiler_params=pltpu.CompilerParams(dimension_semantics=("arbitrary",),
                                             vmem_limit_bytes=VMEM_LIMIT),
        name="dispatch",
    )(seg_end, seg_len, pos_flat, h2)


def _expert_kernel(tile_expert_ref, tile_live_ref, xs_ref, wg_ref, wu_ref, wd_ref, o_ref, wg, wu, wd):
    i = pl.program_id(0)

    @pl.when(tile_live_ref[i] > 0)
    def _():
        changed = jnp.logical_or(i == 0, tile_expert_ref[i] != tile_expert_ref[jnp.maximum(i - 1, 0)])

        @pl.when(changed)
        def _():
            wg[...] = wg_ref[0].astype(BF16)
            wu[...] = wu_ref[0].astype(BF16)
            wd[...] = wd_ref[0].astype(BF16)

        x = xs_ref[...].astype(BF16)
        gate = _dot(x, wg[...])
        up = _dot(x, wu[...])
        hid = (gate * jax.nn.sigmoid(gate)) * up
        o_ref[...] = _dot(hid.astype(BF16), wd[...])

    @pl.when(tile_live_ref[i] == 0)
    def _():
        o_ref[...] = jnp.zeros_like(o_ref)


def _experts(tile_expert, tile_live, xs, w_gate, w_up, w_down):
    n_rows, d = xs.shape
    de = w_gate.shape[2]
    tm = EXPERT_ROWS
    grid_spec = pltpu.PrefetchScalarGridSpec(
        num_scalar_prefetch=2,
        grid=(n_rows // tm,),
        in_specs=[pl.BlockSpec((tm, d), lambda i, te, tl: (i, 0)),
                  pl.BlockSpec((1, d, de), lambda i, te, tl: (te[i], 0, 0)),
                  pl.BlockSpec((1, d, de), lambda i, te, tl: (te[i], 0, 0)),
                  pl.BlockSpec((1, de, d), lambda i, te, tl: (te[i], 0, 0))],
        out_specs=pl.BlockSpec((tm, d), lambda i, te, tl: (i, 0)),
        scratch_shapes=[pltpu.VMEM((d, de), BF16), pltpu.VMEM((d, de), BF16), pltpu.VMEM((de, d), BF16)],
    )
    return pl.pallas_call(
        _expert_kernel,
        grid_spec=grid_spec,
        out_shape=jax.ShapeDtypeStruct((n_rows, d), F32),
        compiler_params=_params(1),
        name="experts",
    )(tile_expert, tile_live, xs, w_gate, w_up, w_down)


def _combine_kernel(pos_ref, pos_next_ref, ys_ref, x1_ref, ew_ref, mod_ref, g_ref, o_ref, buf, sem, *,
                    d, final_norm):
    i = pl.program_id(0)
    n = pl.num_programs(0)
    tm = MOVE_ROWS

    def fetch(pref, slot):
        def issue(r, _):
            for k in range(2):
                pltpu.make_async_copy(ys_ref.at[pl.ds(pref[2 * r + k], 1)],
                                      buf.at[slot, k, pl.ds(r, 1)], sem.at[slot]).start()
            return 0
        lax.fori_loop(0, tm, issue, 0, unroll=8)

    @pl.when(i == 0)
    def _():
        fetch(pos_ref, 0)

    @pl.when(i + 1 < n)
    def _():
        fetch(pos_next_ref, (i + 1) % 2)

    slot = i % 2

    def drain(r, _):
        pltpu.make_async_copy(ys_ref.at[pl.ds(0, 1)], buf.at[slot, 0, pl.ds(0, 1)], sem.at[slot]).wait()
        return 0

    lax.fori_loop(0, 2 * tm, drain, 0, unroll=8)

    ew = ew_ref[...]
    y = ew[:, 0:1] * buf[slot, 0] + ew[:, 1:2] * buf[slot, 1]
    g2 = mod_ref[0, :, 5 * d:6 * d]
    x2 = x1_ref[...] + g2 * y
    o_ref[...] = _rms(x2, g_ref[...]) if final_norm else x2


def _combine(pos_flat, ys, x1, ew, mod3, final_g, seq, final_norm):
    t, d = x1.shape
    tm = MOVE_ROWS
    n = t // tm
    per_batch = seq // tm
    return pl.pallas_call(
        functools.partial(_combine_kernel, d=d, final_norm=final_norm),
        grid=(n,),
        in_specs=[pl.BlockSpec((2 * tm,), lambda i: (i,), memory_space=pltpu.SMEM),
                  pl.BlockSpec((2 * tm,), lambda i: (jnp.minimum(i + 1, n - 1),), memory_space=pltpu.SMEM),
                  pl.BlockSpec(memory_space=pl.ANY),
                  pl.BlockSpec((tm, d), lambda i: (i, 0)),
                  pl.BlockSpec((tm, LANES), lambda i: (i, 0)),
                  pl.BlockSpec((1, 1, mod3.shape[2]), lambda i: (i // per_batch, 0, 0)),
                  pl.BlockSpec((1, d), lambda i: (0, 0))],
        out_specs=pl.BlockSpec((tm, d), lambda i: (i, 0)),
        out_shape=jax.ShapeDtypeStruct((t, d), F32),
        scratch_shapes=[pltpu.VMEM((2, 2, tm, d), F32), pltpu.SemaphoreType.DMA((2,))],
        compiler_params=_params(1),
        name="combine",
    )(pos_flat, pos_flat, ys, x1, ew, mod3, final_g)


def _rope_tables(seq):
    inv = 1.0 / (ROPE_THETA ** (jnp.arange(0, HEAD_DIM, 2, dtype=F32) / HEAD_DIM))
    ang = jnp.arange(seq, dtype=F32)[:, None] * inv[None, :]
    cos, sin = jnp.cos(ang), jnp.sin(ang)
    cos_head = jnp.concatenate([cos, cos], axis=1)
    sin_head = jnp.concatenate([-sin, sin], axis=1)
    reps = SEG // HEAD_DIM
    return jnp.tile(cos_head, (1, reps)), jnp.tile(sin_head, (1, reps))


def kernel(x, c, w_ada, b_ada, norm1_g, w_in, lambda_q1, lambda_k1, lambda_q2, lambda_k2,
           diff_subln_g, w_proj_moba, w_proj_diff, w_out, norm2_g, w_group, b_group,
           w_expert, b_expert, w_gate, w_up, w_down, final_g):
    batch, seq, d = x.shape
    depth = w_ada.shape[0]
    t = batch * seq
    assert seq % PROJ_ROWS == 0 and seq % MOBA_BLOCK == 0 and seq // MOBA_BLOCK <= LANES
    assert ATTN_ROWS == MOBA_BLOCK and d % SEG == 0 and t % SORT_ROWS == 0 and batch <= 8
    assert EXPERT_ROWS & (EXPERT_ROWS - 1) == 0
    n_rows = 2 * t + N_EXPERTS * EXPERT_ROWS
    n_tiles = n_rows // EXPERT_ROWS
    cos, sin = _rope_tables(seq)
    c_pad = jnp.zeros((8, d), F32).at[:batch].set(c)
    xf = x.reshape(t, d)
    row = lambda v: v.reshape(1, -1)
    for l in range(depth):
        mod = _ada(c_pad, w_ada[l], row(b_ada[l]))
        mod3 = mod[:batch].reshape(batch, 1, 6 * d)
        w_qkv = w_in[l][:, :N_QKV_SEGS * SEG].astype(BF16)
        w_gates = w_in[l][:, N_QKV_SEGS * SEG:].astype(BF16)
        qm, km, vm, qd, kd, vd, kmean = _proj(xf, mod3, row(norm1_g[l]), w_qkv, cos, sin, seq)
        kmean = kmean.reshape(batch, seq // MOBA_BLOCK, SEG)
        om = _moba(qm, km, vm, kmean, batch, seq)
        lam_init = 0.8 - 0.6 * math.exp(-0.3 * l)
        od = _diff(qd, kd, vd, row(lambda_q1[l]), row(lambda_k1[l]), row(lambda_q2[l]),
                   row(lambda_k2[l]), row(diff_subln_g[l]), batch, seq, lam_init)
        w_router = jnp.zeros((d, LANES), F32)
        w_router = w_router.at[:, :N_GROUPS].set(w_group[l])
        w_router = w_router.at[:, N_GROUPS:N_GROUPS + N_EXPERTS].set(w_expert[l])
        b_router = jnp.zeros((1, LANES), F32)
        b_router = b_router.at[0, :N_GROUPS].set(b_group[l])
        b_router = b_router.at[0, N_GROUPS:N_GROUPS + N_EXPERTS].set(b_expert[l])
        x1, h2, eid, ew = _merge(xf, om, od, mod3, row(norm1_g[l]), row(norm2_g[l]), w_gates,
                                 w_proj_moba[l].astype(BF16), w_proj_diff[l].astype(BF16),
                                 w_out[l].astype(BF16), w_router, b_router, seq)
        rank, counts = _rank(eid)
        pos, tiles, segs = _pos(eid, rank, counts, n_tiles)
        pos_flat = pos[:, :2].reshape(2 * t)
        xs = _dispatch(segs[0, :N_EXPERTS], segs[8, :N_EXPERTS], pos_flat, h2, n_rows)
        ys = _experts(tiles[:, 0], tiles[:, 1], xs, w_gate[l], w_up[l], w_down[l])
        xf = _combine(pos_flat, ys, x1, ew, mod3, row(final_g), seq, final_norm=(l == depth - 1))
    return xf.reshape(batch, seq, d)
```

```python
import functools
import math

import jax
import jax.numpy as jnp
from jax import lax
from jax.experimental import pallas as pl
from jax.experimental.pallas import tpu as pltpu

F32 = jnp.float32
BF16 = jnp.bfloat16
I32 = jnp.int32

LANES = 128
HEAD_DIM = 64
HEADS_PER_VREG = LANES // HEAD_DIM
MOBA_HEADS = 8
MOBA_BLOCK = 256
MOBA_TOPK = 3
DIFF_HEADS = 4
ROPE_THETA = 10000.0
N_GROUPS = 4
EXPERTS_PER_GROUP = 8
N_EXPERTS = N_GROUPS * EXPERTS_PER_GROUP
NORM_EPS = 1e-6
NEG_INF = -1e30
SEG = MOBA_HEADS * HEAD_DIM
N_QKV_SEGS = 6

PROJ_ROWS = 512
ATTN_ROWS = 256
ATTN_Q_ROWS = 1024
MERGE_ROWS = 256
SORT_ROWS = 512
EXPERT_ROWS = 256
MOVE_ROWS = 256
VMEM_LIMIT = 56 * 1024 * 1024


def _params(n_axes, vmem=VMEM_LIMIT):
    return pltpu.CompilerParams(dimension_semantics=("arbitrary",) * n_axes,
                                vmem_limit_bytes=vmem)


def _dot(a, b):
    return jnp.dot(a, b, preferred_element_type=F32)


def _dot_nt(a, b):
    return lax.dot_general(a, b, (((1,), (1,)), ((), ())), preferred_element_type=F32)


def _rms(x, g):
    return x * lax.rsqrt(jnp.mean(x * x, axis=-1, keepdims=True) + NORM_EPS) * g


def _ada_kernel(c_ref, w_ref, b_ref, o_ref):
    c = c_ref[...]
    o_ref[...] = _dot(c * jax.nn.sigmoid(c), w_ref[...]) + b_ref[...]


def _ada(c_pad, w, b):
    rows, d = c_pad.shape
    n = w.shape[1]
    tn = 1536
    return pl.pallas_call(
        _ada_kernel,
        grid=(n // tn,),
        in_specs=[pl.BlockSpec((rows, d), lambda j: (0, 0)),
                  pl.BlockSpec((d, tn), lambda j: (0, j)),
                  pl.BlockSpec((1, tn), lambda j: (0, j))],
        out_specs=pl.BlockSpec((rows, tn), lambda j: (0, j)),
        out_shape=jax.ShapeDtypeStruct((rows, n), F32),
        compiler_params=_params(1),
        name="ada",
    )(c_pad, w, b)


def _rope(x, cos, sin_signed):
    half = HEAD_DIM // 2
    width = x.shape[1]
    lane = lax.broadcasted_iota(I32, x.shape, 1)
    first = (lane & (HEAD_DIM - 1)) < half
    partner = jnp.where(first, pltpu.roll(x, width - half, 1), pltpu.roll(x, half, 1))
    return x * cos + partner * sin_signed


def _proj_kernel(x_ref, mod_ref, g_ref, w_ref, cos_ref, sin_ref,
                 qm_ref, km_ref, vm_ref, qd_ref, kd_ref, vd_ref, kmean_ref, *, d):
    x = x_ref[...]
    sh = mod_ref[0, :, 0:d]
    sc = mod_ref[0, :, d:2 * d]
    h = (_rms(x, g_ref[...]) * (1.0 + sc) + sh).astype(BF16)
    cos = cos_ref[...]
    sin = sin_ref[...]
    scale = HEAD_DIM ** -0.5
    outs = (qm_ref, km_ref, vm_ref, qd_ref, kd_ref, vd_ref)
    n_blk = x.shape[0] // ATTN_ROWS
    for seg, o_ref in enumerate(outs):
        y = _dot(h, w_ref[:, seg * SEG:(seg + 1) * SEG])
        if seg in (0, 1, 3, 4):
            y = _rope(y, cos, sin)
        if seg in (0, 3):
            y = y * scale
        if seg == 1:
            for blk in range(n_blk):
                rows = y[blk * MOBA_BLOCK:(blk + 1) * MOBA_BLOCK]
                kmean_ref[0, blk:blk + 1, :] = jnp.mean(rows, axis=0, keepdims=True)
        if seg in (2, 5):
            for blk in range(n_blk):
                for part in range(SEG // LANES):
                    piece = y[blk * ATTN_ROWS:(blk + 1) * ATTN_ROWS, part * LANES:(part + 1) * LANES]
                    o_ref[blk, part * LANES:(part + 1) * LANES, :] = piece.T.astype(BF16)
        else:
            o_ref[...] = y.astype(BF16)


def _proj(x2, mod3, g, w_qkv, cos, sin, seq):
    t, d = x2.shape
    tm = PROJ_ROWS
    per_batch = seq // tm
    row_spec = pl.BlockSpec((tm, SEG), lambda i: (i, 0))
    tab_spec = pl.BlockSpec((tm, SEG), lambda i: (i % per_batch, 0))
    act = jax.ShapeDtypeStruct((t, SEG), BF16)
    act_t = jax.ShapeDtypeStruct((t // ATTN_ROWS, SEG, ATTN_ROWS), BF16)
    t_spec = pl.BlockSpec((tm // ATTN_ROWS, SEG, ATTN_ROWS), lambda i: (i, 0, 0))
    return pl.pallas_call(
        functools.partial(_proj_kernel, d=d),
        grid=(t // tm,),
        in_specs=[pl.BlockSpec((tm, d), lambda i: (i, 0)),
                  pl.BlockSpec((1, 1, mod3.shape[2]), lambda i: (i // per_batch, 0, 0)),
                  pl.BlockSpec((1, d), lambda i: (0, 0)),
                  pl.BlockSpec(w_qkv.shape, lambda i: (0, 0)),
                  tab_spec, tab_spec],
        out_specs=[row_spec, row_spec, t_spec, row_spec, row_spec, t_spec,
                   pl.BlockSpec((1, tm // MOBA_BLOCK, SEG), lambda i: (i, 0, 0))],
        out_shape=[act, act, act_t, act, act, act_t,
                   jax.ShapeDtypeStruct((t // tm, tm // MOBA_BLOCK, SEG), F32)],
        compiler_params=_params(1),
        name="proj",
    )(x2, mod3, g, w_qkv, cos, sin)


def _flash_first(s, vt):
    m = jnp.max(s, axis=0, keepdims=True)
    p = jnp.exp(s - m)
    l = jnp.sum(p, axis=0, keepdims=True)
    return m, l, _dot(vt, p.astype(BF16))


def _flash_step(s, vt, m, l, acc):
    m_new = jnp.maximum(m, jnp.max(s, axis=0, keepdims=True))
    alpha = jnp.exp(m - m_new)
    p = jnp.exp(s - m_new)
    l = alpha * l + jnp.sum(p, axis=0, keepdims=True)
    return m_new, l, alpha * acc + _dot(vt, p.astype(BF16))


def _attend(qs, keys_of_tile, k_ref, vt_ref, v_rows, q_tile, own_only_rows):
    tq, tk = ATTN_Q_ROWS, ATTN_ROWS
    sub = tq // tk
    key = lax.broadcasted_iota(I32, (tk, tq), 0)
    qry = lax.broadcasted_iota(I32, (tk, tq), 1)

    def tile(j, mask, carry):
        start = pl.multiple_of(j * tk, tk)
        keys = keys_of_tile(j, k_ref[pl.ds(start, tk), :])
        vt = vt_ref[j]
        new = ()
        for n, q in enumerate(qs):
            s = _dot_nt(keys, q)
            if mask is not None:
                s = jnp.where(mask, s, NEG_INF)
            if carry is None:
                new += _flash_first(s, vt[v_rows[n]])
            else:
                new += _flash_step(s, vt[v_rows[n]], *carry[3 * n:3 * n + 3])
        return new

    carry = None
    for b in range(sub):
        keep = key + b * tk <= qry
        if own_only_rows:
            keep = keep | ((qry >> int(math.log2(tk))) != b)
        carry = tile(q_tile * sub + b, keep, carry)
    carry = lax.fori_loop(0, q_tile * sub, lambda j, cr: tile(j, None, cr), carry)
    return [carry[3 * n:3 * n + 3] for n in range(len(qs))]


def _moba_kernel(q_ref, k_ref, v_ref, kmean_ref, o_ref, *, n_blocks):
    tq = ATTN_Q_ROWS
    q_tile = pl.program_id(2)
    q2 = q_ref[...]
    lane = lax.broadcasted_iota(I32, (tq, LANES), 1)
    row = lax.broadcasted_iota(I32, (tq, LANES), 0)
    own_block = q_tile * (tq // MOBA_BLOCK) + (row >> int(math.log2(MOBA_BLOCK)))
    head_of_lane = lane >> 6
    kmean = kmean_ref[0]
    km_lane = lax.broadcasted_iota(I32, kmean.shape, 1) >> 6
    pad = jnp.zeros((LANES - n_blocks, LANES), F32)
    lane_k = lax.broadcasted_iota(I32, (ATTN_ROWS, LANES), 1)
    lhs = []
    for head in range(HEADS_PER_VREG):
        qh = jnp.where(head_of_lane == head, q2, jnp.zeros_like(q2))
        km = jnp.concatenate([jnp.where(km_lane == head, kmean, 0.0), pad], axis=0)
        km_hi = km.astype(BF16)
        km_lo = (km - km_hi.astype(F32)).astype(BF16)
        gate = _dot_nt(qh, km_hi) + _dot_nt(qh, km_lo)
        gate = jnp.where(lane < own_block, gate, NEG_INF)
        bias = jnp.where(lane == own_block, 0.0, NEG_INF)
        for _ in range(min(MOBA_TOPK, n_blocks)):
            top = jnp.max(gate, axis=1, keepdims=True)
            first = jnp.min(jnp.where(gate == top, lane, LANES), axis=1, keepdims=True)
            picked = lane == first
            bias = jnp.where(picked & (lane < own_block), 0.0, bias)
            gate = jnp.where(picked, -jnp.inf, gate)
        lhs.append(jnp.concatenate([qh, bias.astype(BF16)], axis=1))

    def rhs_of_tile(j, keys):
        return jnp.concatenate([keys, jnp.where(lane_k == j, 1.0, 0.0).astype(BF16)], axis=1)

    v_rows = [slice(h * HEAD_DIM, (h + 1) * HEAD_DIM) for h in range(HEADS_PER_VREG)]
    stats = _attend(lhs, rhs_of_tile, k_ref, v_ref, v_rows, q_tile, True)
    out_t = jnp.concatenate([acc / l for _, l, acc in stats], axis=0)
    o_ref[...] = out_t.T.astype(BF16)


def _moba(qm, km, vm_t, kmean, batch, seq):
    t = qm.shape[0]
    tq = ATTN_Q_ROWS
    nq = seq // tq
    n_blocks = seq // MOBA_BLOCK
    pairs = MOBA_HEADS // HEADS_PER_VREG
    n_kv = seq // ATTN_ROWS
    return pl.pallas_call(
        functools.partial(_moba_kernel, n_blocks=n_blocks),
        grid=(batch, pairs, nq),
        in_specs=[pl.BlockSpec((tq, LANES), lambda b, p, i: (b * nq + i, p)),
                  pl.BlockSpec((seq, LANES), lambda b, p, i: (b, p)),
                  pl.BlockSpec((n_kv, LANES, ATTN_ROWS), lambda b, p, i: (b, p, 0)),
                  pl.BlockSpec((1, n_blocks, LANES), lambda b, p, i: (b, 0, p))],
        out_specs=pl.BlockSpec((tq, LANES), lambda b, p, i: (b * nq + i, p)),
        out_shape=jax.ShapeDtypeStruct((t, SEG), BF16),
        compiler_params=_params(3),
        name="moba",
    )(qm, km, vm_t, kmean)


def _diff_kernel(q_ref, k_ref, v_ref, lq1_ref, lk1_ref, lq2_ref, lk2_ref, g_ref, o_ref, *, lam_init):
    tq = ATTN_Q_ROWS
    q2 = q_ref[...]
    comp_of_lane = lax.broadcasted_iota(I32, (tq, LANES), 1) >> 6
    qs = [jnp.where(comp_of_lane == c, q2, jnp.zeros_like(q2)) for c in range(2)]
    v_rows = [slice(0, LANES)] * 2
    (m1, l1, a1), (m2, l2, a2) = _attend(qs, lambda j, keys: keys, k_ref, v_ref, v_rows,
                                         pl.program_id(2), False)
    lam = (jnp.exp(jnp.sum(lq1_ref[...] * lk1_ref[...], axis=1, keepdims=True))
           - jnp.exp(jnp.sum(lq2_ref[...] * lk2_ref[...], axis=1, keepdims=True)) + lam_init)
    o = (a1 / l1 - lam * (a2 / l2)).T
    o_ref[...] = (_rms(o, g_ref[...]) * (1.0 - lam_init)).astype(BF16)


def _diff(qd, kd, vd_t, lq1, lk1, lq2, lk2, g, batch, seq, lam_init):
    t = qd.shape[0]
    tq = ATTN_Q_ROWS
    nq = seq // tq
    n_kv = seq // ATTN_ROWS
    vec = lambda a: pl.BlockSpec(a.shape, lambda b, h, i: (0, 0))
    return pl.pallas_call(
        functools.partial(_diff_kernel, lam_init=lam_init),
        grid=(batch, DIFF_HEADS, nq),
        in_specs=[pl.BlockSpec((tq, LANES), lambda b, h, i: (b * nq + i, h)),
                  pl.BlockSpec((seq, LANES), lambda b, h, i: (b, h)),
                  pl.BlockSpec((n_kv, LANES, ATTN_ROWS), lambda b, h, i: (b, h, 0)),
                  vec(lq1), vec(lk1), vec(lq2), vec(lk2), vec(g)],
        out_specs=pl.BlockSpec((tq, LANES), lambda b, h, i: (b * nq + i, h)),
        out_shape=jax.ShapeDtypeStruct((t, SEG), BF16),
        compiler_params=_params(3),
        name="diff",
    )(qd, kd, vd_t, lq1, lk1, lq2, lk2, g)


def _merge_kernel(x_ref, om_ref, od_ref, mod_ref, g1n_ref, g2n_ref, wg_ref, wpm_ref, wpd_ref,
                  wout_ref, wr_ref, br_ref, x1_ref, h2_ref, eid_ref, ew_ref, merged, *, d):
    x = x_ref[...]
    sh1 = mod_ref[0, :, 0:d]
    sc1 = mod_ref[0, :, d:2 * d]
    g1 = mod_ref[0, :, 2 * d:3 * d]
    sh2 = mod_ref[0, :, 3 * d:4 * d]
    sc2 = mod_ref[0, :, 4 * d:5 * d]
    h = (_rms(x, g1n_ref[...]) * (1.0 + sc1) + sh1).astype(BF16)
    om = om_ref[...]
    od = od_ref[...]
    chunk = SEG
    for c in range(d // chunk):
        cols = slice(c * chunk, (c + 1) * chunk)
        gm = _dot(h, wg_ref[:, c * chunk:(c + 1) * chunk])
        gd = _dot(h, wg_ref[:, d + c * chunk:d + (c + 1) * chunk])
        ym = _dot(om, wpm_ref[:, cols])
        yd = _dot(od, wpd_ref[:, cols])
        merged[:, cols] = (jax.nn.sigmoid(gm) * ym + jax.nn.sigmoid(gd) * yd).astype(BF16)
    x1 = x + g1 * _dot(merged[...], wout_ref[...])
    x1_ref[...] = x1
    h2 = _rms(x1, g2n_ref[...]) * (1.0 + sc2) + sh2
    h2_ref[...] = h2

    logits = jnp.dot(h2, wr_ref[...], preferred_element_type=F32,
                     precision=lax.Precision.HIGHEST) + br_ref[...]
    lane = lax.broadcasted_iota(I32, logits.shape, 1)
    is_group = lane < N_GROUPS
    gl = jnp.where(is_group, logits, -jnp.inf)
    gmax = jnp.max(gl, axis=1, keepdims=True)
    gexp = jnp.exp(gl - gmax)
    g_w = 1.0 / jnp.sum(gexp, axis=1, keepdims=True)
    g_idx = jnp.min(jnp.where(gl == gmax, lane, LANES), axis=1, keepdims=True)
    e_lane = lane - N_GROUPS
    in_group = (e_lane >= g_idx * EXPERTS_PER_GROUP) & (e_lane < (g_idx + 1) * EXPERTS_PER_GROUP)
    el = jnp.where(in_group, logits, -jnp.inf)
    emax = jnp.max(el, axis=1, keepdims=True)
    eexp = jnp.exp(el - emax)
    prob = eexp / jnp.sum(eexp, axis=1, keepdims=True)
    prob = jnp.where(in_group, prob, -1.0)
    p1 = jnp.max(prob, axis=1, keepdims=True)
    i1 = jnp.min(jnp.where(prob == p1, lane, LANES), axis=1, keepdims=True)
    prob2 = jnp.where(lane == i1, -1.0, prob)
    p2 = jnp.max(prob2, axis=1, keepdims=True)
    i2 = jnp.min(jnp.where(prob2 == p2, lane, LANES), axis=1, keepdims=True)
    tot = p1 + p2
    eid_ref[...] = jnp.where(lane == 0, i1 - N_GROUPS, jnp.where(lane == 1, i2 - N_GROUPS, 0))
    ew_ref[...] = jnp.where(lane == 0, g_w * (p1 / tot), jnp.where(lane == 1, g_w * (p2 / tot), 0.0))


def _merge(x2, om, od, mod3, g1n, g2n, wg, wpm, wpd, wout, wr, br, seq):
    t, d = x2.shape
    tm = MERGE_ROWS
    per_batch = seq // tm
    full = lambda a: pl.BlockSpec(a.shape, lambda i: (0,) * a.ndim)
    row = lambda w: pl.BlockSpec((tm, w), lambda i: (i, 0))
    return pl.pallas_call(
        functools.partial(_merge_kernel, d=d),
        grid=(t // tm,),
        in_specs=[row(d), row(SEG), row(SEG),
                  pl.BlockSpec((1, 1, mod3.shape[2]), lambda i: (i // per_batch, 0, 0)),
                  full(g1n), full(g2n), full(wg), full(wpm), full(wpd), full(wout), full(wr), full(br)],
        out_specs=[row(d), row(d), row(LANES), row(LANES)],
        out_shape=[jax.ShapeDtypeStruct((t, d), F32), jax.ShapeDtypeStruct((t, d), F32),
                   jax.ShapeDtypeStruct((t, LANES), I32), jax.ShapeDtypeStruct((t, LANES), F32)],
        scratch_shapes=[pltpu.VMEM((tm, d), BF16)],
        compiler_params=_params(1),
        name="merge",
    )(x2, om, od, mod3, g1n, g2n, wg, wpm, wpd, wout, wr, br)


def _onehots(eid):
    lane = lax.broadcasted_iota(I32, eid.shape, 1)
    return lane, lane == eid[:, 0:1], lane == eid[:, 1:2]


def _rank_kernel(eid_ref, rank_ref, counts_ref, carry):
    tm = eid_ref.shape[0]

    @pl.when(pl.program_id(0) == 0)
    def _():
        carry[...] = jnp.zeros_like(carry)

    lane, oh0, oh1 = _onehots(eid_ref[...])
    used = jnp.where(oh0 | oh1, 1.0, 0.0)
    r = lax.broadcasted_iota(I32, (tm, tm), 0)
    c = lax.broadcasted_iota(I32, (tm, tm), 1)
    before = jnp.where(c < r, 1.0, 0.0).astype(BF16)
    base = carry[0:1, :] + _dot(before, used.astype(BF16))
    r0 = jnp.sum(jnp.where(oh0, base, 0.0), axis=1, keepdims=True)
    r1 = jnp.sum(jnp.where(oh1, base, 0.0), axis=1, keepdims=True)
    rank_ref[...] = jnp.where(lane == 0, r0, jnp.where(lane == 1, r1, 0.0)).astype(I32)
    carry[...] = carry[...] + jnp.sum(used, axis=0, keepdims=True)
    counts_ref[...] = carry[...].astype(I32)


def _rank(eid):
    t = eid.shape[0]
    tm = SORT_ROWS
    return pl.pallas_call(
        _rank_kernel,
        grid=(t // tm,),
        in_specs=[pl.BlockSpec((tm, LANES), lambda i: (i, 0))],
        out_specs=[pl.BlockSpec((tm, LANES), lambda i: (i, 0)),
                   pl.BlockSpec((8, LANES), lambda i: (0, 0))],
        out_shape=[jax.ShapeDtypeStruct((t, LANES), I32), jax.ShapeDtypeStruct((8, LANES), I32)],
        scratch_shapes=[pltpu.VMEM((8, LANES), F32)],
        compiler_params=_params(1),
        name="rank",
    )(eid)


def _segment_ends(counts):
    lane = lax.broadcasted_iota(I32, counts.shape, 1)
    padded = (counts + (EXPERT_ROWS - 1)) & (-EXPERT_ROWS)
    padded = jnp.where(lane < N_EXPERTS, padded, 0)
    ends = padded
    shift = 1
    while shift < N_EXPERTS:
        ends = ends + jnp.where(lane >= shift, pltpu.roll(ends, shift, 1), 0)
        shift *= 2
    return padded, ends


def _pos_kernel(eid_ref, rank_ref, counts_ref, pos_ref, tile_ref, seg_ref):
    padded, ends = _segment_ends(counts_ref[...])
    starts = (ends - padded)[0:1, :]
    lane, oh0, oh1 = _onehots(eid_ref[...])
    rank = rank_ref[...]
    p0 = rank[:, 0:1] + jnp.sum(jnp.where(oh0, starts, 0), axis=1, keepdims=True)
    p1 = rank[:, 1:2] + jnp.sum(jnp.where(oh1, starts, 0), axis=1, keepdims=True)
    pos_ref[...] = jnp.where(lane == 0, p0, jnp.where(lane == 1, p1, 0))

    @pl.when(pl.program_id(0) == 0)
    def _():
        n_tiles = tile_ref.shape[0]
        first_row = lax.broadcasted_iota(I32, (n_tiles, LANES), 0) * EXPERT_ROWS
        elane = lax.broadcasted_iota(I32, (n_tiles, LANES), 1)
        done = jnp.where((ends[0:1, :] <= first_row) & (elane < N_EXPERTS), 1, 0)
        expert = jnp.minimum(jnp.sum(done, axis=1, keepdims=True), N_EXPERTS - 1)
        total = jnp.max(ends[0:1, :], axis=1, keepdims=True)
        live = jnp.where(first_row < total, 1, 0)
        tile_ref[...] = jnp.where(elane == 0, expert, jnp.where(elane == 1, live, 0))
        seg_ref[...] = jnp.concatenate([ends, padded], axis=0)


def _pos(eid, rank, counts, n_tiles):
    t = eid.shape[0]
    tm = SORT_ROWS
    row = pl.BlockSpec((tm, LANES), lambda i: (i, 0))
    return pl.pallas_call(
        _pos_kernel,
        grid=(t // tm,),
        in_specs=[row, row, pl.BlockSpec((8, LANES), lambda i: (0, 0))],
        out_specs=[row, pl.BlockSpec((n_tiles, LANES), lambda i: (0, 0)),
                   pl.BlockSpec((16, LANES), lambda i: (0, 0))],
        out_shape=[jax.ShapeDtypeStruct((t, LANES), I32),
                   jax.ShapeDtypeStruct((n_tiles, LANES), I32),
                   jax.ShapeDtypeStruct((16, LANES), I32)],
        compiler_params=_params(1),
        name="pos",
    )(eid, rank, counts)


def _row_copy(src, src_row, dst, dst_row, sem):
    return pltpu.make_async_copy(src.at[pl.ds(src_row, 1)], dst.at[pl.ds(dst_row, 1)], sem)


def _dispatch_kernel(seg_end_ref, seg_len_ref, pos_ref, h2_ref, xs_ref, zeros, sem):
    i = pl.program_id(0)
    tm = MOVE_ROWS

    @pl.when(i == 0)
    def _():
        zeros[...] = jnp.zeros_like(zeros)
        for e in range(N_EXPERTS):
            @pl.when(seg_len_ref[e] > 0)
            def _():
                start = pl.multiple_of(seg_end_ref[e] - EXPERT_ROWS, EXPERT_ROWS)
                pltpu.make_async_copy(zeros, xs_ref.at[pl.ds(start, EXPERT_ROWS)], sem).start()
        for e in range(N_EXPERTS):
            @pl.when(seg_len_ref[e] > 0)
            def _():
                pltpu.make_async_copy(zeros, xs_ref.at[pl.ds(0, EXPERT_ROWS)], sem).wait()

        first_unused = seg_end_ref[N_EXPERTS - 1] // EXPERT_ROWS
        n_tiles = xs_ref.shape[0] // EXPERT_ROWS

        def clear(tile, _):
            start = pl.multiple_of(tile * EXPERT_ROWS, EXPERT_ROWS)
            pltpu.make_async_copy(zeros, xs_ref.at[pl.ds(start, EXPERT_ROWS)], sem).start()
            return 0

        def clear_done(tile, _):
            pltpu.make_async_copy(zeros, xs_ref.at[pl.ds(0, EXPERT_ROWS)], sem).wait()
            return 0

        lax.fori_loop(first_unused, n_tiles, clear, 0)
        lax.fori_loop(first_unused, n_tiles, clear_done, 0)

    def issue(r, _):
        _row_copy(h2_ref, r, xs_ref, pos_ref[2 * r], sem).start()
        _row_copy(h2_ref, r, xs_ref, pos_ref[2 * r + 1], sem).start()
        return 0

    lax.fori_loop(0, tm, issue, 0, unroll=8)

    def drain(r, _):
        _row_copy(h2_ref, 0, xs_ref, 0, sem).wait()
        return 0

    lax.fori_loop(0, 2 * tm, drain, 0, unroll=8)


def _dispatch(seg_end, seg_len, pos_flat, h2, n_rows):
    t, d = h2.shape
    tm = MOVE_ROWS
    grid_spec = pltpu.PrefetchScalarGridSpec(
        num_scalar_prefetch=2,
        grid=(t // tm,),
        in_specs=[pl.BlockSpec((2 * tm,), lambda i, *_: (i,), memory_space=pltpu.SMEM),
                  pl.BlockSpec((tm, d), lambda i, *_: (i, 0))],
        out_specs=pl.BlockSpec(memory_space=pl.ANY),
        scratch_shapes=[pltpu.VMEM((EXPERT_ROWS, d), F32), pltpu.SemaphoreType.DMA(())],
    )
    return pl.pallas_call(
        _dispatch_kernel,
        grid_spec=grid_spec,
        out_shape=jax.ShapeDtypeStruct((n_rows, d), F32),
        compiler_params=pltpu.CompilerParams(dimension_semantics=("arbitrary",),
                                             vmem_limit_bytes=VMEM_LIMIT),
        name="dispatch",
    )(seg_end, seg_len, pos_flat, h2)


def _expert_kernel(tile_expert_ref, tile_live_ref, xs_ref, wg_ref, wu_ref, wd_ref, o_ref, wg, wu, wd):
    i = pl.program_id(0)

    @pl.when(tile_live_ref[i] > 0)
    def _():
        changed = jnp.logical_or(i == 0, tile_expert_ref[i] != tile_expert_ref[jnp.maximum(i - 1, 0)])

        @pl.when(changed)
        def _():
            wg[...] = wg_ref[0].astype(BF16)
            wu[...] = wu_ref[0].astype(BF16)
            wd[...] = wd_ref[0].astype(BF16)

        x = xs_ref[...].astype(BF16)
        gate = _dot(x, wg[...])
        up = _dot(x, wu[...])
        hid = (gate * jax.nn.sigmoid(gate)) * up
        o_ref[...] = _dot(hid.astype(BF16), wd[...])

    @pl.when(tile_live_ref[i] == 0)
    def _():
        o_ref[...] = jnp.zeros_like(o_ref)


def _experts(tile_expert, tile_live, xs, w_gate, w_up, w_down):
    n_rows, d = xs.shape
    de = w_gate.shape[2]
    tm = EXPERT_ROWS
    grid_spec = pltpu.PrefetchScalarGridSpec(
        num_scalar_prefetch=2,
        grid=(n_rows // tm,),
        in_specs=[pl.BlockSpec((tm, d), lambda i, te, tl: (i, 0)),
                  pl.BlockSpec((1, d, de), lambda i, te, tl: (te[i], 0, 0)),
                  pl.BlockSpec((1, d, de), lambda i, te, tl: (te[i], 0, 0)),
                  pl.BlockSpec((1, de, d), lambda i, te, tl: (te[i], 0, 0))],
        out_specs=pl.BlockSpec((tm, d), lambda i, te, tl: (i, 0)),
        scratch_shapes=[pltpu.VMEM((d, de), BF16), pltpu.VMEM((d, de), BF16), pltpu.VMEM((de, d), BF16)],
    )
    return pl.pallas_call(
        _expert_kernel,
        grid_spec=grid_spec,
        out_shape=jax.ShapeDtypeStruct((n_rows, d), F32),
        compiler_params=_params(1),
        name="experts",
    )(tile_expert, tile_live, xs, w_gate, w_up, w_down)


def _combine_kernel(pos_ref, pos_next_ref, ys_ref, x1_ref, ew_ref, mod_ref, g_ref, o_ref, buf, sem, *,
                    d, final_norm):
    i = pl.program_id(0)
    n = pl.num_programs(0)
    tm = MOVE_ROWS

    def fetch(pref, slot):
        def issue(r, _):
            for k in range(2):
                pltpu.make_async_copy(ys_ref.at[pl.ds(pref[2 * r + k], 1)],
                                      buf.at[slot, k, pl.ds(r, 1)], sem.at[slot]).start()
            return 0
        lax.fori_loop(0, tm, issue, 0, unroll=8)

    @pl.when(i == 0)
    def _():
        fetch(pos_ref, 0)

    @pl.when(i + 1 < n)
    def _():
        fetch(pos_next_ref, (i + 1) % 2)

    slot = i % 2

    def drain(r, _):
        pltpu.make_async_copy(ys_ref.at[pl.ds(0, 1)], buf.at[slot, 0, pl.ds(0, 1)], sem.at[slot]).wait()
        return 0

    lax.fori_loop(0, 2 * tm, drain, 0, unroll=8)

    ew = ew_ref[...]
    y = ew[:, 0:1] * buf[slot, 0] + ew[:, 1:2] * buf[slot, 1]
    g2 = mod_ref[0, :, 5 * d:6 * d]
    x2 = x1_ref[...] + g2 * y
    o_ref[...] = _rms(x2, g_ref[...]) if final_norm else x2


def _combine(pos_flat, ys, x1, ew, mod3, final_g, seq, final_norm):
    t, d = x1.shape
    tm = MOVE_ROWS
    n = t // tm
    per_batch = seq // tm
    return pl.pallas_call(
        functools.partial(_combine_kernel, d=d, final_norm=final_norm),
        grid=(n,),
        in_specs=[pl.BlockSpec((2 * tm,), lambda i: (i,), memory_space=pltpu.SMEM),
                  pl.BlockSpec((2 * tm,), lambda i: (jnp.minimum(i + 1, n - 1),), memory_space=pltpu.SMEM),
                  pl.BlockSpec(memory_space=pl.ANY),
                  pl.BlockSpec((tm, d), lambda i: (i, 0)),
                  pl.BlockSpec((tm, LANES), lambda i: (i, 0)),
                  pl.BlockSpec((1, 1, mod3.shape[2]), lambda i: (i // per_batch, 0, 0)),
                  pl.BlockSpec((1, d), lambda i: (0, 0))],
        out_specs=pl.BlockSpec((tm, d), lambda i: (i, 0)),
        out_shape=jax.ShapeDtypeStruct((t, d), F32),
        scratch_shapes=[pltpu.VMEM((2, 2, tm, d), F32), pltpu.SemaphoreType.DMA((2,))],
        compiler_params=_params(1),
        name="combine",
    )(pos_flat, pos_flat, ys, x1, ew, mod3, final_g)


def _rope_tables(seq):
    inv = 1.0 / (ROPE_THETA ** (jnp.arange(0, HEAD_DIM, 2, dtype=F32) / HEAD_DIM))
    ang = jnp.arange(seq, dtype=F32)[:, None] * inv[None, :]
    cos, sin = jnp.cos(ang), jnp.sin(ang)
    cos_head = jnp.concatenate([cos, cos], axis=1)
    sin_head = jnp.concatenate([-sin, sin], axis=1)
    reps = SEG // HEAD_DIM
    return jnp.tile(cos_head, (1, reps)), jnp.tile(sin_head, (1, reps))


def kernel(x, c, w_ada, b_ada, norm1_g, w_in, lambda_q1, lambda_k1, lambda_q2, lambda_k2,
           diff_subln_g, w_proj_moba, w_proj_diff, w_out, norm2_g, w_group, b_group,
           w_expert, b_expert, w_gate, w_up, w_down, final_g):
    batch, seq, d = x.shape
    depth = w_ada.shape[0]
    t = batch * seq
    assert seq % PROJ_ROWS == 0 and seq % MOBA_BLOCK == 0 and seq // MOBA_BLOCK <= LANES
    assert ATTN_ROWS == MOBA_BLOCK and d % SEG == 0 and t % SORT_ROWS == 0 and batch <= 8
    assert seq % ATTN_Q_ROWS == 0 and ATTN_Q_ROWS % ATTN_ROWS == 0
    assert EXPERT_ROWS & (EXPERT_ROWS - 1) == 0
    n_rows = 2 * t + N_EXPERTS * EXPERT_ROWS
    n_tiles = n_rows // EXPERT_ROWS
    cos, sin = _rope_tables(seq)
    c_pad = jnp.zeros((8, d), F32).at[:batch].set(c)
    xf = x.reshape(t, d)
    row = lambda v: v.reshape(1, -1)
    for l in range(depth):
        mod = _ada(c_pad, w_ada[l], row(b_ada[l]))
        mod3 = mod[:batch].reshape(batch, 1, 6 * d)
        w_qkv = w_in[l][:, :N_QKV_SEGS * SEG].astype(BF16)
        w_gates = w_in[l][:, N_QKV_SEGS * SEG:].astype(BF16)
        qm, km, vm, qd, kd, vd, kmean = _proj(xf, mod3, row(norm1_g[l]), w_qkv, cos, sin, seq)
        kmean = kmean.reshape(batch, seq // MOBA_BLOCK, SEG)
        om = _moba(qm, km, vm, kmean, batch, seq)
        lam_init = 0.8 - 0.6 * math.exp(-0.3 * l)
        od = _diff(qd, kd, vd, row(lambda_q1[l]), row(lambda_k1[l]), row(lambda_q2[l]),
                   row(lambda_k2[l]), row(diff_subln_g[l]), batch, seq, lam_init)
        w_router = jnp.zeros((d, LANES), F32)
        w_router = w_router.at[:, :N_GROUPS].set(w_group[l])
        w_router = w_router.at[:, N_GROUPS:N_GROUPS + N_EXPERTS].set(w_expert[l])
        b_router = jnp.zeros((1, LANES), F32)
        b_router = b_router.at[0, :N_GROUPS].set(b_group[l])
        b_router = b_router.at[0, N_GROUPS:N_GROUPS + N_EXPERTS].set(b_expert[l])
        x1, h2, eid, ew = _merge(xf, om, od, mod3, row(norm1_g[l]), row(norm2_g[l]), w_gates,
                                 w_proj_moba[l].astype(BF16), w_proj_diff[l].astype(BF16),
                                 w_out[l].astype(BF16), w_router, b_router, seq)
        rank, counts = _rank(eid)
        pos, tiles, segs = _pos(eid, rank, counts, n_tiles)
        pos_flat = pos[:, :2].reshape(2 * t)
        xs = _dispatch(segs[0, :N_EXPERTS], segs[8, :N_EXPERTS], pos_flat, h2, n_rows)
        ys = _experts(tiles[:, 0], tiles[:, 1], xs, w_gate[l], w_up[l], w_down[l])
        xf = _combine(pos_flat, ys, x1, ew, mod3, row(final_g), seq, final_norm=(l == depth - 1))
    return xf.reshape(batch, seq, d)
```

```python
import functools
import math

import jax
import jax.numpy as jnp
from jax import lax
from jax.experimental import pallas as pl
from jax.experimental.pallas import tpu as pltpu

F32 = jnp.float32
BF16 = jnp.bfloat16
I32 = jnp.int32

LANES = 128
HEAD_DIM = 64
HEADS_PER_VREG = LANES // HEAD_DIM
MOBA_HEADS = 8
MOBA_BLOCK = 256
MOBA_TOPK = 3
DIFF_HEADS = 4
ROPE_THETA = 10000.0
N_GROUPS = 4
EXPERTS_PER_GROUP = 8
N_EXPERTS = N_GROUPS * EXPERTS_PER_GROUP
NORM_EPS = 1e-6
NEG_INF = -1e30
SEG = MOBA_HEADS * HEAD_DIM
N_QKV_SEGS = 6

PROJ_ROWS = 512
ATTN_ROWS = 256
ATTN_Q_ROWS = 1024
MERGE_ROWS = 256
SORT_ROWS = 512
EXPERT_ROWS = 256
MOVE_ROWS = 256
VMEM_LIMIT = 56 * 1024 * 1024


def _params(n_axes, vmem=VMEM_LIMIT):
    return pltpu.CompilerParams(dimension_semantics=("arbitrary",) * n_axes,
                                vmem_limit_bytes=vmem)


def _dot(a, b):
    return jnp.dot(a, b, preferred_element_type=F32)


def _dot_nt(a, b):
    return lax.dot_general(a, b, (((1,), (1,)), ((), ())), preferred_element_type=F32)


def _rms(x, g):
    return x * lax.rsqrt(jnp.mean(x * x, axis=-1, keepdims=True) + NORM_EPS) * g


def _ada_kernel(c_ref, w_ref, b_ref, o_ref):
    c = c_ref[...]
    o_ref[...] = _dot(c * jax.nn.sigmoid(c), w_ref[...]) + b_ref[...]


def _ada(c_pad, w, b):
    rows, d = c_pad.shape
    n = w.shape[1]
    tn = 1536
    return pl.pallas_call(
        _ada_kernel,
        grid=(n // tn,),
        in_specs=[pl.BlockSpec((rows, d), lambda j: (0, 0)),
                  pl.BlockSpec((d, tn), lambda j: (0, j)),
                  pl.BlockSpec((1, tn), lambda j: (0, j))],
        out_specs=pl.BlockSpec((rows, tn), lambda j: (0, j)),
        out_shape=jax.ShapeDtypeStruct((rows, n), F32),
        compiler_params=_params(1),
        name="ada",
    )(c_pad, w, b)


def _rope(x, cos, sin_signed):
    half = HEAD_DIM // 2
    width = x.shape[1]
    lane = lax.broadcasted_iota(I32, x.shape, 1)
    first = (lane & (HEAD_DIM - 1)) < half
    partner = jnp.where(first, pltpu.roll(x, width - half, 1), pltpu.roll(x, half, 1))
    return x * cos + partner * sin_signed


def _proj_kernel(x_ref, mod_ref, g_ref, w_ref, cos_ref, sin_ref,
                 qm_ref, km_ref, vm_ref, qd_ref, kd_ref, vd_ref, kmean_ref, *, d):
    x = x_ref[...]
    sh = mod_ref[0, :, 0:d]
    sc = mod_ref[0, :, d:2 * d]
    h = (_rms(x, g_ref[...]) * (1.0 + sc) + sh).astype(BF16)
    cos = cos_ref[...]
    sin = sin_ref[...]
    scale = HEAD_DIM ** -0.5
    outs = (qm_ref, km_ref, vm_ref, qd_ref, kd_ref, vd_ref)
    n_blk = x.shape[0] // ATTN_ROWS
    for seg, o_ref in enumerate(outs):
        y = _dot(h, w_ref[:, seg * SEG:(seg + 1) * SEG])
        if seg in (0, 1, 3, 4):
            y = _rope(y, cos, sin)
        if seg in (0, 3):
            y = y * scale
        if seg == 1:
            for blk in range(n_blk):
                rows = y[blk * MOBA_BLOCK:(blk + 1) * MOBA_BLOCK]
                kmean_ref[0, blk:blk + 1, :] = jnp.mean(rows, axis=0, keepdims=True)
        if seg in (2, 5):
            for blk in range(n_blk):
                for part in range(SEG // LANES):
                    piece = y[blk * ATTN_ROWS:(blk + 1) * ATTN_ROWS, part * LANES:(part + 1) * LANES]
                    o_ref[blk, part * LANES:(part + 1) * LANES, :] = piece.T.astype(BF16)
        else:
            o_ref[...] = y.astype(BF16)


def _proj(x2, mod3, g, w_qkv, cos, sin, seq):
    t, d = x2.shape
    tm = PROJ_ROWS
    per_batch = seq // tm
    row_spec = pl.BlockSpec((tm, SEG), lambda i: (i, 0))
    tab_spec = pl.BlockSpec((tm, SEG), lambda i: (i % per_batch, 0))
    act = jax.ShapeDtypeStruct((t, SEG), BF16)
    act_t = jax.ShapeDtypeStruct((t // ATTN_ROWS, SEG, ATTN_ROWS), BF16)
    t_spec = pl.BlockSpec((tm // ATTN_ROWS, SEG, ATTN_ROWS), lambda i: (i, 0, 0))
    return pl.pallas_call(
        functools.partial(_proj_kernel, d=d),
        grid=(t // tm,),
        in_specs=[pl.BlockSpec((tm, d), lambda i: (i, 0)),
                  pl.BlockSpec((1, 1, mod3.shape[2]), lambda i: (i // per_batch, 0, 0)),
                  pl.BlockSpec((1, d), lambda i: (0, 0)),
                  pl.BlockSpec(w_qkv.shape, lambda i: (0, 0)),
                  tab_spec, tab_spec],
        out_specs=[row_spec, row_spec, t_spec, row_spec, row_spec, t_spec,
                   pl.BlockSpec((1, tm // MOBA_BLOCK, SEG), lambda i: (i, 0, 0))],
        out_shape=[act, act, act_t, act, act, act_t,
                   jax.ShapeDtypeStruct((t // tm, tm // MOBA_BLOCK, SEG), F32)],
        compiler_params=_params(1),
        name="proj",
    )(x2, mod3, g, w_qkv, cos, sin)


def _attend(qs, keys_of_tile, k_ref, vt_ref, v_rows, q_tile, own_only_rows, acc_refs):
    tq, tk = ATTN_Q_ROWS, ATTN_ROWS
    sub = tq // tk
    key = lax.broadcasted_iota(I32, (tk, tq), 0)
    qry = lax.broadcasted_iota(I32, (tk, tq), 1)
    for acc_ref in acc_refs:
        acc_ref[...] = jnp.zeros(acc_ref.shape, F32)

    def scores(j):
        start = pl.multiple_of(j * tk, tk)
        keys = keys_of_tile(j, k_ref[pl.ds(start, tk), :])
        return tuple(_dot(keys, q) for q in qs), vt_ref[j]

    def update(tile, mask, stats):
        s_all, vt = tile
        new = ()
        for n, s in enumerate(s_all):
            if mask is not None:
                s = jnp.where(mask, s, NEG_INF)
            m, l = stats[2 * n:2 * n + 2]
            m_new = jnp.maximum(m, jnp.max(s, axis=0, keepdims=True))
            alpha = jnp.exp(m - m_new)
            p = jnp.exp(s - m_new)
            new += (m_new, alpha * l + jnp.sum(p, axis=0, keepdims=True))
            acc_refs[n][...] = alpha * acc_refs[n][...] + _dot(vt[v_rows[n]], p.astype(BF16))
        return new

    stats = (jnp.full((1, tq), NEG_INF, F32), jnp.zeros((1, tq), F32)) * len(qs)
    first_own = q_tile * sub

    def body(group, stats):
        tiles = [scores(sub * group + b) for b in range(sub)]
        for tile in tiles:
            stats = update(tile, None, stats)
        return stats

    stats = lax.fori_loop(0, q_tile, body, stats)
    own = [scores(first_own + b) for b in range(sub)]
    for b in range(sub):
        keep = key + b * tk <= qry
        if own_only_rows:
            keep = keep | ((qry >> int(math.log2(tk))) != b)
        stats = update(own[b], keep, stats)
    return [(stats[2 * n + 1], acc_refs[n][...]) for n in range(len(qs))]


def _softmax_scratch(n_softmax, features):
    return [pltpu.VMEM((features, ATTN_Q_ROWS), F32)] * n_softmax


def _moba_kernel(q_ref, k_ref, v_ref, kmean_ref, o_ref, *acc_refs, n_blocks):
    tq = ATTN_Q_ROWS
    q_tile = pl.program_id(2)
    q_t = q_ref[...].astype(F32).T
    feat = lax.broadcasted_iota(I32, (LANES, tq), 0)
    blk = lax.broadcasted_iota(I32, (n_blocks, tq), 0)
    qry = lax.broadcasted_iota(I32, (n_blocks, tq), 1)
    own_block = q_tile * (tq // MOBA_BLOCK) + (qry >> int(math.log2(MOBA_BLOCK)))
    kmean = kmean_ref[0]
    km_head = lax.broadcasted_iota(I32, kmean.shape, 1) >> 6
    pad = jnp.zeros((LANES - n_blocks, tq), BF16)
    lane_k = lax.broadcasted_iota(I32, (ATTN_ROWS, LANES), 1)
    qs = []
    for head in range(HEADS_PER_VREG):
        qh = jnp.where((feat >> 6) == head, q_t, 0.0).astype(BF16)
        km = jnp.where(km_head == head, kmean, 0.0)
        km_hi = km.astype(BF16)
        km_lo = (km - km_hi.astype(F32)).astype(BF16)
        gate = _dot(km_hi, qh) + _dot(km_lo, qh)
        gate = jnp.where(blk < own_block, gate, NEG_INF)
        bias = jnp.where(blk == own_block, 0.0, NEG_INF)
        for _ in range(min(MOBA_TOPK, n_blocks)):
            top = jnp.max(gate, axis=0, keepdims=True)
            first = jnp.min(jnp.where(gate == top, blk, n_blocks), axis=0, keepdims=True)
            picked = blk == first
            bias = jnp.where(picked & (blk < own_block), 0.0, bias)
            gate = jnp.where(picked, -jnp.inf, gate)
        qs.append(jnp.concatenate([qh, bias.astype(BF16), pad], axis=0))

    def rhs_of_tile(j, keys):
        return jnp.concatenate([keys, jnp.where(lane_k == j, 1.0, 0.0).astype(BF16)], axis=1)

    v_rows = [slice(h * HEAD_DIM, (h + 1) * HEAD_DIM) for h in range(HEADS_PER_VREG)]
    stats = _attend(qs, rhs_of_tile, k_ref, v_ref, v_rows, q_tile, True, acc_refs)
    out_t = jnp.concatenate([acc / l for l, acc in stats], axis=0)
    o_ref[...] = out_t.T.astype(BF16)


def _moba(qm, km, vm_t, kmean, batch, seq):
    t = qm.shape[0]
    tq = ATTN_Q_ROWS
    nq = seq // tq
    n_blocks = seq // MOBA_BLOCK
    pairs = MOBA_HEADS // HEADS_PER_VREG
    n_kv = seq // ATTN_ROWS
    return pl.pallas_call(
        functools.partial(_moba_kernel, n_blocks=n_blocks),
        grid=(batch, pairs, nq),
        in_specs=[pl.BlockSpec((tq, LANES), lambda b, p, i: (b * nq + i, p)),
                  pl.BlockSpec((seq, LANES), lambda b, p, i: (b, p)),
                  pl.BlockSpec((n_kv, LANES, ATTN_ROWS), lambda b, p, i: (b, p, 0)),
                  pl.BlockSpec((1, n_blocks, LANES), lambda b, p, i: (b, 0, p))],
        out_specs=pl.BlockSpec((tq, LANES), lambda b, p, i: (b * nq + i, p)),
        out_shape=jax.ShapeDtypeStruct((t, SEG), BF16),
        scratch_shapes=_softmax_scratch(HEADS_PER_VREG, HEAD_DIM),
        compiler_params=_params(3),
        name="moba",
    )(qm, km, vm_t, kmean)


def _diff_kernel(q_ref, k_ref, v_ref, lq1_ref, lk1_ref, lq2_ref, lk2_ref, g_ref, o_ref,
                 *acc_refs, lam_init):
    tq = ATTN_Q_ROWS
    q_t = q_ref[...].astype(F32).T
    comp = lax.broadcasted_iota(I32, (LANES, tq), 0) >> 6
    qs = [jnp.where(comp == c, q_t, 0.0).astype(BF16) for c in range(2)]
    v_rows = [slice(0, LANES)] * 2
    (l1, a1), (l2, a2) = _attend(qs, lambda j, keys: keys, k_ref, v_ref, v_rows,
                                 pl.program_id(2), False, acc_refs)
    lam = (jnp.exp(jnp.sum(lq1_ref[...] * lk1_ref[...], axis=1, keepdims=True))
           - jnp.exp(jnp.sum(lq2_ref[...] * lk2_ref[...], axis=1, keepdims=True)) + lam_init)
    o = (a1 / l1 - lam * (a2 / l2)).T
    o_ref[...] = (_rms(o, g_ref[...]) * (1.0 - lam_init)).astype(BF16)


def _diff(qd, kd, vd_t, lq1, lk1, lq2, lk2, g, batch, seq, lam_init):
    t = qd.shape[0]
    tq = ATTN_Q_ROWS
    nq = seq // tq
    n_kv = seq // ATTN_ROWS
    vec = lambda a: pl.BlockSpec(a.shape, lambda b, h, i: (0, 0))
    return pl.pallas_call(
        functools.partial(_diff_kernel, lam_init=lam_init),
        grid=(batch, DIFF_HEADS, nq),
        in_specs=[pl.BlockSpec((tq, LANES), lambda b, h, i: (b * nq + i, h)),
                  pl.BlockSpec((seq, LANES), lambda b, h, i: (b, h)),
                  pl.BlockSpec((n_kv, LANES, ATTN_ROWS), lambda b, h, i: (b, h, 0)),
                  vec(lq1), vec(lk1), vec(lq2), vec(lk2), vec(g)],
        out_specs=pl.BlockSpec((tq, LANES), lambda b, h, i: (b * nq + i, h)),
        out_shape=jax.ShapeDtypeStruct((t, SEG), BF16),
        scratch_shapes=_softmax_scratch(2, LANES),
        compiler_params=_params(3),
        name="diff",
    )(qd, kd, vd_t, lq1, lk1, lq2, lk2, g)


def _merge_kernel(x_ref, om_ref, od_ref, mod_ref, g1n_ref, g2n_ref, wg_ref, wpm_ref, wpd_ref,
                  wout_ref, wr_ref, br_ref, x1_ref, h2_ref, eid_ref, ew_ref, merged, *, d):
    x = x_ref[...]
    sh1 = mod_ref[0, :, 0:d]
    sc1 = mod_ref[0, :, d:2 * d]
    g1 = mod_ref[0, :, 2 * d:3 * d]
    sh2 = mod_ref[0, :, 3 * d:4 * d]
    sc2 = mod_ref[0, :, 4 * d:5 * d]
    h = (_rms(x, g1n_ref[...]) * (1.0 + sc1) + sh1).astype(BF16)
    om = om_ref[...]
    od = od_ref[...]
    chunk = SEG
    for c in range(d // chunk):
        cols = slice(c * chunk, (c + 1) * chunk)
        gm = _dot(h, wg_ref[:, c * chunk:(c + 1) * chunk])
        gd = _dot(h, wg_ref[:, d + c * chunk:d + (c + 1) * chunk])
        ym = _dot(om, wpm_ref[:, cols])
        yd = _dot(od, wpd_ref[:, cols])
        merged[:, cols] = (jax.nn.sigmoid(gm) * ym + jax.nn.sigmoid(gd) * yd).astype(BF16)
    x1 = x + g1 * _dot(merged[...], wout_ref[...])
    x1_ref[...] = x1
    h2 = _rms(x1, g2n_ref[...]) * (1.0 + sc2) + sh2
    h2_ref[...] = h2

    logits = jnp.dot(h2, wr_ref[...], preferred_element_type=F32,
                     precision=lax.Precision.HIGHEST) + br_ref[...]
    lane = lax.broadcasted_iota(I32, logits.shape, 1)
    is_group = lane < N_GROUPS
    gl = jnp.where(is_group, logits, -jnp.inf)
    gmax = jnp.max(gl, axis=1, keepdims=True)
    gexp = jnp.exp(gl - gmax)
    g_w = 1.0 / jnp.sum(gexp, axis=1, keepdims=True)
    g_idx = jnp.min(jnp.where(gl == gmax, lane, LANES), axis=1, keepdims=True)
    e_lane = lane - N_GROUPS
    in_group = (e_lane >= g_idx * EXPERTS_PER_GROUP) & (e_lane < (g_idx + 1) * EXPERTS_PER_GROUP)
    el = jnp.where(in_group, logits, -jnp.inf)
    emax = jnp.max(el, axis=1, keepdims=True)
    eexp = jnp.exp(el - emax)
    prob = eexp / jnp.sum(eexp, axis=1, keepdims=True)
    prob = jnp.where(in_group, prob, -1.0)
    p1 = jnp.max(prob, axis=1, keepdims=True)
    i1 = jnp.min(jnp.where(prob == p1, lane, LANES), axis=1, keepdims=True)
    prob2 = jnp.where(lane == i1, -1.0, prob)
    p2 = jnp.max(prob2, axis=1, keepdims=True)
    i2 = jnp.min(jnp.where(prob2 == p2, lane, LANES), axis=1, keepdims=True)
    tot = p1 + p2
    eid_ref[...] = jnp.where(lane == 0, i1 - N_GROUPS, jnp.where(lane == 1, i2 - N_GROUPS, 0))
    ew_ref[...] = jnp.where(lane == 0, g_w * (p1 / tot), jnp.where(lane == 1, g_w * (p2 / tot), 0.0))


def _merge(x2, om, od, mod3, g1n, g2n, wg, wpm, wpd, wout, wr, br, seq):
    t, d = x2.shape
    tm = MERGE_ROWS
    per_batch = seq // tm
    full = lambda a: pl.BlockSpec(a.shape, lambda i: (0,) * a.ndim)
    row = lambda w: pl.BlockSpec((tm, w), lambda i: (i, 0))
    return pl.pallas_call(
        functools.partial(_merge_kernel, d=d),
        grid=(t // tm,),
        in_specs=[row(d), row(SEG), row(SEG),
                  pl.BlockSpec((1, 1, mod3.shape[2]), lambda i: (i // per_batch, 0, 0)),
                  full(g1n), full(g2n), full(wg), full(wpm), full(wpd), full(wout), full(wr), full(br)],
        out_specs=[row(d), row(d), row(LANES), row(LANES)],
        out_shape=[jax.ShapeDtypeStruct((t, d), F32), jax.ShapeDtypeStruct((t, d), F32),
                   jax.ShapeDtypeStruct((t, LANES), I32), jax.ShapeDtypeStruct((t, LANES), F32)],
        scratch_shapes=[pltpu.VMEM((tm, d), BF16)],
        compiler_params=_params(1),
        name="merge",
    )(x2, om, od, mod3, g1n, g2n, wg, wpm, wpd, wout, wr, br)


def _onehots(eid):
    lane = lax.broadcasted_iota(I32, eid.shape, 1)
    return lane, lane == eid[:, 0:1], lane == eid[:, 1:2]


def _rank_kernel(eid_ref, rank_ref, counts_ref, carry):
    tm = eid_ref.shape[0]

    @pl.when(pl.program_id(0) == 0)
    def _():
        carry[...] = jnp.zeros_like(carry)

    lane, oh0, oh1 = _onehots(eid_ref[...])
    used = jnp.where(oh0 | oh1, 1.0, 0.0)
    r = lax.broadcasted_iota(I32, (tm, tm), 0)
    c = lax.broadcasted_iota(I32, (tm, tm), 1)
    before = jnp.where(c < r, 1.0, 0.0).astype(BF16)
    base = carry[0:1, :] + _dot(before, used.astype(BF16))
    r0 = jnp.sum(jnp.where(oh0, base, 0.0), axis=1, keepdims=True)
    r1 = jnp.sum(jnp.where(oh1, base, 0.0), axis=1, keepdims=True)
    rank_ref[...] = jnp.where(lane == 0, r0, jnp.where(lane == 1, r1, 0.0)).astype(I32)
    carry[...] = carry[...] + jnp.sum(used, axis=0, keepdims=True)
    counts_ref[...] = carry[...].astype(I32)


def _rank(eid):
    t = eid.shape[0]
    tm = SORT_ROWS
    return pl.pallas_call(
        _rank_kernel,
        grid=(t // tm,),
        in_specs=[pl.BlockSpec((tm, LANES), lambda i: (i, 0))],
        out_specs=[pl.BlockSpec((tm, LANES), lambda i: (i, 0)),
                   pl.BlockSpec((8, LANES), lambda i: (0, 0))],
        out_shape=[jax.ShapeDtypeStruct((t, LANES), I32), jax.ShapeDtypeStruct((8, LANES), I32)],
        scratch_shapes=[pltpu.VMEM((8, LANES), F32)],
        compiler_params=_params(1),
        name="rank",
    )(eid)


def _segment_ends(counts):
    lane = lax.broadcasted_iota(I32, counts.shape, 1)
    padded = (counts + (EXPERT_ROWS - 1)) & (-EXPERT_ROWS)
    padded = jnp.where(lane < N_EXPERTS, padded, 0)
    ends = padded
    shift = 1
    while shift < N_EXPERTS:
        ends = ends + jnp.where(lane >= shift, pltpu.roll(ends, shift, 1), 0)
        shift *= 2
    return padded, ends


def _pos_kernel(eid_ref, rank_ref, counts_ref, pos_ref, tile_ref, seg_ref):
    padded, ends = _segment_ends(counts_ref[...])
    starts = (ends - padded)[0:1, :]
    lane, oh0, oh1 = _onehots(eid_ref[...])
    rank = rank_ref[...]
    p0 = rank[:, 0:1] + jnp.sum(jnp.where(oh0, starts, 0), axis=1, keepdims=True)
    p1 = rank[:, 1:2] + jnp.sum(jnp.where(oh1, starts, 0), axis=1, keepdims=True)
    pos_ref[...] = jnp.where(lane == 0, p0, jnp.where(lane == 1, p1, 0))

    @pl.when(pl.program_id(0) == 0)
    def _():
        n_tiles = tile_ref.shape[0]
        first_row = lax.broadcasted_iota(I32, (n_tiles, LANES), 0) * EXPERT_ROWS
        elane = lax.broadcasted_iota(I32, (n_tiles, LANES), 1)
        done = jnp.where((ends[0:1, :] <= first_row) & (elane < N_EXPERTS), 1, 0)
        expert = jnp.minimum(jnp.sum(done, axis=1, keepdims=True), N_EXPERTS - 1)
        total = jnp.max(ends[0:1, :], axis=1, keepdims=True)
        live = jnp.where(first_row < total, 1, 0)
        tile_ref[...] = jnp.where(elane == 0, expert, jnp.where(elane == 1, live, 0))
        seg_ref[...] = jnp.concatenate([ends, padded], axis=0)


def _pos(eid, rank, counts, n_tiles):
    t = eid.shape[0]
    tm = SORT_ROWS
    row = pl.BlockSpec((tm, LANES), lambda i: (i, 0))
    return pl.pallas_call(
        _pos_kernel,
        grid=(t // tm,),
        in_specs=[row, row, pl.BlockSpec((8, LANES), lambda i: (0, 0))],
        out_specs=[row, pl.BlockSpec((n_tiles, LANES), lambda i: (0, 0)),
                   pl.BlockSpec((16, LANES), lambda i: (0, 0))],
        out_shape=[jax.ShapeDtypeStruct((t, LANES), I32),
                   jax.ShapeDtypeStruct((n_tiles, LANES), I32),
                   jax.ShapeDtypeStruct((16, LANES), I32)],
        compiler_params=_params(1),
        name="pos",
    )(eid, rank, counts)


def _row_copy(src, src_row, dst, dst_row, sem):
    return pltpu.make_async_copy(src.at[pl.ds(src_row, 1)], dst.at[pl.ds(dst_row, 1)], sem)


def _dispatch_kernel(seg_end_ref, seg_len_ref, pos_ref, h2_ref, xs_ref, zeros, sem):
    i = pl.program_id(0)
    tm = MOVE_ROWS

    @pl.when(i == 0)
    def _():
        zeros[...] = jnp.zeros_like(zeros)
        for e in range(N_EXPERTS):
            @pl.when(seg_len_ref[e] > 0)
            def _():
                start = pl.multiple_of(seg_end_ref[e] - EXPERT_ROWS, EXPERT_ROWS)
                pltpu.make_async_copy(zeros, xs_ref.at[pl.ds(start, EXPERT_ROWS)], sem).start()
        for e in range(N_EXPERTS):
            @pl.when(seg_len_ref[e] > 0)
            def _():
                pltpu.make_async_copy(zeros, xs_ref.at[pl.ds(0, EXPERT_ROWS)], sem).wait()

        first_unused = seg_end_ref[N_EXPERTS - 1] // EXPERT_ROWS
        n_tiles = xs_ref.shape[0] // EXPERT_ROWS

        def clear(tile, _):
            start = pl.multiple_of(tile * EXPERT_ROWS, EXPERT_ROWS)
            pltpu.make_async_copy(zeros, xs_ref.at[pl.ds(start, EXPERT_ROWS)], sem).start()
            return 0

        def clear_done(tile, _):
            pltpu.make_async_copy(zeros, xs_ref.at[pl.ds(0, EXPERT_ROWS)], sem).wait()
            return 0

        lax.fori_loop(first_unused, n_tiles, clear, 0)
        lax.fori_loop(first_unused, n_tiles, clear_done, 0)

    def issue(r, _):
        _row_copy(h2_ref, r, xs_ref, pos_ref[2 * r], sem).start()
        _row_copy(h2_ref, r, xs_ref, pos_ref[2 * r + 1], sem).start()
        return 0

    lax.fori_loop(0, tm, issue, 0, unroll=8)

    def drain(r, _):
        _row_copy(h2_ref, 0, xs_ref, 0, sem).wait()
        return 0

    lax.fori_loop(0, 2 * tm, drain, 0, unroll=8)


def _dispatch(seg_end, seg_len, pos_flat, h2, n_rows):
    t, d = h2.shape
    tm = MOVE_ROWS
    grid_spec = pltpu.PrefetchScalarGridSpec(
        num_scalar_prefetch=2,
        grid=(t // tm,),
        in_specs=[pl.BlockSpec((2 * tm,), lambda i, *_: (i,), memory_space=pltpu.SMEM),
                  pl.BlockSpec((tm, d), lambda i, *_: (i, 0))],
        out_specs=pl.BlockSpec(memory_space=pl.ANY),
        scratch_shapes=[pltpu.VMEM((EXPERT_ROWS, d), F32), pltpu.SemaphoreType.DMA(())],
    )
    return pl.pallas_call(
        _dispatch_kernel,
        grid_spec=grid_spec,
        out_shape=jax.ShapeDtypeStruct((n_rows, d), F32),
        compiler_params=pltpu.CompilerParams(dimension_semantics=("arbitrary",),
                                             vmem_limit_bytes=VMEM_LIMIT),
        name="dispatch",
    )(seg_end, seg_len, pos_flat, h2)


def _expert_kernel(tile_expert_ref, tile_live_ref, xs_ref, wg_ref, wu_ref, wd_ref, o_ref, wg, wu, wd):
    i = pl.program_id(0)

    @pl.when(tile_live_ref[i] > 0)
    def _():
        changed = jnp.logical_or(i == 0, tile_expert_ref[i] != tile_expert_ref[jnp.maximum(i - 1, 0)])

        @pl.when(changed)
        def _():
            wg[...] = wg_ref[0].astype(BF16)
            wu[...] = wu_ref[0].astype(BF16)
            wd[...] = wd_ref[0].astype(BF16)

        x = xs_ref[...].astype(BF16)
        gate = _dot(x, wg[...])
        up = _dot(x, wu[...])
        hid = (gate * jax.nn.sigmoid(gate)) * up
        o_ref[...] = _dot(hid.astype(BF16), wd[...])

    @pl.when(tile_live_ref[i] == 0)
    def _():
        o_ref[...] = jnp.zeros_like(o_ref)


def _experts(tile_expert, tile_live, xs, w_gate, w_up, w_down):
    n_rows, d = xs.shape
    de = w_gate.shape[2]
    tm = EXPERT_ROWS
    grid_spec = pltpu.PrefetchScalarGridSpec(
        num_scalar_prefetch=2,
        grid=(n_rows // tm,),
        in_specs=[pl.BlockSpec((tm, d), lambda i, te, tl: (i, 0)),
                  pl.BlockSpec((1, d, de), lambda i, te, tl: (te[i], 0, 0)),
                  pl.BlockSpec((1, d, de), lambda i, te, tl: (te[i], 0, 0)),
                  pl.BlockSpec((1, de, d), lambda i, te, tl: (te[i], 0, 0))],
        out_specs=pl.BlockSpec((tm, d), lambda i, te, tl: (i, 0)),
        scratch_shapes=[pltpu.VMEM((d, de), BF16), pltpu.VMEM((d, de), BF16), pltpu.VMEM((de, d), BF16)],
    )
    return pl.pallas_call(
        _expert_kernel,
        grid_spec=grid_spec,
        out_shape=jax.ShapeDtypeStruct((n_rows, d), F32),
        compiler_params=_params(1),
        name="experts",
    )(tile_expert, tile_live, xs, w_gate, w_up, w_down)


def _combine_kernel(pos_ref, pos_next_ref, ys_ref, x1_ref, ew_ref, mod_ref, g_ref, o_ref, buf, sem, *,
                    d, final_norm):
    i = pl.program_id(0)
    n = pl.num_programs(0)
    tm = MOVE_ROWS

    def fetch(pref, slot):
        def issue(r, _):
            for k in range(2):
                pltpu.make_async_copy(ys_ref.at[pl.ds(pref[2 * r + k], 1)],
                                      buf.at[slot, k, pl.ds(r, 1)], sem.at[slot]).start()
            return 0
        lax.fori_loop(0, tm, issue, 0, unroll=8)

    @pl.when(i == 0)
    def _():
        fetch(pos_ref, 0)

    @pl.when(i + 1 < n)
    def _():
        fetch(pos_next_ref, (i + 1) % 2)

    slot = i % 2

    def drain(r, _):
        pltpu.make_async_copy(ys_ref.at[pl.ds(0, 1)], buf.at[slot, 0, pl.ds(0, 1)], sem.at[slot]).wait()
        return 0

    lax.fori_loop(0, 2 * tm, drain, 0, unroll=8)

    ew = ew_ref[...]
    y = ew[:, 0:1] * buf[slot, 0] + ew[:, 1:2] * buf[slot, 1]
    g2 = mod_ref[0, :, 5 * d:6 * d]
    x2 = x1_ref[...] + g2 * y
    o_ref[...] = _rms(x2, g_ref[...]) if final_norm else x2


def _combine(pos_flat, ys, x1, ew, mod3, final_g, seq, final_norm):
    t, d = x1.shape
    tm = MOVE_ROWS
    n = t // tm
    per_batch = seq // tm
    return pl.pallas_call(
        functools.partial(_combine_kernel, d=d, final_norm=final_norm),
        grid=(n,),
        in_specs=[pl.BlockSpec((2 * tm,), lambda i: (i,), memory_space=pltpu.SMEM),
                  pl.BlockSpec((2 * tm,), lambda i: (jnp.minimum(i + 1, n - 1),), memory_space=pltpu.SMEM),
                  pl.BlockSpec(memory_space=pl.ANY),
                  pl.BlockSpec((tm, d), lambda i: (i, 0)),
                  pl.BlockSpec((tm, LANES), lambda i: (i, 0)),
                  pl.BlockSpec((1, 1, mod3.shape[2]), lambda i: (i // per_batch, 0, 0)),
                  pl.BlockSpec((1, d), lambda i: (0, 0))],
        out_specs=pl.BlockSpec((tm, d), lambda i: (i, 0)),
        out_shape=jax.ShapeDtypeStruct((t, d), F32),
        scratch_shapes=[pltpu.VMEM((2, 2, tm, d), F32), pltpu.SemaphoreType.DMA((2,))],
        compiler_params=_params(1),
        name="combine",
    )(pos_flat, pos_flat, ys, x1, ew, mod3, final_g)


def _rope_tables(seq):
    inv = 1.0 / (ROPE_THETA ** (jnp.arange(0, HEAD_DIM, 2, dtype=F32) / HEAD_DIM))
    ang = jnp.arange(seq, dtype=F32)[:, None] * inv[None, :]
    cos, sin = jnp.cos(ang), jnp.sin(ang)
    cos_head = jnp.concatenate([cos, cos], axis=1)
    sin_head = jnp.concatenate([-sin, sin], axis=1)
    reps = SEG // HEAD_DIM
    return jnp.tile(cos_head, (1, reps)), jnp.tile(sin_head, (1, reps))


def kernel(x, c, w_ada, b_ada, norm1_g, w_in, lambda_q1, lambda_k1, lambda_q2, lambda_k2,
           diff_subln_g, w_proj_moba, w_proj_diff, w_out, norm2_g, w_group, b_group,
           w_expert, b_expert, w_gate, w_up, w_down, final_g):
    batch, seq, d = x.shape
    depth = w_ada.shape[0]
    t = batch * seq
    assert seq % PROJ_ROWS == 0 and seq % MOBA_BLOCK == 0 and seq // MOBA_BLOCK <= LANES
    assert ATTN_ROWS == MOBA_BLOCK and d % SEG == 0 and t % SORT_ROWS == 0 and batch <= 8
    assert seq % ATTN_Q_ROWS == 0 and ATTN_Q_ROWS % ATTN_ROWS == 0
    assert EXPERT_ROWS & (EXPERT_ROWS - 1) == 0
    n_rows = 2 * t + N_EXPERTS * EXPERT_ROWS
    n_tiles = n_rows // EXPERT_ROWS
    cos, sin = _rope_tables(seq)
    c_pad = jnp.zeros((8, d), F32).at[:batch].set(c)
    xf = x.reshape(t, d)
    row = lambda v: v.reshape(1, -1)
    for l in range(depth):
        mod = _ada(c_pad, w_ada[l], row(b_ada[l]))
        mod3 = mod[:batch].reshape(batch, 1, 6 * d)
        w_qkv = w_in[l][:, :N_QKV_SEGS * SEG].astype(BF16)
        w_gates = w_in[l][:, N_QKV_SEGS * SEG:].astype(BF16)
        qm, km, vm, qd, kd, vd, kmean = _proj(xf, mod3, row(norm1_g[l]), w_qkv, cos, sin, seq)
        kmean = kmean.reshape(batch, seq // MOBA_BLOCK, SEG)
        om = _moba(qm, km, vm, kmean, batch, seq)
        lam_init = 0.8 - 0.6 * math.exp(-0.3 * l)
        od = _diff(qd, kd, vd, row(lambda_q1[l]), row(lambda_k1[l]), row(lambda_q2[l]),
                   row(lambda_k2[l]), row(diff_subln_g[l]), batch, seq, lam_init)
        w_router = jnp.zeros((d, LANES), F32)
        w_router = w_router.at[:, :N_GROUPS].set(w_group[l])
        w_router = w_router.at[:, N_GROUPS:N_GROUPS + N_EXPERTS].set(w_expert[l])
        b_router = jnp.zeros((1, LANES), F32)
        b_router = b_router.at[0, :N_GROUPS].set(b_group[l])
        b_router = b_router.at[0, N_GROUPS:N_GROUPS + N_EXPERTS].set(b_expert[l])
        x1, h2, eid, ew = _merge(xf, om, od, mod3, row(norm1_g[l]), row(norm2_g[l]), w_gates,
                                 w_proj_moba[l].astype(BF16), w_proj_diff[l].astype(BF16),
                                 w_out[l].astype(BF16), w_router, b_router, seq)
        rank, counts = _rank(eid)
        pos, tiles, segs = _pos(eid, rank, counts, n_tiles)
        pos_flat = pos[:, :2].reshape(2 * t)
        xs = _dispatch(segs[0, :N_EXPERTS], segs[8, :N_EXPERTS], pos_flat, h2, n_rows)
        ys = _experts(tiles[:, 0], tiles[:, 1], xs, w_gate[l], w_up[l], w_down[l])
        xf = _combine(pos_flat, ys, x1, ew, mod3, row(final_g), seq, final_norm=(l == depth - 1))
    return xf.reshape(batch, seq, d)
```

```python
import functools
import math

import jax
import jax.numpy as jnp
from jax import lax
from jax.experimental import pallas as pl
from jax.experimental.pallas import tpu as pltpu

F32 = jnp.float32
BF16 = jnp.bfloat16
I32 = jnp.int32

LANES = 128
HEAD_DIM = 64
HEADS_PER_VREG = LANES // HEAD_DIM
MOBA_HEADS = 8
MOBA_BLOCK = 256
MOBA_TOPK = 3
DIFF_HEADS = 4
ROPE_THETA = 10000.0
N_GROUPS = 4
EXPERTS_PER_GROUP = 8
N_EXPERTS = N_GROUPS * EXPERTS_PER_GROUP
NORM_EPS = 1e-6
NEG_INF = -1e30
SEG = MOBA_HEADS * HEAD_DIM
N_QKV_SEGS = 6
SUM_ROWS = 16

PROJ_ROWS = 512
ATTN_ROWS = 256
ATTN_Q_ROWS = 1024
MERGE_ROWS = 512
SORT_ROWS = 512
EXPERT_ROWS = 256
MOVE_ROWS = 256
VMEM_LIMIT = 56 * 1024 * 1024


def _params(n_axes, vmem=VMEM_LIMIT):
    return pltpu.CompilerParams(dimension_semantics=("arbitrary",) * n_axes,
                                vmem_limit_bytes=vmem)


def _dot(a, b):
    return jnp.dot(a, b, preferred_element_type=F32)


def _dot_nt(a, b):
    return lax.dot_general(a, b, (((1,), (1,)), ((), ())), preferred_element_type=F32)


def _rms(x, g):
    return x * lax.rsqrt(jnp.mean(x * x, axis=-1, keepdims=True) + NORM_EPS) * g


def _ada_kernel(c_ref, w_ref, b_ref, o_ref):
    c = c_ref[...]
    o_ref[...] = _dot(c * jax.nn.sigmoid(c), w_ref[...]) + b_ref[...]


def _ada(c_pad, w, b):
    rows, d = c_pad.shape
    n = w.shape[1]
    tn = 1536
    return pl.pallas_call(
        _ada_kernel,
        grid=(n // tn,),
        in_specs=[pl.BlockSpec((rows, d), lambda j: (0, 0)),
                  pl.BlockSpec((d, tn), lambda j: (0, j)),
                  pl.BlockSpec((1, tn), lambda j: (0, j))],
        out_specs=pl.BlockSpec((rows, tn), lambda j: (0, j)),
        out_shape=jax.ShapeDtypeStruct((rows, n), F32),
        compiler_params=_params(1),
        name="ada",
    )(c_pad, w, b)


def _rope(x, cos, sin_signed):
    half = HEAD_DIM // 2
    width = x.shape[1]
    lane = lax.broadcasted_iota(I32, x.shape, 1)
    first = (lane & (HEAD_DIM - 1)) < half
    partner = jnp.where(first, pltpu.roll(x, width - half, 1), pltpu.roll(x, half, 1))
    return x * cos + partner * sin_signed


def _proj_kernel(x_ref, mod_ref, g_ref, w_ref, cos_ref, sin_ref,
                 qm_ref, km_ref, vm_ref, qd_ref, kd_ref, vd_ref, kmean_ref, *, d):
    x = x_ref[...]
    sh = mod_ref[0, :, 0:d]
    sc = mod_ref[0, :, d:2 * d]
    h = (_rms(x, g_ref[...]) * (1.0 + sc) + sh).astype(BF16)
    cos = cos_ref[...]
    sin = sin_ref[...]
    scale = HEAD_DIM ** -0.5 * math.log2(math.e)
    outs = (qm_ref, km_ref, vm_ref, qd_ref, kd_ref, vd_ref)
    n_blk = x.shape[0] // ATTN_ROWS
    for seg, o_ref in enumerate(outs):
        y = _dot(h, w_ref[:, seg * SEG:(seg + 1) * SEG])
        if seg in (0, 1, 3, 4):
            y = _rope(y, cos, sin)
        if seg in (0, 3):
            y = y * scale
        if seg == 1:
            for blk in range(n_blk):
                rows = y[blk * MOBA_BLOCK:(blk + 1) * MOBA_BLOCK]
                kmean_ref[0, blk:blk + 1, :] = jnp.mean(rows, axis=0, keepdims=True)
        if seg in (2, 5):
            for blk in range(n_blk):
                for part in range(SEG // LANES):
                    piece = y[blk * ATTN_ROWS:(blk + 1) * ATTN_ROWS, part * LANES:(part + 1) * LANES]
                    o_ref[blk, part * LANES:(part + 1) * LANES, :] = piece.T.astype(BF16)
        else:
            o_ref[...] = y.astype(BF16)


def _proj(x2, mod3, g, w_qkv, cos, sin, seq):
    t, d = x2.shape
    tm = PROJ_ROWS
    per_batch = seq // tm
    row_spec = pl.BlockSpec((tm, SEG), lambda i: (i, 0))
    tab_spec = pl.BlockSpec((tm, SEG), lambda i: (i % per_batch, 0))
    act = jax.ShapeDtypeStruct((t, SEG), BF16)
    act_t = jax.ShapeDtypeStruct((t // ATTN_ROWS, SEG, ATTN_ROWS), BF16)
    t_spec = pl.BlockSpec((tm // ATTN_ROWS, SEG, ATTN_ROWS), lambda i: (i, 0, 0))
    return pl.pallas_call(
        functools.partial(_proj_kernel, d=d),
        grid=(t // tm,),
        in_specs=[pl.BlockSpec((tm, d), lambda i: (i, 0)),
                  pl.BlockSpec((1, 1, mod3.shape[2]), lambda i: (i // per_batch, 0, 0)),
                  pl.BlockSpec((1, d), lambda i: (0, 0)),
                  pl.BlockSpec(w_qkv.shape, lambda i: (0, 0)),
                  tab_spec, tab_spec],
        out_specs=[row_spec, row_spec, t_spec, row_spec, row_spec, t_spec,
                   pl.BlockSpec((1, tm // MOBA_BLOCK, SEG), lambda i: (i, 0, 0))],
        out_shape=[act, act, act_t, act, act, act_t,
                   jax.ShapeDtypeStruct((t // tm, tm // MOBA_BLOCK, SEG), F32)],
        compiler_params=_params(1),
        name="proj",
    )(x2, mod3, g, w_qkv, cos, sin)


def _attend(qs, keys_of_tile, k_ref, vt_ref, v_rows, q_tile, own_only_rows, scratch):
    tq, tk = ATTN_Q_ROWS, ATTN_ROWS
    sub = tq // tk
    ones = jnp.ones((SUM_ROWS, tk), BF16)
    n_soft = len(qs)
    acc_refs, max_refs = scratch[:n_soft], scratch[n_soft:]
    for acc_ref in acc_refs:
        acc_ref[...] = jnp.zeros(acc_ref.shape, F32)

    def scores(j, first_query=0):
        start = pl.multiple_of(j * tk, tk)
        keys = keys_of_tile(j, k_ref[pl.ds(start, tk), :])
        return tuple(_dot(keys, q[:, first_query:]) for q in qs), vt_ref[j]

    def update(tile, mask, maxes, first_query=0):
        s_all, vt = tile
        new = ()
        for n, s in enumerate(s_all):
            if mask is not None:
                s = jnp.where(mask, s, NEG_INF)
            m = max_refs[n][:, first_query:] if maxes is None else maxes[n]
            m_new = jnp.maximum(m, jnp.max(s, axis=0, keepdims=True))
            alpha = jnp.exp2(m - m_new)
            p = jnp.exp2(s - m_new).astype(BF16)
            if maxes is None:
                max_refs[n][:, first_query:] = m_new
            new += (m_new,)
            vt_sum = jnp.concatenate([vt[v_rows[n]], ones], axis=0)
            acc_refs[n][:, first_query:] = alpha * acc_refs[n][:, first_query:] + _dot(vt_sum, p)
        return new

    first_own = q_tile * sub

    def body(group, maxes):
        for tile in [scores(sub * group + b) for b in range(sub)]:
            maxes = update(tile, None, maxes)
        return maxes

    maxes = lax.fori_loop(0, q_tile, body, (jnp.full((1, tq), NEG_INF, F32),) * n_soft)
    for max_ref, m in zip(max_refs, maxes):
        max_ref[...] = m
    own = [scores(first_own + b, b * tk) for b in range(sub)]
    for b in range(sub):
        key = lax.broadcasted_iota(I32, (tk, tq - b * tk), 0)
        qry = lax.broadcasted_iota(I32, (tk, tq - b * tk), 1)
        keep = key <= qry
        if own_only_rows:
            keep = keep | (qry >= tk)
        update(own[b], keep, None, b * tk)
    out = []
    for n, acc_ref in enumerate(acc_refs):
        rows = v_rows[n].stop - v_rows[n].start
        out.append((acc_ref[rows:rows + 1, :], acc_ref[0:rows, :]))
    return out


def _softmax_scratch(n_softmax, features):
    return ([pltpu.VMEM((features + SUM_ROWS, ATTN_Q_ROWS), F32)] * n_softmax
            + [pltpu.VMEM((1, ATTN_Q_ROWS), F32)] * n_softmax)


def _moba_kernel(q_ref, k_ref, v_ref, kmean_ref, o_ref, *scratch, n_blocks):
    tq = ATTN_Q_ROWS
    q_tile = pl.program_id(2)
    q_t = q_ref[...].astype(F32).T
    feat = lax.broadcasted_iota(I32, (LANES, tq), 0)
    blk = lax.broadcasted_iota(I32, (n_blocks, tq), 0)
    qry = lax.broadcasted_iota(I32, (n_blocks, tq), 1)
    own_block = q_tile * (tq // MOBA_BLOCK) + (qry >> int(math.log2(MOBA_BLOCK)))
    kmean = kmean_ref[0]
    km_head = lax.broadcasted_iota(I32, kmean.shape, 1) >> 6
    pad = jnp.zeros((LANES - n_blocks, tq), BF16)
    lane_k = lax.broadcasted_iota(I32, (ATTN_ROWS, LANES), 1)
    qs = []
    for head in range(HEADS_PER_VREG):
        qh = jnp.where((feat >> 6) == head, q_t, 0.0).astype(BF16)
        km = jnp.where(km_head == head, kmean, 0.0)
        km_hi = km.astype(BF16)
        km_lo = (km - km_hi.astype(F32)).astype(BF16)
        gate = _dot(km_hi, qh) + _dot(km_lo, qh)
        gate = jnp.where(blk < own_block, gate, NEG_INF)
        bias = jnp.where(blk == own_block, 0.0, NEG_INF)
        for _ in range(min(MOBA_TOPK, n_blocks)):
            top = jnp.max(gate, axis=0, keepdims=True)
            first = jnp.min(jnp.where(gate == top, blk, n_blocks), axis=0, keepdims=True)
            picked = blk == first
            bias = jnp.where(picked & (blk < own_block), 0.0, bias)
            gate = jnp.where(picked, -jnp.inf, gate)
        qs.append(jnp.concatenate([qh, bias.astype(BF16), pad], axis=0))

    def rhs_of_tile(j, keys):
        return jnp.concatenate([keys, jnp.where(lane_k == j, 1.0, 0.0).astype(BF16)], axis=1)

    v_rows = [slice(h * HEAD_DIM, (h + 1) * HEAD_DIM) for h in range(HEADS_PER_VREG)]
    stats = _attend(qs, rhs_of_tile, k_ref, v_ref, v_rows, q_tile, True, scratch)
    out_t = jnp.concatenate([acc / l for l, acc in stats], axis=0)
    o_ref[...] = out_t.T.astype(BF16)


def _moba(qm, km, vm_t, kmean, batch, seq):
    t = qm.shape[0]
    tq = ATTN_Q_ROWS
    nq = seq // tq
    n_blocks = seq // MOBA_BLOCK
    pairs = MOBA_HEADS // HEADS_PER_VREG
    n_kv = seq // ATTN_ROWS
    return pl.pallas_call(
        functools.partial(_moba_kernel, n_blocks=n_blocks),
        grid=(batch, pairs, nq),
        in_specs=[pl.BlockSpec((tq, LANES), lambda b, p, i: (b * nq + i, p)),
                  pl.BlockSpec((seq, LANES), lambda b, p, i: (b, p)),
                  pl.BlockSpec((n_kv, LANES, ATTN_ROWS), lambda b, p, i: (b, p, 0)),
                  pl.BlockSpec((1, n_blocks, LANES), lambda b, p, i: (b, 0, p))],
        out_specs=pl.BlockSpec((tq, LANES), lambda b, p, i: (b * nq + i, p)),
        out_shape=jax.ShapeDtypeStruct((t, SEG), BF16),
        scratch_shapes=_softmax_scratch(HEADS_PER_VREG, HEAD_DIM),
        compiler_params=_params(3),
        name="moba",
    )(qm, km, vm_t, kmean)


def _diff_kernel(q_ref, k_ref, v_ref, lq1_ref, lk1_ref, lq2_ref, lk2_ref, g_ref, o_ref,
                 *scratch, lam_init):
    tq = ATTN_Q_ROWS
    q_t = q_ref[...].astype(F32).T
    comp = lax.broadcasted_iota(I32, (LANES, tq), 0) >> 6
    qs = [jnp.where(comp == c, q_t, 0.0).astype(BF16) for c in range(2)]
    v_rows = [slice(0, LANES)] * 2
    (l1, a1), (l2, a2) = _attend(qs, lambda j, keys: keys, k_ref, v_ref, v_rows,
                                 pl.program_id(2), False, scratch)
    lam = (jnp.exp(jnp.sum(lq1_ref[...] * lk1_ref[...], axis=1, keepdims=True))
           - jnp.exp(jnp.sum(lq2_ref[...] * lk2_ref[...], axis=1, keepdims=True)) + lam_init)
    o = (a1 / l1 - lam * (a2 / l2)).T
    o_ref[...] = (_rms(o, g_ref[...]) * (1.0 - lam_init)).astype(BF16)


def _diff(qd, kd, vd_t, lq1, lk1, lq2, lk2, g, batch, seq, lam_init):
    t = qd.shape[0]
    tq = ATTN_Q_ROWS
    nq = seq // tq
    n_kv = seq // ATTN_ROWS
    vec = lambda a: pl.BlockSpec(a.shape, lambda b, h, i: (0, 0))
    return pl.pallas_call(
        functools.partial(_diff_kernel, lam_init=lam_init),
        grid=(batch, DIFF_HEADS, nq),
        in_specs=[pl.BlockSpec((tq, LANES), lambda b, h, i: (b * nq + i, h)),
                  pl.BlockSpec((seq, LANES), lambda b, h, i: (b, h)),
                  pl.BlockSpec((n_kv, LANES, ATTN_ROWS), lambda b, h, i: (b, h, 0)),
                  vec(lq1), vec(lk1), vec(lq2), vec(lk2), vec(g)],
        out_specs=pl.BlockSpec((tq, LANES), lambda b, h, i: (b * nq + i, h)),
        out_shape=jax.ShapeDtypeStruct((t, SEG), BF16),
        scratch_shapes=_softmax_scratch(2, LANES),
        compiler_params=_params(3),
        name="diff",
    )(qd, kd, vd_t, lq1, lk1, lq2, lk2, g)


def _merge_kernel(x_ref, om_ref, od_ref, mod_ref, g1n_ref, g2n_ref, wg_ref, wpm_ref, wpd_ref,
                  wout_ref, wr_ref, br_ref, x1_ref, h2_ref, eid_ref, ew_ref, merged, *, d):
    x = x_ref[...]
    sh1 = mod_ref[0, :, 0:d]
    sc1 = mod_ref[0, :, d:2 * d]
    g1 = mod_ref[0, :, 2 * d:3 * d]
    sh2 = mod_ref[0, :, 3 * d:4 * d]
    sc2 = mod_ref[0, :, 4 * d:5 * d]
    h = (_rms(x, g1n_ref[...]) * (1.0 + sc1) + sh1).astype(BF16)
    om = om_ref[...]
    od = od_ref[...]
    chunk = SEG
    for c in range(d // chunk):
        cols = slice(c * chunk, (c + 1) * chunk)
        gm = _dot(h, wg_ref[:, c * chunk:(c + 1) * chunk])
        gd = _dot(h, wg_ref[:, d + c * chunk:d + (c + 1) * chunk])
        ym = _dot(om, wpm_ref[:, cols])
        yd = _dot(od, wpd_ref[:, cols])
        merged[:, cols] = (jax.nn.sigmoid(gm) * ym + jax.nn.sigmoid(gd) * yd).astype(BF16)
    x1 = x + g1 * _dot(merged[...], wout_ref[...])
    x1_ref[...] = x1
    h2 = _rms(x1, g2n_ref[...]) * (1.0 + sc2) + sh2
    h2_ref[...] = h2

    h2_hi = h2.astype(BF16)
    h2_lo = (h2 - h2_hi.astype(F32)).astype(BF16)
    wr = wr_ref[...]
    wr_hi = wr.astype(BF16)
    wr_lo = (wr - wr_hi.astype(F32)).astype(BF16)
    logits = _dot(h2_hi, wr_hi) + _dot(h2_lo, wr_hi) + _dot(h2_hi, wr_lo) + br_ref[...]
    lane = lax.broadcasted_iota(I32, logits.shape, 1)
    is_group = lane < N_GROUPS
    gl = jnp.where(is_group, logits, -jnp.inf)
    gmax = jnp.max(gl, axis=1, keepdims=True)
    gexp = jnp.exp(gl - gmax)
    g_w = 1.0 / jnp.sum(gexp, axis=1, keepdims=True)
    g_idx = jnp.min(jnp.where(gl == gmax, lane, LANES), axis=1, keepdims=True)
    e_lane = lane - N_GROUPS
    in_group = (e_lane >= g_idx * EXPERTS_PER_GROUP) & (e_lane < (g_idx + 1) * EXPERTS_PER_GROUP)
    el = jnp.where(in_group, logits, -jnp.inf)
    emax = jnp.max(el, axis=1, keepdims=True)
    eexp = jnp.exp(el - emax)
    prob = eexp / jnp.sum(eexp, axis=1, keepdims=True)
    prob = jnp.where(in_group, prob, -1.0)
    p1 = jnp.max(prob, axis=1, keepdims=True)
    i1 = jnp.min(jnp.where(prob == p1, lane, LANES), axis=1, keepdims=True)
    prob2 = jnp.where(lane == i1, -1.0, prob)
    p2 = jnp.max(prob2, axis=1, keepdims=True)
    i2 = jnp.min(jnp.where(prob2 == p2, lane, LANES), axis=1, keepdims=True)
    tot = p1 + p2
    eid_ref[...] = jnp.where(lane == 0, i1 - N_GROUPS, jnp.where(lane == 1, i2 - N_GROUPS, 0))
    ew_ref[...] = jnp.where(lane == 0, g_w * (p1 / tot), jnp.where(lane == 1, g_w * (p2 / tot), 0.0))


def _merge(x2, om, od, mod3, g1n, g2n, wg, wpm, wpd, wout, wr, br, seq):
    t, d = x2.shape
    tm = MERGE_ROWS
    per_batch = seq // tm
    full = lambda a: pl.BlockSpec(a.shape, lambda i: (0,) * a.ndim)
    row = lambda w: pl.BlockSpec((tm, w), lambda i: (i, 0))
    return pl.pallas_call(
        functools.partial(_merge_kernel, d=d),
        grid=(t // tm,),
        in_specs=[row(d), row(SEG), row(SEG),
                  pl.BlockSpec((1, 1, mod3.shape[2]), lambda i: (i // per_batch, 0, 0)),
                  full(g1n), full(g2n), full(wg), full(wpm), full(wpd), full(wout), full(wr), full(br)],
        out_specs=[row(d), row(d), row(LANES), row(LANES)],
        out_shape=[jax.ShapeDtypeStruct((t, d), F32), jax.ShapeDtypeStruct((t, d), F32),
                   jax.ShapeDtypeStruct((t, LANES), I32), jax.ShapeDtypeStruct((t, LANES), F32)],
        scratch_shapes=[pltpu.VMEM((tm, d), BF16)],
        compiler_params=_params(1),
        name="merge",
    )(x2, om, od, mod3, g1n, g2n, wg, wpm, wpd, wout, wr, br)


def _onehots(eid):
    lane = lax.broadcasted_iota(I32, eid.shape, 1)
    return lane, lane == eid[:, 0:1], lane == eid[:, 1:2]


def _rank_kernel(eid_ref, rank_ref, counts_ref, carry):
    tm = eid_ref.shape[0]

    @pl.when(pl.program_id(0) == 0)
    def _():
        carry[...] = jnp.zeros_like(carry)

    lane, oh0, oh1 = _onehots(eid_ref[...])
    used = jnp.where(oh0 | oh1, 1.0, 0.0)
    r = lax.broadcasted_iota(I32, (tm, tm), 0)
    c = lax.broadcasted_iota(I32, (tm, tm), 1)
    before = jnp.where(c < r, 1.0, 0.0).astype(BF16)
    base = carry[0:1, :] + _dot(before, used.astype(BF16))
    r0 = jnp.sum(jnp.where(oh0, base, 0.0), axis=1, keepdims=True)
    r1 = jnp.sum(jnp.where(oh1, base, 0.0), axis=1, keepdims=True)
    rank_ref[...] = jnp.where(lane == 0, r0, jnp.where(lane == 1, r1, 0.0)).astype(I32)
    carry[...] = carry[...] + jnp.sum(used, axis=0, keepdims=True)
    counts_ref[...] = carry[...].astype(I32)


def _rank(eid):
    t = eid.shape[0]
    tm = SORT_ROWS
    return pl.pallas_call(
        _rank_kernel,
        grid=(t // tm,),
        in_specs=[pl.BlockSpec((tm, LANES), lambda i: (i, 0))],
        out_specs=[pl.BlockSpec((tm, LANES), lambda i: (i, 0)),
                   pl.BlockSpec((8, LANES), lambda i: (0, 0))],
        out_shape=[jax.ShapeDtypeStruct((t, LANES), I32), jax.ShapeDtypeStruct((8, LANES), I32)],
        scratch_shapes=[pltpu.VMEM((8, LANES), F32)],
        compiler_params=_params(1),
        name="rank",
    )(eid)


def _segment_ends(counts):
    lane = lax.broadcasted_iota(I32, counts.shape, 1)
    padded = (counts + (EXPERT_ROWS - 1)) & (-EXPERT_ROWS)
    padded = jnp.where(lane < N_EXPERTS, padded, 0)
    ends = padded
    shift = 1
    while shift < N_EXPERTS:
        ends = ends + jnp.where(lane >= shift, pltpu.roll(ends, shift, 1), 0)
        shift *= 2
    return padded, ends


def _pos_kernel(eid_ref, rank_ref, counts_ref, pos_ref, tile_ref, seg_ref):
    padded, ends = _segment_ends(counts_ref[...])
    starts = (ends - padded)[0:1, :]
    lane, oh0, oh1 = _onehots(eid_ref[...])
    rank = rank_ref[...]
    p0 = rank[:, 0:1] + jnp.sum(jnp.where(oh0, starts, 0), axis=1, keepdims=True)
    p1 = rank[:, 1:2] + jnp.sum(jnp.where(oh1, starts, 0), axis=1, keepdims=True)
    pos_ref[...] = jnp.where(lane == 0, p0, jnp.where(lane == 1, p1, 0))

    @pl.when(pl.program_id(0) == 0)
    def _():
        n_tiles = tile_ref.shape[0]
        first_row = lax.broadcasted_iota(I32, (n_tiles, LANES), 0) * EXPERT_ROWS
        elane = lax.broadcasted_iota(I32, (n_tiles, LANES), 1)
        done = jnp.where((ends[0:1, :] <= first_row) & (elane < N_EXPERTS), 1, 0)
        expert = jnp.minimum(jnp.sum(done, axis=1, keepdims=True), N_EXPERTS - 1)
        total = jnp.max(ends[0:1, :], axis=1, keepdims=True)
        live = jnp.where(first_row < total, 1, 0)
        tile_ref[...] = jnp.where(elane == 0, expert, jnp.where(elane == 1, live, 0))
        seg_ref[...] = jnp.concatenate([ends, padded], axis=0)


def _pos(eid, rank, counts, n_tiles):
    t = eid.shape[0]
    tm = SORT_ROWS
    row = pl.BlockSpec((tm, LANES), lambda i: (i, 0))
    return pl.pallas_call(
        _pos_kernel,
        grid=(t // tm,),
        in_specs=[row, row, pl.BlockSpec((8, LANES), lambda i: (0, 0))],
        out_specs=[row, pl.BlockSpec((n_tiles, LANES), lambda i: (0, 0)),
                   pl.BlockSpec((16, LANES), lambda i: (0, 0))],
        out_shape=[jax.ShapeDtypeStruct((t, LANES), I32),
                   jax.ShapeDtypeStruct((n_tiles, LANES), I32),
                   jax.ShapeDtypeStruct((16, LANES), I32)],
        compiler_params=_params(1),
        name="pos",
    )(eid, rank, counts)


def _row_copy(src, src_row, dst, dst_row, sem):
    return pltpu.make_async_copy(src.at[pl.ds(src_row, 1)], dst.at[pl.ds(dst_row, 1)], sem)


def _dispatch_kernel(seg_end_ref, seg_len_ref, pos_ref, h2_ref, xs_ref, zeros, sem):
    i = pl.program_id(0)
    tm = MOVE_ROWS

    @pl.when(i == 0)
    def _():
        zeros[...] = jnp.zeros_like(zeros)
        for e in range(N_EXPERTS):
            @pl.when(seg_len_ref[e] > 0)
            def _():
                start = pl.multiple_of(seg_end_ref[e] - EXPERT_ROWS, EXPERT_ROWS)
                pltpu.make_async_copy(zeros, xs_ref.at[pl.ds(start, EXPERT_ROWS)], sem).start()
        for e in range(N_EXPERTS):
            @pl.when(seg_len_ref[e] > 0)
            def _():
                pltpu.make_async_copy(zeros, xs_ref.at[pl.ds(0, EXPERT_ROWS)], sem).wait()

        first_unused = seg_end_ref[N_EXPERTS - 1] // EXPERT_ROWS
        n_tiles = xs_ref.shape[0] // EXPERT_ROWS

        def clear(tile, _):
            start = pl.multiple_of(tile * EXPERT_ROWS, EXPERT_ROWS)
            pltpu.make_async_copy(zeros, xs_ref.at[pl.ds(start, EXPERT_ROWS)], sem).start()
            return 0

        def clear_done(tile, _):
            pltpu.make_async_copy(zeros, xs_ref.at[pl.ds(0, EXPERT_ROWS)], sem).wait()
            return 0

        lax.fori_loop(first_unused, n_tiles, clear, 0)
        lax.fori_loop(first_unused, n_tiles, clear_done, 0)

    def issue(r, _):
        _row_copy(h2_ref, r, xs_ref, pos_ref[2 * r], sem).start(priority=0)
        _row_copy(h2_ref, r, xs_ref, pos_ref[2 * r + 1], sem).start(priority=1)
        return 0

    lax.fori_loop(0, tm, issue, 0, unroll=8)

    def drain(r, _):
        _row_copy(h2_ref, 0, xs_ref, 0, sem).wait()
        return 0

    lax.fori_loop(0, 2 * tm, drain, 0, unroll=8)


def _dispatch(seg_end, seg_len, pos_flat, h2, n_rows):
    t, d = h2.shape
    tm = MOVE_ROWS
    grid_spec = pltpu.PrefetchScalarGridSpec(
        num_scalar_prefetch=2,
        grid=(t // tm,),
        in_specs=[pl.BlockSpec((2 * tm,), lambda i, *_: (i,), memory_space=pltpu.SMEM),
                  pl.BlockSpec((tm, d), lambda i, *_: (i, 0))],
        out_specs=pl.BlockSpec(memory_space=pl.ANY),
        scratch_shapes=[pltpu.VMEM((EXPERT_ROWS, d), F32), pltpu.SemaphoreType.DMA(())],
    )
    return pl.pallas_call(
        _dispatch_kernel,
        grid_spec=grid_spec,
        out_shape=jax.ShapeDtypeStruct((n_rows, d), F32),
        compiler_params=pltpu.CompilerParams(dimension_semantics=("arbitrary",),
                                             vmem_limit_bytes=VMEM_LIMIT),
        name="dispatch",
    )(seg_end, seg_len, pos_flat, h2)


def _expert_kernel(tile_expert_ref, tile_live_ref, xs_ref, wg_ref, wu_ref, wd_ref, o_ref, wg, wu, wd):
    i = pl.program_id(0)

    @pl.when(tile_live_ref[i] > 0)
    def _():
        changed = jnp.logical_or(i == 0, tile_expert_ref[i] != tile_expert_ref[jnp.maximum(i - 1, 0)])

        @pl.when(changed)
        def _():
            wg[...] = wg_ref[0].astype(BF16)
            wu[...] = wu_ref[0].astype(BF16)
            wd[...] = wd_ref[0].astype(BF16)

        x = xs_ref[...].astype(BF16)
        gate = _dot(x, wg[...])
        up = _dot(x, wu[...])
        hid = (gate * jax.nn.sigmoid(gate)) * up
        o_ref[...] = _dot(hid.astype(BF16), wd[...])

    @pl.when(tile_live_ref[i] == 0)
    def _():
        o_ref[...] = jnp.zeros_like(o_ref)


def _experts(tile_expert, tile_live, xs, w_gate, w_up, w_down):
    n_rows, d = xs.shape
    de = w_gate.shape[2]
    tm = EXPERT_ROWS
    grid_spec = pltpu.PrefetchScalarGridSpec(
        num_scalar_prefetch=2,
        grid=(n_rows // tm,),
        in_specs=[pl.BlockSpec((tm, d), lambda i, te, tl: (i, 0)),
                  pl.BlockSpec((1, d, de), lambda i, te, tl: (te[i], 0, 0)),
                  pl.BlockSpec((1, d, de), lambda i, te, tl: (te[i], 0, 0)),
                  pl.BlockSpec((1, de, d), lambda i, te, tl: (te[i], 0, 0))],
        out_specs=pl.BlockSpec((tm, d), lambda i, te, tl: (i, 0)),
        scratch_shapes=[pltpu.VMEM((d, de), BF16), pltpu.VMEM((d, de), BF16), pltpu.VMEM((de, d), BF16)],
    )
    return pl.pallas_call(
        _expert_kernel,
        grid_spec=grid_spec,
        out_shape=jax.ShapeDtypeStruct((n_rows, d), F32),
        compiler_params=_params(1),
        name="experts",
    )(tile_expert, tile_live, xs, w_gate, w_up, w_down)


def _combine_kernel(pos_ref, pos_next_ref, ys_ref, x1_ref, ew_ref, mod_ref, g_ref, o_ref, buf, sem, *,
                    d, final_norm):
    i = pl.program_id(0)
    n = pl.num_programs(0)
    tm = MOVE_ROWS

    def fetch(pref, slot):
        def issue(r, _):
            for k in range(2):
                pltpu.make_async_copy(ys_ref.at[pl.ds(pref[2 * r + k], 1)],
                                      buf.at[slot, k, pl.ds(r, 1)], sem.at[slot]).start(priority=k)
            return 0
        lax.fori_loop(0, tm, issue, 0, unroll=8)

    @pl.when(i == 0)
    def _():
        fetch(pos_ref, 0)

    @pl.when(i + 1 < n)
    def _():
        fetch(pos_next_ref, (i + 1) % 2)

    slot = i % 2

    def drain(r, _):
        pltpu.make_async_copy(ys_ref.at[pl.ds(0, 1)], buf.at[slot, 0, pl.ds(0, 1)], sem.at[slot]).wait()
        return 0

    lax.fori_loop(0, 2 * tm, drain, 0, unroll=8)

    ew = ew_ref[...]
    y = ew[:, 0:1] * buf[slot, 0] + ew[:, 1:2] * buf[slot, 1]
    g2 = mod_ref[0, :, 5 * d:6 * d]
    x2 = x1_ref[...] + g2 * y
    o_ref[...] = _rms(x2, g_ref[...]) if final_norm else x2


def _combine(pos_flat, ys, x1, ew, mod3, final_g, seq, final_norm):
    t, d = x1.shape
    tm = MOVE_ROWS
    n = t // tm
    per_batch = seq // tm
    return pl.pallas_call(
        functools.partial(_combine_kernel, d=d, final_norm=final_norm),
        grid=(n,),
        in_specs=[pl.BlockSpec((2 * tm,), lambda i: (i,), memory_space=pltpu.SMEM),
                  pl.BlockSpec((2 * tm,), lambda i: (jnp.minimum(i + 1, n - 1),), memory_space=pltpu.SMEM),
                  pl.BlockSpec(memory_space=pl.ANY),
                  pl.BlockSpec((tm, d), lambda i: (i, 0)),
                  pl.BlockSpec((tm, LANES), lambda i: (i, 0)),
                  pl.BlockSpec((1, 1, mod3.shape[2]), lambda i: (i // per_batch, 0, 0)),
                  pl.BlockSpec((1, d), lambda i: (0, 0))],
        out_specs=pl.BlockSpec((tm, d), lambda i: (i, 0)),
        out_shape=jax.ShapeDtypeStruct((t, d), F32),
        scratch_shapes=[pltpu.VMEM((2, 2, tm, d), F32), pltpu.SemaphoreType.DMA((2,))],
        compiler_params=_params(1),
        name="combine",
    )(pos_flat, pos_flat, ys, x1, ew, mod3, final_g)


def _rope_tables(seq):
    inv = 1.0 / (ROPE_THETA ** (jnp.arange(0, HEAD_DIM, 2, dtype=F32) / HEAD_DIM))
    ang = jnp.arange(seq, dtype=F32)[:, None] * inv[None, :]
    cos, sin = jnp.cos(ang), jnp.sin(ang)
    cos_head = jnp.concatenate([cos, cos], axis=1)
    sin_head = jnp.concatenate([-sin, sin], axis=1)
    reps = SEG // HEAD_DIM
    return jnp.tile(cos_head, (1, reps)), jnp.tile(sin_head, (1, reps))


def kernel(x, c, w_ada, b_ada, norm1_g, w_in, lambda_q1, lambda_k1, lambda_q2, lambda_k2,
           diff_subln_g, w_proj_moba, w_proj_diff, w_out, norm2_g, w_group, b_group,
           w_expert, b_expert, w_gate, w_up, w_down, final_g):
    batch, seq, d = x.shape
    depth = w_ada.shape[0]
    t = batch * seq
    assert seq % PROJ_ROWS == 0 and seq % MOBA_BLOCK == 0 and seq // MOBA_BLOCK <= LANES
    assert ATTN_ROWS == MOBA_BLOCK and d % SEG == 0 and t % SORT_ROWS == 0 and batch <= 8
    assert seq % ATTN_Q_ROWS == 0 and ATTN_Q_ROWS % ATTN_ROWS == 0
    assert EXPERT_ROWS & (EXPERT_ROWS - 1) == 0
    n_rows = 2 * t + N_EXPERTS * EXPERT_ROWS
    n_tiles = n_rows // EXPERT_ROWS
    cos, sin = _rope_tables(seq)
    c_pad = jnp.zeros((8, d), F32).at[:batch].set(c)
    xf = x.reshape(t, d)
    row = lambda v: v.reshape(1, -1)
    for l in range(depth):
        mod = _ada(c_pad, w_ada[l], row(b_ada[l]))
        mod3 = mod[:batch].reshape(batch, 1, 6 * d)
        w_qkv = w_in[l][:, :N_QKV_SEGS * SEG].astype(BF16)
        w_gates = w_in[l][:, N_QKV_SEGS * SEG:].astype(BF16)
        qm, km, vm, qd, kd, vd, kmean = _proj(xf, mod3, row(norm1_g[l]), w_qkv, cos, sin, seq)
        kmean = kmean.reshape(batch, seq // MOBA_BLOCK, SEG)
        om = _moba(qm, km, vm, kmean, batch, seq)
        lam_init = 0.8 - 0.6 * math.exp(-0.3 * l)
        od = _diff(qd, kd, vd, row(lambda_q1[l]), row(lambda_k1[l]), row(lambda_q2[l]),
                   row(lambda_k2[l]), row(diff_subln_g[l]), batch, seq, lam_init)
        w_router = jnp.zeros((d, LANES), F32)
        w_router = w_router.at[:, :N_GROUPS].set(w_group[l])
        w_router = w_router.at[:, N_GROUPS:N_GROUPS + N_EXPERTS].set(w_expert[l])
        b_router = jnp.zeros((1, LANES), F32)
        b_router = b_router.at[0, :N_GROUPS].set(b_group[l])
        b_router = b_router.at[0, N_GROUPS:N_GROUPS + N_EXPERTS].set(b_expert[l])
        x1, h2, eid, ew = _merge(xf, om, od, mod3, row(norm1_g[l]), row(norm2_g[l]), w_gates,
                                 w_proj_moba[l].astype(BF16), w_proj_diff[l].astype(BF16),
                                 w_out[l].astype(BF16), w_router, b_router, seq)
        rank, counts = _rank(eid)
        pos, tiles, segs = _pos(eid, rank, counts, n_tiles)
        pos_flat = pos[:, :2].reshape(2 * t)
        xs = _dispatch(segs[0, :N_EXPERTS], segs[8, :N_EXPERTS], pos_flat, h2, n_rows)
        ys = _experts(tiles[:, 0], tiles[:, 1], xs, w_gate[l], w_up[l], w_down[l])
        xf = _combine(pos_flat, ys, x1, ew, mod3, row(final_g), seq, final_norm=(l == depth - 1))
    return xf.reshape(batch, seq, d)
```

```python
import functools
import math

import jax
import jax.numpy as jnp
from jax import lax
from jax.experimental import pallas as pl
from jax.experimental.pallas import tpu as pltpu

F32 = jnp.float32
BF16 = jnp.bfloat16
I32 = jnp.int32

LANES = 128
HEAD_DIM = 64
HEADS_PER_VREG = LANES // HEAD_DIM
MOBA_HEADS = 8
MOBA_BLOCK = 256
MOBA_TOPK = 3
DIFF_HEADS = 4
ROPE_THETA = 10000.0
N_GROUPS = 4
EXPERTS_PER_GROUP = 8
N_EXPERTS = N_GROUPS * EXPERTS_PER_GROUP
NORM_EPS = 1e-6
NEG_INF = -1e30
SEG = MOBA_HEADS * HEAD_DIM
N_QKV_SEGS = 6
SUM_ROWS = 16

PROJ_ROWS = 512
ATTN_ROWS = 256
ATTN_Q_ROWS = 1024
MERGE_ROWS = 512
SORT_ROWS = 512
EXPERT_ROWS = 256
MOVE_ROWS = 256
VMEM_LIMIT = 56 * 1024 * 1024


def _params(n_axes, vmem=VMEM_LIMIT):
    return pltpu.CompilerParams(dimension_semantics=("arbitrary",) * n_axes,
                                vmem_limit_bytes=vmem)


def _dot(a, b):
    return jnp.dot(a, b, preferred_element_type=F32)


def _dot_nt(a, b):
    return lax.dot_general(a, b, (((1,), (1,)), ((), ())), preferred_element_type=F32)


def _store_token_tiles(ref, x):
    n, d = x.shape
    chunks = d // LANES
    for c in range(chunks):
        ref[pl.ds(c, n, stride=chunks), :] = x[:, c * LANES:(c + 1) * LANES]


def _load_token_tiles(ref, n):
    chunks = ref.shape[0] // n
    return jnp.concatenate([ref[pl.ds(c, n, stride=chunks), :] for c in range(chunks)], axis=1)


def _rms(x, g):
    return x * lax.rsqrt(jnp.mean(x * x, axis=-1, keepdims=True) + NORM_EPS) * g


def _ada_kernel(c_ref, w_ref, b_ref, o_ref):
    c = c_ref[...]
    o_ref[...] = _dot(c * jax.nn.sigmoid(c), w_ref[...]) + b_ref[...]


def _ada(c_pad, w, b):
    rows, d = c_pad.shape
    n = w.shape[1]
    tn = 1536
    return pl.pallas_call(
        _ada_kernel,
        grid=(n // tn,),
        in_specs=[pl.BlockSpec((rows, d), lambda j: (0, 0)),
                  pl.BlockSpec((d, tn), lambda j: (0, j)),
                  pl.BlockSpec((1, tn), lambda j: (0, j))],
        out_specs=pl.BlockSpec((rows, tn), lambda j: (0, j)),
        out_shape=jax.ShapeDtypeStruct((rows, n), F32),
        compiler_params=_params(1),
        name="ada",
    )(c_pad, w, b)


def _rope(x, cos, sin_signed):
    half = HEAD_DIM // 2
    width = x.shape[1]
    lane = lax.broadcasted_iota(I32, x.shape, 1)
    first = (lane & (HEAD_DIM - 1)) < half
    partner = jnp.where(first, pltpu.roll(x, width - half, 1), pltpu.roll(x, half, 1))
    return x * cos + partner * sin_signed


def _proj_kernel(x_ref, mod_ref, g_ref, w_ref, cos_ref, sin_ref,
                 qm_ref, km_ref, vm_ref, qd_ref, kd_ref, vd_ref, kmean_ref, *, d):
    x = x_ref[...]
    sh = mod_ref[0, :, 0:d]
    sc = mod_ref[0, :, d:2 * d]
    h = (_rms(x, g_ref[...]) * (1.0 + sc) + sh).astype(BF16)
    cos = cos_ref[...]
    sin = sin_ref[...]
    scale = HEAD_DIM ** -0.5 * math.log2(math.e)
    outs = (qm_ref, km_ref, vm_ref, qd_ref, kd_ref, vd_ref)
    n_blk = x.shape[0] // ATTN_ROWS
    for seg, o_ref in enumerate(outs):
        y = _dot(h, w_ref[:, seg * SEG:(seg + 1) * SEG])
        if seg in (0, 1, 3, 4):
            y = _rope(y, cos, sin)
        if seg in (0, 3):
            y = y * scale
        if seg == 1:
            for blk in range(n_blk):
                rows = y[blk * MOBA_BLOCK:(blk + 1) * MOBA_BLOCK]
                kmean_ref[0, blk:blk + 1, :] = jnp.mean(rows, axis=0, keepdims=True)
        if seg in (2, 5):
            for blk in range(n_blk):
                for part in range(SEG // LANES):
                    piece = y[blk * ATTN_ROWS:(blk + 1) * ATTN_ROWS, part * LANES:(part + 1) * LANES]
                    o_ref[blk, part * LANES:(part + 1) * LANES, :] = piece.T.astype(BF16)
        else:
            o_ref[...] = y.astype(BF16)


def _proj(x2, mod3, g, w_qkv, cos, sin, seq):
    t, d = x2.shape
    tm = PROJ_ROWS
    per_batch = seq // tm
    row_spec = pl.BlockSpec((tm, SEG), lambda i: (i, 0))
    tab_spec = pl.BlockSpec((tm, SEG), lambda i: (i % per_batch, 0))
    act = jax.ShapeDtypeStruct((t, SEG), BF16)
    act_t = jax.ShapeDtypeStruct((t // ATTN_ROWS, SEG, ATTN_ROWS), BF16)
    t_spec = pl.BlockSpec((tm // ATTN_ROWS, SEG, ATTN_ROWS), lambda i: (i, 0, 0))
    return pl.pallas_call(
        functools.partial(_proj_kernel, d=d),
        grid=(t // tm,),
        in_specs=[pl.BlockSpec((tm, d), lambda i: (i, 0)),
                  pl.BlockSpec((1, 1, mod3.shape[2]), lambda i: (i // per_batch, 0, 0)),
                  pl.BlockSpec((1, d), lambda i: (0, 0)),
                  pl.BlockSpec(w_qkv.shape, lambda i: (0, 0)),
                  tab_spec, tab_spec],
        out_specs=[row_spec, row_spec, t_spec, row_spec, row_spec, t_spec,
                   pl.BlockSpec((1, tm // MOBA_BLOCK, SEG), lambda i: (i, 0, 0))],
        out_shape=[act, act, act_t, act, act, act_t,
                   jax.ShapeDtypeStruct((t // tm, tm // MOBA_BLOCK, SEG), F32)],
        compiler_params=_params(1),
        name="proj",
    )(x2, mod3, g, w_qkv, cos, sin)


def _attend(qs, keys_of_tile, k_ref, vt_ref, v_rows, q_tile, own_only_rows, scratch):
    tq, tk = ATTN_Q_ROWS, ATTN_ROWS
    sub = tq // tk
    ones = jnp.ones((SUM_ROWS, tk), BF16)
    n_soft = len(qs)
    acc_refs, max_refs = scratch[:n_soft], scratch[n_soft:]
    for acc_ref in acc_refs:
        acc_ref[...] = jnp.zeros(acc_ref.shape, F32)

    def scores(j, first_query=0):
        start = pl.multiple_of(j * tk, tk)
        keys = keys_of_tile(j, k_ref[pl.ds(start, tk), :])
        return tuple(_dot(keys, q[:, first_query:]) for q in qs), vt_ref[j]

    def update(tile, mask, maxes, first_query=0):
        s_all, vt = tile
        new = ()
        for n, s in enumerate(s_all):
            if mask is not None:
                s = jnp.where(mask, s, NEG_INF)
            m = max_refs[n][:, first_query:] if maxes is None else maxes[n]
            m_new = jnp.maximum(m, jnp.max(s, axis=0, keepdims=True))
            alpha = jnp.exp2(m - m_new)
            p = jnp.exp2(s - m_new).astype(BF16)
            if maxes is None:
                max_refs[n][:, first_query:] = m_new
            new += (m_new,)
            vt_sum = jnp.concatenate([vt[v_rows[n]], ones], axis=0)
            acc_refs[n][:, first_query:] = alpha * acc_refs[n][:, first_query:] + _dot(vt_sum, p)
        return new

    first_own = q_tile * sub

    def body(group, maxes):
        for tile in [scores(sub * group + b) for b in range(sub)]:
            maxes = update(tile, None, maxes)
        return maxes

    maxes = lax.fori_loop(0, q_tile, body, (jnp.full((1, tq), NEG_INF, F32),) * n_soft)
    for max_ref, m in zip(max_refs, maxes):
        max_ref[...] = m
    own = [scores(first_own + b, b * tk) for b in range(sub)]
    for b in range(sub):
        key = lax.broadcasted_iota(I32, (tk, tq - b * tk), 0)
        qry = lax.broadcasted_iota(I32, (tk, tq - b * tk), 1)
        keep = key <= qry
        if own_only_rows:
            keep = keep | (qry >= tk)
        update(own[b], keep, None, b * tk)
    out = []
    for n, acc_ref in enumerate(acc_refs):
        rows = v_rows[n].stop - v_rows[n].start
        out.append((acc_ref[rows:rows + 1, :], acc_ref[0:rows, :]))
    return out


def _softmax_scratch(n_softmax, features):
    return ([pltpu.VMEM((features + SUM_ROWS, ATTN_Q_ROWS), F32)] * n_softmax
            + [pltpu.VMEM((1, ATTN_Q_ROWS), F32)] * n_softmax)


def _moba_kernel(q_ref, k_ref, v_ref, kmean_ref, o_ref, *scratch, n_blocks):
    tq = ATTN_Q_ROWS
    q_tile = pl.program_id(2)
    q_t = q_ref[...].astype(F32).T
    feat = lax.broadcasted_iota(I32, (LANES, tq), 0)
    blk = lax.broadcasted_iota(I32, (n_blocks, tq), 0)
    qry = lax.broadcasted_iota(I32, (n_blocks, tq), 1)
    own_block = q_tile * (tq // MOBA_BLOCK) + (qry >> int(math.log2(MOBA_BLOCK)))
    kmean = kmean_ref[0]
    km_head = lax.broadcasted_iota(I32, kmean.shape, 1) >> 6
    pad = jnp.zeros((LANES - n_blocks, tq), BF16)
    lane_k = lax.broadcasted_iota(I32, (ATTN_ROWS, LANES), 1)
    qs = []
    for head in range(HEADS_PER_VREG):
        qh = jnp.where((feat >> 6) == head, q_t, 0.0).astype(BF16)
        km = jnp.where(km_head == head, kmean, 0.0)
        km_hi = km.astype(BF16)
        km_lo = (km - km_hi.astype(F32)).astype(BF16)
        gate = _dot(km_hi, qh) + _dot(km_lo, qh)
        gate = jnp.where(blk < own_block, gate, NEG_INF)
        bias = jnp.where(blk == own_block, 0.0, NEG_INF)
        for _ in range(min(MOBA_TOPK, n_blocks)):
            top = jnp.max(gate, axis=0, keepdims=True)
            first = jnp.min(jnp.where(gate == top, blk, n_blocks), axis=0, keepdims=True)
            picked = blk == first
            bias = jnp.where(picked & (blk < own_block), 0.0, bias)
            gate = jnp.where(picked, -jnp.inf, gate)
        qs.append(jnp.concatenate([qh, bias.astype(BF16), pad], axis=0))

    def rhs_of_tile(j, keys):
        return jnp.concatenate([keys, jnp.where(lane_k == j, 1.0, 0.0).astype(BF16)], axis=1)

    v_rows = [slice(h * HEAD_DIM, (h + 1) * HEAD_DIM) for h in range(HEADS_PER_VREG)]
    stats = _attend(qs, rhs_of_tile, k_ref, v_ref, v_rows, q_tile, True, scratch)
    out_t = jnp.concatenate([acc / l for l, acc in stats], axis=0)
    o_ref[...] = out_t.T.astype(BF16)


def _moba(qm, km, vm_t, kmean, batch, seq):
    t = qm.shape[0]
    tq = ATTN_Q_ROWS
    nq = seq // tq
    n_blocks = seq // MOBA_BLOCK
    pairs = MOBA_HEADS // HEADS_PER_VREG
    n_kv = seq // ATTN_ROWS
    return pl.pallas_call(
        functools.partial(_moba_kernel, n_blocks=n_blocks),
        grid=(batch, pairs, nq),
        in_specs=[pl.BlockSpec((tq, LANES), lambda b, p, i: (b * nq + i, p)),
                  pl.BlockSpec((seq, LANES), lambda b, p, i: (b, p)),
                  pl.BlockSpec((n_kv, LANES, ATTN_ROWS), lambda b, p, i: (b, p, 0)),
                  pl.BlockSpec((1, n_blocks, LANES), lambda b, p, i: (b, 0, p))],
        out_specs=pl.BlockSpec((tq, LANES), lambda b, p, i: (b * nq + i, p)),
        out_shape=jax.ShapeDtypeStruct((t, SEG), BF16),
        scratch_shapes=_softmax_scratch(HEADS_PER_VREG, HEAD_DIM),
        compiler_params=_params(3),
        name="moba",
    )(qm, km, vm_t, kmean)


def _diff_kernel(q_ref, k_ref, v_ref, lq1_ref, lk1_ref, lq2_ref, lk2_ref, g_ref, o_ref,
                 *scratch, lam_init):
    tq = ATTN_Q_ROWS
    q_t = q_ref[...].astype(F32).T
    comp = lax.broadcasted_iota(I32, (LANES, tq), 0) >> 6
    qs = [jnp.where(comp == c, q_t, 0.0).astype(BF16) for c in range(2)]
    v_rows = [slice(0, LANES)] * 2
    (l1, a1), (l2, a2) = _attend(qs, lambda j, keys: keys, k_ref, v_ref, v_rows,
                                 pl.program_id(2), False, scratch)
    lam = (jnp.exp(jnp.sum(lq1_ref[...] * lk1_ref[...], axis=1, keepdims=True))
           - jnp.exp(jnp.sum(lq2_ref[...] * lk2_ref[...], axis=1, keepdims=True)) + lam_init)
    o = (a1 / l1 - lam * (a2 / l2)).T
    o_ref[...] = (_rms(o, g_ref[...]) * (1.0 - lam_init)).astype(BF16)


def _diff(qd, kd, vd_t, lq1, lk1, lq2, lk2, g, batch, seq, lam_init):
    t = qd.shape[0]
    tq = ATTN_Q_ROWS
    nq = seq // tq
    n_kv = seq // ATTN_ROWS
    vec = lambda a: pl.BlockSpec(a.shape, lambda b, h, i: (0, 0))
    return pl.pallas_call(
        functools.partial(_diff_kernel, lam_init=lam_init),
        grid=(batch, DIFF_HEADS, nq),
        in_specs=[pl.BlockSpec((tq, LANES), lambda b, h, i: (b * nq + i, h)),
                  pl.BlockSpec((seq, LANES), lambda b, h, i: (b, h)),
                  pl.BlockSpec((n_kv, LANES, ATTN_ROWS), lambda b, h, i: (b, h, 0)),
                  vec(lq1), vec(lk1), vec(lq2), vec(lk2), vec(g)],
        out_specs=pl.BlockSpec((tq, LANES), lambda b, h, i: (b * nq + i, h)),
        out_shape=jax.ShapeDtypeStruct((t, SEG), BF16),
        scratch_shapes=_softmax_scratch(2, LANES),
        compiler_params=_params(3),
        name="diff",
    )(qd, kd, vd_t, lq1, lk1, lq2, lk2, g)


def _merge_kernel(x_ref, om_ref, od_ref, mod_ref, g1n_ref, g2n_ref, wg_ref, wpm_ref, wpd_ref,
                  wout_ref, wr_ref, br_ref, x1_ref, h2_ref, eid_ref, ew_ref, merged, *, d):
    x = x_ref[...]
    sh1 = mod_ref[0, :, 0:d]
    sc1 = mod_ref[0, :, d:2 * d]
    g1 = mod_ref[0, :, 2 * d:3 * d]
    sh2 = mod_ref[0, :, 3 * d:4 * d]
    sc2 = mod_ref[0, :, 4 * d:5 * d]
    h = (_rms(x, g1n_ref[...]) * (1.0 + sc1) + sh1).astype(BF16)
    om = om_ref[...]
    od = od_ref[...]
    chunk = SEG
    for c in range(d // chunk):
        cols = slice(c * chunk, (c + 1) * chunk)
        gm = _dot(h, wg_ref[:, c * chunk:(c + 1) * chunk])
        gd = _dot(h, wg_ref[:, d + c * chunk:d + (c + 1) * chunk])
        ym = _dot(om, wpm_ref[:, cols])
        yd = _dot(od, wpd_ref[:, cols])
        merged[:, cols] = (jax.nn.sigmoid(gm) * ym + jax.nn.sigmoid(gd) * yd).astype(BF16)
    x1 = x + g1 * _dot(merged[...], wout_ref[...])
    x1_ref[...] = x1
    h2 = _rms(x1, g2n_ref[...]) * (1.0 + sc2) + sh2
    _store_token_tiles(h2_ref, h2)

    h2_hi = h2.astype(BF16)
    h2_lo = (h2 - h2_hi.astype(F32)).astype(BF16)
    wr = wr_ref[...]
    wr_hi = wr.astype(BF16)
    wr_lo = (wr - wr_hi.astype(F32)).astype(BF16)
    logits = _dot(h2_hi, wr_hi) + _dot(h2_lo, wr_hi) + _dot(h2_hi, wr_lo) + br_ref[...]
    lane = lax.broadcasted_iota(I32, logits.shape, 1)
    is_group = lane < N_GROUPS
    gl = jnp.where(is_group, logits, -jnp.inf)
    gmax = jnp.max(gl, axis=1, keepdims=True)
    gexp = jnp.exp(gl - gmax)
    g_w = 1.0 / jnp.sum(gexp, axis=1, keepdims=True)
    g_idx = jnp.min(jnp.where(gl == gmax, lane, LANES), axis=1, keepdims=True)
    e_lane = lane - N_GROUPS
    in_group = (e_lane >= g_idx * EXPERTS_PER_GROUP) & (e_lane < (g_idx + 1) * EXPERTS_PER_GROUP)
    el = jnp.where(in_group, logits, -jnp.inf)
    emax = jnp.max(el, axis=1, keepdims=True)
    eexp = jnp.exp(el - emax)
    prob = eexp / jnp.sum(eexp, axis=1, keepdims=True)
    prob = jnp.where(in_group, prob, -1.0)
    p1 = jnp.max(prob, axis=1, keepdims=True)
    i1 = jnp.min(jnp.where(prob == p1, lane, LANES), axis=1, keepdims=True)
    prob2 = jnp.where(lane == i1, -1.0, prob)
    p2 = jnp.max(prob2, axis=1, keepdims=True)
    i2 = jnp.min(jnp.where(prob2 == p2, lane, LANES), axis=1, keepdims=True)
    tot = p1 + p2
    eid_ref[...] = jnp.where(lane == 0, i1 - N_GROUPS, jnp.where(lane == 1, i2 - N_GROUPS, 0))
    ew_ref[...] = jnp.where(lane == 0, g_w * (p1 / tot), jnp.where(lane == 1, g_w * (p2 / tot), 0.0))


def _merge(x2, om, od, mod3, g1n, g2n, wg, wpm, wpd, wout, wr, br, seq):
    t, d = x2.shape
    tm = MERGE_ROWS
    per_batch = seq // tm
    full = lambda a: pl.BlockSpec(a.shape, lambda i: (0,) * a.ndim)
    row = lambda w: pl.BlockSpec((tm, w), lambda i: (i, 0))
    return pl.pallas_call(
        functools.partial(_merge_kernel, d=d),
        grid=(t // tm,),
        in_specs=[row(d), row(SEG), row(SEG),
                  pl.BlockSpec((1, 1, mod3.shape[2]), lambda i: (i // per_batch, 0, 0)),
                  full(g1n), full(g2n), full(wg), full(wpm), full(wpd), full(wout), full(wr), full(br)],
        out_specs=[row(d), pl.BlockSpec((tm * d // LANES, LANES), lambda i: (i, 0)), row(LANES), row(LANES)],
        out_shape=[jax.ShapeDtypeStruct((t, d), F32), jax.ShapeDtypeStruct((t * d // LANES, LANES), F32),
                   jax.ShapeDtypeStruct((t, LANES), I32), jax.ShapeDtypeStruct((t, LANES), F32)],
        scratch_shapes=[pltpu.VMEM((tm, d), BF16)],
        compiler_params=_params(1),
        name="merge",
    )(x2, om, od, mod3, g1n, g2n, wg, wpm, wpd, wout, wr, br)


def _onehots(eid):
    lane = lax.broadcasted_iota(I32, eid.shape, 1)
    return lane, lane == eid[:, 0:1], lane == eid[:, 1:2]


def _rank_kernel(eid_ref, rank_ref, counts_ref, carry):
    tm = eid_ref.shape[0]

    @pl.when(pl.program_id(0) == 0)
    def _():
        carry[...] = jnp.zeros_like(carry)

    lane, oh0, oh1 = _onehots(eid_ref[...])
    used = jnp.where(oh0 | oh1, 1.0, 0.0)
    r = lax.broadcasted_iota(I32, (tm, tm), 0)
    c = lax.broadcasted_iota(I32, (tm, tm), 1)
    before = jnp.where(c < r, 1.0, 0.0).astype(BF16)
    base = carry[0:1, :] + _dot(before, used.astype(BF16))
    r0 = jnp.sum(jnp.where(oh0, base, 0.0), axis=1, keepdims=True)
    r1 = jnp.sum(jnp.where(oh1, base, 0.0), axis=1, keepdims=True)
    rank_ref[...] = jnp.where(lane == 0, r0, jnp.where(lane == 1, r1, 0.0)).astype(I32)
    carry[...] = carry[...] + jnp.sum(used, axis=0, keepdims=True)
    counts_ref[...] = carry[...].astype(I32)


def _rank(eid):
    t = eid.shape[0]
    tm = SORT_ROWS
    return pl.pallas_call(
        _rank_kernel,
        grid=(t // tm,),
        in_specs=[pl.BlockSpec((tm, LANES), lambda i: (i, 0))],
        out_specs=[pl.BlockSpec((tm, LANES), lambda i: (i, 0)),
                   pl.BlockSpec((8, LANES), lambda i: (0, 0))],
        out_shape=[jax.ShapeDtypeStruct((t, LANES), I32), jax.ShapeDtypeStruct((8, LANES), I32)],
        scratch_shapes=[pltpu.VMEM((8, LANES), F32)],
        compiler_params=_params(1),
        name="rank",
    )(eid)


def _segment_ends(counts):
    lane = lax.broadcasted_iota(I32, counts.shape, 1)
    padded = (counts + (EXPERT_ROWS - 1)) & (-EXPERT_ROWS)
    padded = jnp.where(lane < N_EXPERTS, padded, 0)
    ends = padded
    shift = 1
    while shift < N_EXPERTS:
        ends = ends + jnp.where(lane >= shift, pltpu.roll(ends, shift, 1), 0)
        shift *= 2
    return padded, ends


def _pos_kernel(eid_ref, rank_ref, counts_ref, pos_ref, tile_ref, seg_ref):
    padded, ends = _segment_ends(counts_ref[...])
    starts = (ends - padded)[0:1, :]
    lane, oh0, oh1 = _onehots(eid_ref[...])
    rank = rank_ref[...]
    p0 = rank[:, 0:1] + jnp.sum(jnp.where(oh0, starts, 0), axis=1, keepdims=True)
    p1 = rank[:, 1:2] + jnp.sum(jnp.where(oh1, starts, 0), axis=1, keepdims=True)
    pos_ref[...] = jnp.where(lane == 0, p0, jnp.where(lane == 1, p1, 0))

    @pl.when(pl.program_id(0) == 0)
    def _():
        n_tiles = tile_ref.shape[0]
        first_row = lax.broadcasted_iota(I32, (n_tiles, LANES), 0) * EXPERT_ROWS
        elane = lax.broadcasted_iota(I32, (n_tiles, LANES), 1)
        done = jnp.where((ends[0:1, :] <= first_row) & (elane < N_EXPERTS), 1, 0)
        expert = jnp.minimum(jnp.sum(done, axis=1, keepdims=True), N_EXPERTS - 1)
        total = jnp.max(ends[0:1, :], axis=1, keepdims=True)
        live = jnp.where(first_row < total, 1, 0)
        tile_ref[...] = jnp.where(elane == 0, expert, jnp.where(elane == 1, live, 0))
        seg_ref[...] = jnp.concatenate([ends, padded], axis=0)


def _pos(eid, rank, counts, n_tiles):
    t = eid.shape[0]
    tm = SORT_ROWS
    row = pl.BlockSpec((tm, LANES), lambda i: (i, 0))
    return pl.pallas_call(
        _pos_kernel,
        grid=(t // tm,),
        in_specs=[row, row, pl.BlockSpec((8, LANES), lambda i: (0, 0))],
        out_specs=[row, pl.BlockSpec((n_tiles, LANES), lambda i: (0, 0)),
                   pl.BlockSpec((16, LANES), lambda i: (0, 0))],
        out_shape=[jax.ShapeDtypeStruct((t, LANES), I32),
                   jax.ShapeDtypeStruct((n_tiles, LANES), I32),
                   jax.ShapeDtypeStruct((16, LANES), I32)],
        compiler_params=_params(1),
        name="pos",
    )(eid, rank, counts)


def _tokens(ref, first, count, tile):
    start = first * tile
    if not isinstance(start, int):
        start = pl.multiple_of(start, tile)
    return ref.at[pl.ds(start, count * tile)]


def _dispatch_kernel(seg_end_ref, seg_len_ref, pos_ref, h2_ref, xs_ref, zeros, sem, *, tile):
    i = pl.program_id(0)
    tm = MOVE_ROWS

    def clear_copy(first):
        return pltpu.make_async_copy(zeros, _tokens(xs_ref, first, EXPERT_ROWS, tile), sem)

    @pl.when(i == 0)
    def _():
        zeros[...] = jnp.zeros_like(zeros)
        for e in range(N_EXPERTS):
            @pl.when(seg_len_ref[e] > 0)
            def _():
                clear_copy(pl.multiple_of(seg_end_ref[e] - EXPERT_ROWS, EXPERT_ROWS)).start()
        for e in range(N_EXPERTS):
            @pl.when(seg_len_ref[e] > 0)
            def _():
                clear_copy(0).wait()

        first_unused = seg_end_ref[N_EXPERTS - 1] // EXPERT_ROWS
        n_tiles = xs_ref.shape[0] // (EXPERT_ROWS * tile)

        def clear(t, _):
            clear_copy(pl.multiple_of(t * EXPERT_ROWS, EXPERT_ROWS)).start()
            return 0

        def clear_done(t, _):
            clear_copy(0).wait()
            return 0

        lax.fori_loop(first_unused, n_tiles, clear, 0)
        lax.fori_loop(first_unused, n_tiles, clear_done, 0)

    for r in range(tm):
        for k in range(2):
            pltpu.make_async_copy(_tokens(h2_ref, r, 1, tile), _tokens(xs_ref, pos_ref[2 * r + k], 1, tile),
                                  sem).start(priority=k)

    def drain(r, _):
        pltpu.make_async_copy(_tokens(h2_ref, 0, 1, tile), _tokens(xs_ref, 0, 1, tile), sem).wait()
        return 0

    lax.fori_loop(0, 2 * tm, drain, 0, unroll=8)


def _dispatch(seg_end, seg_len, pos_flat, h2_tiles, n_rows, tile):
    t = h2_tiles.shape[0] // tile
    tm = MOVE_ROWS
    grid_spec = pltpu.PrefetchScalarGridSpec(
        num_scalar_prefetch=2,
        grid=(t // tm,),
        in_specs=[pl.BlockSpec((2 * tm,), lambda i, *_: (i,), memory_space=pltpu.SMEM),
                  pl.BlockSpec((tm * tile, LANES), lambda i, *_: (i, 0))],
        out_specs=pl.BlockSpec(memory_space=pl.ANY),
        scratch_shapes=[pltpu.VMEM((EXPERT_ROWS * tile, LANES), F32), pltpu.SemaphoreType.DMA(())],
    )
    return pl.pallas_call(
        functools.partial(_dispatch_kernel, tile=tile),
        grid_spec=grid_spec,
        out_shape=jax.ShapeDtypeStruct((n_rows * tile, LANES), F32),
        compiler_params=pltpu.CompilerParams(dimension_semantics=("arbitrary",),
                                             vmem_limit_bytes=VMEM_LIMIT),
        name="dispatch",
    )(seg_end, seg_len, pos_flat, h2_tiles)


def _expert_kernel(tile_expert_ref, tile_live_ref, xs_ref, wg_ref, wu_ref, wd_ref, o_ref, wg, wu, wd):
    i = pl.program_id(0)

    @pl.when(tile_live_ref[i] > 0)
    def _():
        changed = jnp.logical_or(i == 0, tile_expert_ref[i] != tile_expert_ref[jnp.maximum(i - 1, 0)])

        @pl.when(changed)
        def _():
            wg[...] = wg_ref[0].astype(BF16)
            wu[...] = wu_ref[0].astype(BF16)
            wd[...] = wd_ref[0].astype(BF16)

        x = _load_token_tiles(xs_ref, EXPERT_ROWS).astype(BF16)
        de = wg.shape[1]
        chunk = 2 * LANES
        pre = [(_dot(x, wg[:, c:c + chunk]), _dot(x, wu[:, c:c + chunk])) for c in range(0, de, chunk)]
        hid = [((gate * jax.nn.sigmoid(gate)) * up).astype(BF16) for gate, up in pre]
        tile = o_ref.shape[0] // EXPERT_ROWS
        for oc in range(0, wd.shape[1], chunk):
            out = None
            for n, h in enumerate(hid):
                part = _dot(h, wd[n * chunk:(n + 1) * chunk, oc:oc + chunk])
                out = part if out is None else out + part
            for c in range(chunk // LANES):
                o_ref[pl.ds(oc // LANES + c, EXPERT_ROWS, stride=tile), :] = out[:, c * LANES:(c + 1) * LANES]

    @pl.when(tile_live_ref[i] == 0)
    def _():
        o_ref[...] = jnp.zeros_like(o_ref)


def _experts(tile_expert, tile_live, xs_tiles, w_gate, w_up, w_down):
    d, de = w_gate.shape[1:]
    tile = d // LANES
    n_rows = xs_tiles.shape[0] // tile
    tm = EXPERT_ROWS
    grid_spec = pltpu.PrefetchScalarGridSpec(
        num_scalar_prefetch=2,
        grid=(n_rows // tm,),
        in_specs=[pl.BlockSpec((tm * tile, LANES), lambda i, te, tl: (i, 0)),
                  pl.BlockSpec((1, d, de), lambda i, te, tl: (te[i], 0, 0)),
                  pl.BlockSpec((1, d, de), lambda i, te, tl: (te[i], 0, 0)),
                  pl.BlockSpec((1, de, d), lambda i, te, tl: (te[i], 0, 0))],
        out_specs=pl.BlockSpec((tm * tile, LANES), lambda i, te, tl: (i, 0)),
        scratch_shapes=[pltpu.VMEM((d, de), BF16), pltpu.VMEM((d, de), BF16), pltpu.VMEM((de, d), BF16)],
    )
    return pl.pallas_call(
        _expert_kernel,
        grid_spec=grid_spec,
        out_shape=jax.ShapeDtypeStruct(xs_tiles.shape, F32),
        compiler_params=_params(1),
        name="experts",
    )(tile_expert, tile_live, xs_tiles, w_gate, w_up, w_down)


def _combine_kernel(pos_ref, pos_next_ref, ys_ref, x1_ref, ew_ref, mod_ref, g_ref, o_ref, buf, sem, *,
                    d, final_norm):
    i = pl.program_id(0)
    n = pl.num_programs(0)
    tm = MOVE_ROWS
    tile = d // LANES

    def fetch(pref, slot):
        for r in range(tm):
            for k in range(2):
                pltpu.make_async_copy(_tokens(ys_ref, pref[2 * r + k], 1, tile),
                                      _tokens(buf.at[slot, k], r, 1, tile),
                                      sem.at[slot]).start(priority=k)

    @pl.when(i == 0)
    def _():
        fetch(pos_ref, 0)

    for parity in range(2):
        @pl.when((i + 1 < n) & ((i + 1) % 2 == parity))
        def _():
            fetch(pos_next_ref, parity)

    slot = i % 2

    def drain(r, _):
        pltpu.make_async_copy(_tokens(ys_ref, 0, 1, tile), _tokens(buf.at[slot, 0], 0, 1, tile),
                              sem.at[slot]).wait()
        return 0

    lax.fori_loop(0, 2 * tm, drain, 0, unroll=8)

    ew = ew_ref[...]
    y = (ew[:, 0:1] * _load_token_tiles(buf.at[slot, 0], tm)
         + ew[:, 1:2] * _load_token_tiles(buf.at[slot, 1], tm))
    g2 = mod_ref[0, :, 5 * d:6 * d]
    x2 = x1_ref[...] + g2 * y
    o_ref[...] = _rms(x2, g_ref[...]) if final_norm else x2


def _combine(pos_flat, ys, x1, ew, mod3, final_g, seq, final_norm):
    t, d = x1.shape
    tm = MOVE_ROWS
    n = t // tm
    per_batch = seq // tm
    return pl.pallas_call(
        functools.partial(_combine_kernel, d=d, final_norm=final_norm),
        grid=(n,),
        in_specs=[pl.BlockSpec((2 * tm,), lambda i: (i,), memory_space=pltpu.SMEM),
                  pl.BlockSpec((2 * tm,), lambda i: (jnp.minimum(i + 1, n - 1),), memory_space=pltpu.SMEM),
                  pl.BlockSpec(memory_space=pl.ANY),
                  pl.BlockSpec((tm, d), lambda i: (i, 0)),
                  pl.BlockSpec((tm, LANES), lambda i: (i, 0)),
                  pl.BlockSpec((1, 1, mod3.shape[2]), lambda i: (i // per_batch, 0, 0)),
                  pl.BlockSpec((1, d), lambda i: (0, 0))],
        out_specs=pl.BlockSpec((tm, d), lambda i: (i, 0)),
        out_shape=jax.ShapeDtypeStruct((t, d), F32),
        scratch_shapes=[pltpu.VMEM((2, 2, tm * d // LANES, LANES), F32), pltpu.SemaphoreType.DMA((2,))],
        compiler_params=_params(1),
        name="combine",
    )(pos_flat, pos_flat, ys, x1, ew, mod3, final_g)


def _rope_tables(seq):
    inv = 1.0 / (ROPE_THETA ** (jnp.arange(0, HEAD_DIM, 2, dtype=F32) / HEAD_DIM))
    ang = jnp.arange(seq, dtype=F32)[:, None] * inv[None, :]
    cos, sin = jnp.cos(ang), jnp.sin(ang)
    cos_head = jnp.concatenate([cos, cos], axis=1)
    sin_head = jnp.concatenate([-sin, sin], axis=1)
    reps = SEG // HEAD_DIM
    return jnp.tile(cos_head, (1, reps)), jnp.tile(sin_head, (1, reps))


def kernel(x, c, w_ada, b_ada, norm1_g, w_in, lambda_q1, lambda_k1, lambda_q2, lambda_k2,
           diff_subln_g, w_proj_moba, w_proj_diff, w_out, norm2_g, w_group, b_group,
           w_expert, b_expert, w_gate, w_up, w_down, final_g):
    batch, seq, d = x.shape
    depth = w_ada.shape[0]
    t = batch * seq
    assert seq % PROJ_ROWS == 0 and seq % MOBA_BLOCK == 0 and seq // MOBA_BLOCK <= LANES
    assert ATTN_ROWS == MOBA_BLOCK and d % SEG == 0 and t % SORT_ROWS == 0 and batch <= 8
    assert seq % ATTN_Q_ROWS == 0 and ATTN_Q_ROWS % ATTN_ROWS == 0
    assert EXPERT_ROWS & (EXPERT_ROWS - 1) == 0
    n_rows = 2 * t + N_EXPERTS * EXPERT_ROWS
    n_tiles = n_rows // EXPERT_ROWS
    cos, sin = _rope_tables(seq)
    c_pad = jnp.zeros((8, d), F32).at[:batch].set(c)
    xf = x.reshape(t, d)
    row = lambda v: v.reshape(1, -1)
    for l in range(depth):
        mod = _ada(c_pad, w_ada[l], row(b_ada[l]))
        mod3 = mod[:batch].reshape(batch, 1, 6 * d)
        w_qkv = w_in[l][:, :N_QKV_SEGS * SEG].astype(BF16)
        w_gates = w_in[l][:, N_QKV_SEGS * SEG:].astype(BF16)
        qm, km, vm, qd, kd, vd, kmean = _proj(xf, mod3, row(norm1_g[l]), w_qkv, cos, sin, seq)
        kmean = kmean.reshape(batch, seq // MOBA_BLOCK, SEG)
        om = _moba(qm, km, vm, kmean, batch, seq)
        lam_init = 0.8 - 0.6 * math.exp(-0.3 * l)
        od = _diff(qd, kd, vd, row(lambda_q1[l]), row(lambda_k1[l]), row(lambda_q2[l]),
                   row(lambda_k2[l]), row(diff_subln_g[l]), batch, seq, lam_init)
        w_router = jnp.zeros((d, LANES), F32)
        w_router = w_router.at[:, :N_GROUPS].set(w_group[l])
        w_router = w_router.at[:, N_GROUPS:N_GROUPS + N_EXPERTS].set(w_expert[l])
        b_router = jnp.zeros((1, LANES), F32)
        b_router = b_router.at[0, :N_GROUPS].set(b_group[l])
        b_router = b_router.at[0, N_GROUPS:N_GROUPS + N_EXPERTS].set(b_expert[l])
        x1, h2, eid, ew = _merge(xf, om, od, mod3, row(norm1_g[l]), row(norm2_g[l]), w_gates,
                                 w_proj_moba[l].astype(BF16), w_proj_diff[l].astype(BF16),
                                 w_out[l].astype(BF16), w_router, b_router, seq)
        rank, counts = _rank(eid)
        pos, tiles, segs = _pos(eid, rank, counts, n_tiles)
        pos_flat = pos[:, :2].reshape(2 * t)
        xs = _dispatch(segs[0, :N_EXPERTS], segs[8, :N_EXPERTS], pos_flat, h2, n_rows, d // LANES)
        ys = _experts(tiles[:, 0], tiles[:, 1], xs, w_gate[l], w_up[l], w_down[l])
        xf = _combine(pos_flat, ys, x1, ew, mod3, row(final_g), seq, final_norm=(l == depth - 1))
    return xf.reshape(batch, seq, d)
```

```python
import functools
import math

import jax
import jax.numpy as jnp
from jax import lax
from jax.experimental import pallas as pl
from jax.experimental.pallas import tpu as pltpu

F32 = jnp.float32
BF16 = jnp.bfloat16
I32 = jnp.int32

LANES = 128
HEAD_DIM = 64
HEADS_PER_VREG = LANES // HEAD_DIM
MOBA_HEADS = 8
MOBA_BLOCK = 256
MOBA_TOPK = 3
DIFF_HEADS = 4
ROPE_THETA = 10000.0
N_GROUPS = 4
EXPERTS_PER_GROUP = 8
N_EXPERTS = N_GROUPS * EXPERTS_PER_GROUP
NORM_EPS = 1e-6
NEG_INF = -1e30
SEG = MOBA_HEADS * HEAD_DIM
N_QKV_SEGS = 6
SUM_ROWS = 16

PROJ_ROWS = 512
ATTN_ROWS = 256
ATTN_Q_ROWS = 1024
MERGE_ROWS = 512
SORT_ROWS = 512
EXPERT_ROWS = 256
MOVE_ROWS = 256
VMEM_LIMIT = 56 * 1024 * 1024


def _params(n_axes, vmem=VMEM_LIMIT):
    return pltpu.CompilerParams(dimension_semantics=("arbitrary",) * n_axes,
                                vmem_limit_bytes=vmem)


def _dot(a, b):
    return jnp.dot(a, b, preferred_element_type=F32)


def _dot_nt(a, b):
    return lax.dot_general(a, b, (((1,), (1,)), ((), ())), preferred_element_type=F32)


def _store_token_tiles(ref, x):
    n, d = x.shape
    chunks = d // LANES
    for c in range(chunks):
        ref[pl.ds(c, n, stride=chunks), :] = x[:, c * LANES:(c + 1) * LANES]


def _load_token_tiles(ref, n):
    chunks = ref.shape[0] // n
    return jnp.concatenate([ref[pl.ds(c, n, stride=chunks), :] for c in range(chunks)], axis=1)


def _rms(x, g):
    return x * lax.rsqrt(jnp.mean(x * x, axis=-1, keepdims=True) + NORM_EPS) * g


def _ada_kernel(c_ref, w_ref, b_ref, o_ref):
    c = c_ref[...]
    o_ref[...] = _dot(c * jax.nn.sigmoid(c), w_ref[...]) + b_ref[...]


def _ada(c_pad, w, b):
    rows, d = c_pad.shape
    n = w.shape[1]
    tn = 1536
    return pl.pallas_call(
        _ada_kernel,
        grid=(n // tn,),
        in_specs=[pl.BlockSpec((rows, d), lambda j: (0, 0)),
                  pl.BlockSpec((d, tn), lambda j: (0, j)),
                  pl.BlockSpec((1, tn), lambda j: (0, j))],
        out_specs=pl.BlockSpec((rows, tn), lambda j: (0, j)),
        out_shape=jax.ShapeDtypeStruct((rows, n), F32),
        compiler_params=_params(1),
        name="ada",
    )(c_pad, w, b)


def _rope(x, cos, sin_signed):
    half = HEAD_DIM // 2
    width = x.shape[1]
    lane = lax.broadcasted_iota(I32, x.shape, 1)
    first = (lane & (HEAD_DIM - 1)) < half
    partner = jnp.where(first, pltpu.roll(x, width - half, 1), pltpu.roll(x, half, 1))
    return x * cos + partner * sin_signed


def _proj_kernel(x_ref, mod_ref, g_ref, w_ref, cos_ref, sin_ref,
                 qm_ref, km_ref, vm_ref, qd_ref, kd_ref, vd_ref, kmean_ref, *, d):
    x = x_ref[...]
    sh = mod_ref[0, :, 0:d]
    sc = mod_ref[0, :, d:2 * d]
    h = (_rms(x, g_ref[...]) * (1.0 + sc) + sh).astype(BF16)
    cos = cos_ref[...]
    sin = sin_ref[...]
    scale = HEAD_DIM ** -0.5 * math.log2(math.e)
    outs = (qm_ref, km_ref, vm_ref, qd_ref, kd_ref, vd_ref)
    n_blk = x.shape[0] // ATTN_ROWS
    for seg, o_ref in enumerate(outs):
        y = _dot(h, w_ref[:, seg * SEG:(seg + 1) * SEG])
        if seg in (0, 1, 3, 4):
            y = _rope(y, cos, sin)
        if seg in (0, 3):
            y = y * scale
        if seg == 1:
            for blk in range(n_blk):
                rows = y[blk * MOBA_BLOCK:(blk + 1) * MOBA_BLOCK]
                kmean_ref[0, blk:blk + 1, :] = jnp.mean(rows, axis=0, keepdims=True)
        if seg in (2, 5):
            for blk in range(n_blk):
                for part in range(SEG // LANES):
                    piece = y[blk * ATTN_ROWS:(blk + 1) * ATTN_ROWS, part * LANES:(part + 1) * LANES]
                    o_ref[blk, part * LANES:(part + 1) * LANES, :] = piece.T.astype(BF16)
        else:
            o_ref[...] = y.astype(BF16)


def _proj(x2, mod3, g, w_qkv, cos, sin, seq):
    t, d = x2.shape
    tm = PROJ_ROWS
    per_batch = seq // tm
    row_spec = pl.BlockSpec((tm, SEG), lambda i: (i, 0))
    tab_spec = pl.BlockSpec((tm, SEG), lambda i: (i % per_batch, 0))
    act = jax.ShapeDtypeStruct((t, SEG), BF16)
    act_t = jax.ShapeDtypeStruct((t // ATTN_ROWS, SEG, ATTN_ROWS), BF16)
    t_spec = pl.BlockSpec((tm // ATTN_ROWS, SEG, ATTN_ROWS), lambda i: (i, 0, 0))
    return pl.pallas_call(
        functools.partial(_proj_kernel, d=d),
        grid=(t // tm,),
        in_specs=[pl.BlockSpec((tm, d), lambda i: (i, 0)),
                  pl.BlockSpec((1, 1, mod3.shape[2]), lambda i: (i // per_batch, 0, 0)),
                  pl.BlockSpec((1, d), lambda i: (0, 0)),
                  pl.BlockSpec(w_qkv.shape, lambda i: (0, 0)),
                  tab_spec, tab_spec],
        out_specs=[row_spec, row_spec, t_spec, row_spec, row_spec, t_spec,
                   pl.BlockSpec((1, tm // MOBA_BLOCK, SEG), lambda i: (i, 0, 0))],
        out_shape=[act, act, act_t, act, act, act_t,
                   jax.ShapeDtypeStruct((t // tm, tm // MOBA_BLOCK, SEG), F32)],
        compiler_params=_params(1),
        name="proj",
    )(x2, mod3, g, w_qkv, cos, sin)


def _attend(qs, keys_of_tile, k_ref, vt_ref, v_rows, q_tile, own_only_rows, scratch):
    tq, tk = ATTN_Q_ROWS, ATTN_ROWS
    sub = tq // tk
    ones = jnp.ones((SUM_ROWS, tk), BF16)
    n_soft = len(qs)
    acc_refs, max_refs = scratch[:n_soft], scratch[n_soft:]
    for acc_ref in acc_refs:
        acc_ref[...] = jnp.zeros(acc_ref.shape, F32)

    def scores(j, first_query=0):
        start = pl.multiple_of(j * tk, tk)
        keys = keys_of_tile(j, k_ref[pl.ds(start, tk), :])
        return tuple(_dot(keys, q[:, first_query:]) for q in qs), vt_ref[j]

    def update(tile, mask, maxes, first_query=0):
        s_all, vt = tile
        new = ()
        for n, s in enumerate(s_all):
            if mask is not None:
                s = jnp.where(mask, s, NEG_INF)
            m = max_refs[n][:, first_query:] if maxes is None else maxes[n]
            m_new = jnp.maximum(m, jnp.max(s, axis=0, keepdims=True))
            alpha = jnp.exp2(m - m_new)
            p = jnp.exp2(s - m_new).astype(BF16)
            if maxes is None:
                max_refs[n][:, first_query:] = m_new
            new += (m_new,)
            vt_sum = jnp.concatenate([vt[v_rows[n]], ones], axis=0)
            acc_refs[n][:, first_query:] = alpha * acc_refs[n][:, first_query:] + _dot(vt_sum, p)
        return new

    first_own = q_tile * sub

    def body(group, maxes):
        for tile in [scores(sub * group + b) for b in range(sub)]:
            maxes = update(tile, None, maxes)
        return maxes

    maxes = lax.fori_loop(0, q_tile, body, (jnp.full((1, tq), NEG_INF, F32),) * n_soft)
    for max_ref, m in zip(max_refs, maxes):
        max_ref[...] = m
    own = [scores(first_own + b, b * tk) for b in range(sub)]
    for b in range(sub):
        key = lax.broadcasted_iota(I32, (tk, tq - b * tk), 0)
        qry = lax.broadcasted_iota(I32, (tk, tq - b * tk), 1)
        keep = key <= qry
        if own_only_rows:
            keep = keep | (qry >= tk)
        update(own[b], keep, None, b * tk)
    out = []
    for n, acc_ref in enumerate(acc_refs):
        rows = v_rows[n].stop - v_rows[n].start
        out.append((acc_ref[rows:rows + 1, :], acc_ref[0:rows, :]))
    return out


def _softmax_scratch(n_softmax, features):
    return ([pltpu.VMEM((features + SUM_ROWS, ATTN_Q_ROWS), F32)] * n_softmax
            + [pltpu.VMEM((1, ATTN_Q_ROWS), F32)] * n_softmax)


def _moba_kernel(q_ref, k_ref, v_ref, kmean_ref, o_ref, *scratch, n_blocks):
    tq = ATTN_Q_ROWS
    q_tile = pl.program_id(2)
    q_t = q_ref[...].astype(F32).T
    feat = lax.broadcasted_iota(I32, (LANES, tq), 0)
    blk = lax.broadcasted_iota(I32, (n_blocks, tq), 0)
    qry = lax.broadcasted_iota(I32, (n_blocks, tq), 1)
    own_block = q_tile * (tq // MOBA_BLOCK) + (qry >> int(math.log2(MOBA_BLOCK)))
    kmean = kmean_ref[0]
    km_head = lax.broadcasted_iota(I32, kmean.shape, 1) >> 6
    pad = jnp.zeros((LANES - n_blocks, tq), BF16)
    lane_k = lax.broadcasted_iota(I32, (ATTN_ROWS, LANES), 1)
    qs = []
    for head in range(HEADS_PER_VREG):
        qh = jnp.where((feat >> 6) == head, q_t, 0.0).astype(BF16)
        km = jnp.where(km_head == head, kmean, 0.0)
        km_hi = km.astype(BF16)
        km_lo = (km - km_hi.astype(F32)).astype(BF16)
        gate = _dot(km_hi, qh) + _dot(km_lo, qh)
        gate = jnp.where(blk < own_block, gate, NEG_INF)
        bias = jnp.where(blk == own_block, 0.0, NEG_INF)
        for _ in range(min(MOBA_TOPK, n_blocks)):
            top = jnp.max(gate, axis=0, keepdims=True)
            first = jnp.min(jnp.where(gate == top, blk, n_blocks), axis=0, keepdims=True)
            picked = blk == first
            bias = jnp.where(picked & (blk < own_block), 0.0, bias)
            gate = jnp.where(picked, -jnp.inf, gate)
        qs.append(jnp.concatenate([qh, bias.astype(BF16), pad], axis=0))

    def rhs_of_tile(j, keys):
        return jnp.concatenate([keys, jnp.where(lane_k == j, 1.0, 0.0).astype(BF16)], axis=1)

    v_rows = [slice(h * HEAD_DIM, (h + 1) * HEAD_DIM) for h in range(HEADS_PER_VREG)]
    stats = _attend(qs, rhs_of_tile, k_ref, v_ref, v_rows, q_tile, True, scratch)
    out_t = jnp.concatenate([acc / l for l, acc in stats], axis=0)
    o_ref[...] = out_t.T.astype(BF16)


def _moba(qm, km, vm_t, kmean, batch, seq):
    t = qm.shape[0]
    tq = ATTN_Q_ROWS
    nq = seq // tq
    n_blocks = seq // MOBA_BLOCK
    pairs = MOBA_HEADS // HEADS_PER_VREG
    n_kv = seq // ATTN_ROWS
    return pl.pallas_call(
        functools.partial(_moba_kernel, n_blocks=n_blocks),
        grid=(batch, pairs, nq),
        in_specs=[pl.BlockSpec((tq, LANES), lambda b, p, i: (b * nq + i, p)),
                  pl.BlockSpec((seq, LANES), lambda b, p, i: (b, p)),
                  pl.BlockSpec((n_kv, LANES, ATTN_ROWS), lambda b, p, i: (b, p, 0)),
                  pl.BlockSpec((1, n_blocks, LANES), lambda b, p, i: (b, 0, p))],
        out_specs=pl.BlockSpec((tq, LANES), lambda b, p, i: (b * nq + i, p)),
        out_shape=jax.ShapeDtypeStruct((t, SEG), BF16),
        scratch_shapes=_softmax_scratch(HEADS_PER_VREG, HEAD_DIM),
        compiler_params=_params(3),
        name="moba",
    )(qm, km, vm_t, kmean)


def _diff_kernel(q_ref, k_ref, v_ref, lq1_ref, lk1_ref, lq2_ref, lk2_ref, g_ref, o_ref,
                 *scratch, lam_init):
    tq = ATTN_Q_ROWS
    q_t = q_ref[...].astype(F32).T
    comp = lax.broadcasted_iota(I32, (LANES, tq), 0) >> 6
    qs = [jnp.where(comp == c, q_t, 0.0).astype(BF16) for c in range(2)]
    v_rows = [slice(0, LANES)] * 2
    (l1, a1), (l2, a2) = _attend(qs, lambda j, keys: keys, k_ref, v_ref, v_rows,
                                 pl.program_id(2), False, scratch)
    lam = (jnp.exp(jnp.sum(lq1_ref[...] * lk1_ref[...], axis=1, keepdims=True))
           - jnp.exp(jnp.sum(lq2_ref[...] * lk2_ref[...], axis=1, keepdims=True)) + lam_init)
    o = (a1 / l1 - lam * (a2 / l2)).T
    o_ref[...] = (_rms(o, g_ref[...]) * (1.0 - lam_init)).astype(BF16)


def _diff(qd, kd, vd_t, lq1, lk1, lq2, lk2, g, batch, seq, lam_init):
    t = qd.shape[0]
    tq = ATTN_Q_ROWS
    nq = seq // tq
    n_kv = seq // ATTN_ROWS
    vec = lambda a: pl.BlockSpec(a.shape, lambda b, h, i: (0, 0))
    return pl.pallas_call(
        functools.partial(_diff_kernel, lam_init=lam_init),
        grid=(batch, DIFF_HEADS, nq),
        in_specs=[pl.BlockSpec((tq, LANES), lambda b, h, i: (b * nq + i, h)),
                  pl.BlockSpec((seq, LANES), lambda b, h, i: (b, h)),
                  pl.BlockSpec((n_kv, LANES, ATTN_ROWS), lambda b, h, i: (b, h, 0)),
                  vec(lq1), vec(lk1), vec(lq2), vec(lk2), vec(g)],
        out_specs=pl.BlockSpec((tq, LANES), lambda b, h, i: (b * nq + i, h)),
        out_shape=jax.ShapeDtypeStruct((t, SEG), BF16),
        scratch_shapes=_softmax_scratch(2, LANES),
        compiler_params=_params(3),
        name="diff",
    )(qd, kd, vd_t, lq1, lk1, lq2, lk2, g)


def _merge_kernel(x_ref, om_ref, od_ref, mod_ref, g1n_ref, g2n_ref, wg_ref, wpm_ref, wpd_ref,
                  wout_ref, wr_ref, br_ref, x1_ref, h2_ref, eid_ref, ew_ref, counts_ref, merged, counts,
                  *, d):
    x = x_ref[...]
    sh1 = mod_ref[0, :, 0:d]
    sc1 = mod_ref[0, :, d:2 * d]
    g1 = mod_ref[0, :, 2 * d:3 * d]
    sh2 = mod_ref[0, :, 3 * d:4 * d]
    sc2 = mod_ref[0, :, 4 * d:5 * d]
    h = (_rms(x, g1n_ref[...]) * (1.0 + sc1) + sh1).astype(BF16)
    om = om_ref[...]
    od = od_ref[...]
    chunk = SEG
    for c in range(d // chunk):
        cols = slice(c * chunk, (c + 1) * chunk)
        gm = _dot(h, wg_ref[:, c * chunk:(c + 1) * chunk])
        gd = _dot(h, wg_ref[:, d + c * chunk:d + (c + 1) * chunk])
        ym = _dot(om, wpm_ref[:, cols])
        yd = _dot(od, wpd_ref[:, cols])
        merged[:, cols] = (jax.nn.sigmoid(gm) * ym + jax.nn.sigmoid(gd) * yd).astype(BF16)
    x1 = x + g1 * _dot(merged[...], wout_ref[...])
    x1_ref[...] = x1
    h2 = _rms(x1, g2n_ref[...]) * (1.0 + sc2) + sh2
    _store_token_tiles(h2_ref, h2)

    h2_hi = h2.astype(BF16)
    h2_lo = (h2 - h2_hi.astype(F32)).astype(BF16)
    wr = wr_ref[...]
    wr_hi = wr.astype(BF16)
    wr_lo = (wr - wr_hi.astype(F32)).astype(BF16)
    logits = _dot(h2_hi, wr_hi) + _dot(h2_lo, wr_hi) + _dot(h2_hi, wr_lo) + br_ref[...]
    lane = lax.broadcasted_iota(I32, logits.shape, 1)
    is_group = lane < N_GROUPS
    gl = jnp.where(is_group, logits, -jnp.inf)
    gmax = jnp.max(gl, axis=1, keepdims=True)
    gexp = jnp.exp(gl - gmax)
    g_w = 1.0 / jnp.sum(gexp, axis=1, keepdims=True)
    g_idx = jnp.min(jnp.where(gl == gmax, lane, LANES), axis=1, keepdims=True)
    e_lane = lane - N_GROUPS
    in_group = (e_lane >= g_idx * EXPERTS_PER_GROUP) & (e_lane < (g_idx + 1) * EXPERTS_PER_GROUP)
    el = jnp.where(in_group, logits, -jnp.inf)
    emax = jnp.max(el, axis=1, keepdims=True)
    eexp = jnp.exp(el - emax)
    prob = eexp / jnp.sum(eexp, axis=1, keepdims=True)
    prob = jnp.where(in_group, prob, -1.0)
    p1 = jnp.max(prob, axis=1, keepdims=True)
    i1 = jnp.min(jnp.where(prob == p1, lane, LANES), axis=1, keepdims=True)
    prob2 = jnp.where(lane == i1, -1.0, prob)
    p2 = jnp.max(prob2, axis=1, keepdims=True)
    i2 = jnp.min(jnp.where(prob2 == p2, lane, LANES), axis=1, keepdims=True)
    tot = p1 + p2
    eid = jnp.where(lane == 0, i1 - N_GROUPS, jnp.where(lane == 1, i2 - N_GROUPS, 0))
    eid_ref[...] = eid.T[0:8, :]

    @pl.when(pl.program_id(0) == 0)
    def _():
        counts[...] = jnp.zeros_like(counts)

    chosen = jnp.where((lane == i1) | (lane == i2), 1.0, 0.0)
    per_lane = jnp.broadcast_to(jnp.sum(chosen, axis=0, keepdims=True), counts.shape)
    counts[...] = counts[...] + pltpu.roll(per_lane, LANES - N_GROUPS, 1)
    counts_ref[...] = counts[...].astype(I32)
    ew_ref[...] = jnp.where(lane == 0, g_w * (p1 / tot), jnp.where(lane == 1, g_w * (p2 / tot), 0.0))


def _merge(x2, om, od, mod3, g1n, g2n, wg, wpm, wpd, wout, wr, br, seq):
    t, d = x2.shape
    tm = MERGE_ROWS
    per_batch = seq // tm
    full = lambda a: pl.BlockSpec(a.shape, lambda i: (0,) * a.ndim)
    row = lambda w: pl.BlockSpec((tm, w), lambda i: (i, 0))
    return pl.pallas_call(
        functools.partial(_merge_kernel, d=d),
        grid=(t // tm,),
        in_specs=[row(d), row(SEG), row(SEG),
                  pl.BlockSpec((1, 1, mod3.shape[2]), lambda i: (i // per_batch, 0, 0)),
                  full(g1n), full(g2n), full(wg), full(wpm), full(wpd), full(wout), full(wr), full(br)],
        out_specs=[row(d), pl.BlockSpec((tm * d // LANES, LANES), lambda i: (i, 0)),
                   pl.BlockSpec((8, tm), lambda i: (0, i)), row(LANES),
                   pl.BlockSpec((8, LANES), lambda i: (0, 0))],
        out_shape=[jax.ShapeDtypeStruct((t, d), F32), jax.ShapeDtypeStruct((t * d // LANES, LANES), F32),
                   jax.ShapeDtypeStruct((8, t), I32), jax.ShapeDtypeStruct((t, LANES), F32),
                   jax.ShapeDtypeStruct((8, LANES), I32)],
        scratch_shapes=[pltpu.VMEM((tm, d), BF16), pltpu.VMEM((8, LANES), F32)],
        compiler_params=_params(1),
        name="merge",
    )(x2, om, od, mod3, g1n, g2n, wg, wpm, wpd, wout, wr, br)


def _segment_ends(counts):
    lane = lax.broadcasted_iota(I32, counts.shape, 1)
    padded = (counts + (EXPERT_ROWS - 1)) & (-EXPERT_ROWS)
    padded = jnp.where(lane < N_EXPERTS, padded, 0)
    ends = padded
    shift = 1
    while shift < N_EXPERTS:
        ends = ends + jnp.where(lane >= shift, pltpu.roll(ends, shift, 1), 0)
        shift *= 2
    return padded, ends


def _pos_kernel(eid_ref, counts_ref, pos_ref, tile_ref, seg_ref, carry):
    tm = eid_ref.shape[1]

    @pl.when(pl.program_id(0) == 0)
    def _():
        carry[...] = jnp.zeros_like(carry)

    padded, ends = _segment_ends(counts_ref[...])
    starts = (ends - padded).astype(F32).T[0:N_EXPERTS, 0:1]
    eid = eid_ref[...]
    expert = lax.broadcasted_iota(I32, (N_EXPERTS, tm), 0)
    oh0 = expert == eid[0:1, :]
    oh1 = expert == eid[1:2, :]
    used = jnp.where(oh0 | oh1, 1.0, 0.0)
    r = lax.broadcasted_iota(I32, (tm, tm), 0)
    c = lax.broadcasted_iota(I32, (tm, tm), 1)
    earlier = jnp.where(r < c, 1.0, 0.0).astype(BF16)
    base = starts + carry[:, 0:1] + _dot(used.astype(BF16), earlier)
    p0 = jnp.sum(jnp.where(oh0, base, 0.0), axis=0, keepdims=True)
    p1 = jnp.sum(jnp.where(oh1, base, 0.0), axis=0, keepdims=True)
    slot = lax.broadcasted_iota(I32, pos_ref.shape, 0)
    pos_ref[...] = jnp.where(slot == 0, p0, jnp.where(slot == 1, p1, 0.0)).astype(I32)
    carry[...] = carry[...] + jnp.sum(used, axis=1, keepdims=True)

    @pl.when(pl.program_id(0) == 0)
    def _():
        n_tiles = tile_ref.shape[0]
        first_row = lax.broadcasted_iota(I32, (n_tiles, LANES), 0) * EXPERT_ROWS
        elane = lax.broadcasted_iota(I32, (n_tiles, LANES), 1)
        done = jnp.where((ends[0:1, :] <= first_row) & (elane < N_EXPERTS), 1, 0)
        expert = jnp.minimum(jnp.sum(done, axis=1, keepdims=True), N_EXPERTS - 1)
        total = jnp.max(ends[0:1, :], axis=1, keepdims=True)
        live = jnp.where(first_row < total, 1, 0)
        tile_ref[...] = jnp.where(elane == 0, expert, jnp.where(elane == 1, live, 0))
        seg_ref[...] = jnp.concatenate([ends, padded], axis=0)


def _pos(eid_t, counts, n_tiles):
    t = eid_t.shape[1]
    tm = SORT_ROWS
    col = pl.BlockSpec((8, tm), lambda i: (0, i))
    return pl.pallas_call(
        _pos_kernel,
        grid=(t // tm,),
        in_specs=[col, pl.BlockSpec((8, LANES), lambda i: (0, 0))],
        out_specs=[col, pl.BlockSpec((n_tiles, LANES), lambda i: (0, 0)),
                   pl.BlockSpec((16, LANES), lambda i: (0, 0))],
        out_shape=[jax.ShapeDtypeStruct((8, t), I32),
                   jax.ShapeDtypeStruct((n_tiles, LANES), I32),
                   jax.ShapeDtypeStruct((16, LANES), I32)],
        scratch_shapes=[pltpu.VMEM((N_EXPERTS, LANES), F32)],
        compiler_params=_params(1),
        name="pos",
    )(eid_t, counts)


def _tokens(ref, first, count, tile):
    start = first * tile
    if not isinstance(start, int):
        start = pl.multiple_of(start, tile)
    return ref.at[pl.ds(start, count * tile)]


def _dispatch_kernel(seg_end_ref, seg_len_ref, pos0_ref, pos1_ref, h2_ref, xs_ref, zeros, sem, *, tile):
    i = pl.program_id(0)
    pos_refs = (pos0_ref, pos1_ref)
    tm = MOVE_ROWS

    def clear_copy(first):
        return pltpu.make_async_copy(zeros, _tokens(xs_ref, first, EXPERT_ROWS, tile), sem)

    @pl.when(i == 0)
    def _():
        zeros[...] = jnp.zeros_like(zeros)
        for e in range(N_EXPERTS):
            @pl.when(seg_len_ref[e] > 0)
            def _():
                clear_copy(pl.multiple_of(seg_end_ref[e] - EXPERT_ROWS, EXPERT_ROWS)).start()
        for e in range(N_EXPERTS):
            @pl.when(seg_len_ref[e] > 0)
            def _():
                clear_copy(0).wait()

        first_unused = seg_end_ref[N_EXPERTS - 1] // EXPERT_ROWS
        n_tiles = xs_ref.shape[0] // (EXPERT_ROWS * tile)

        def clear(t, _):
            clear_copy(pl.multiple_of(t * EXPERT_ROWS, EXPERT_ROWS)).start()
            return 0

        def clear_done(t, _):
            clear_copy(0).wait()
            return 0

        lax.fori_loop(first_unused, n_tiles, clear, 0)
        lax.fori_loop(first_unused, n_tiles, clear_done, 0)

    for r in range(tm):
        for k in range(2):
            pltpu.make_async_copy(_tokens(h2_ref, r, 1, tile), _tokens(xs_ref, pos_refs[k][r], 1, tile),
                                  sem).start(priority=k)

    def drain(r, _):
        pltpu.make_async_copy(_tokens(h2_ref, 0, 1, tile), _tokens(xs_ref, 0, 1, tile), sem).wait()
        return 0

    lax.fori_loop(0, 2 * tm, drain, 0, unroll=8)


def _dispatch(seg_end, seg_len, pos_flat, h2_tiles, n_rows, tile):
    t = h2_tiles.shape[0] // tile
    tm = MOVE_ROWS
    n = t // tm
    grid_spec = pltpu.PrefetchScalarGridSpec(
        num_scalar_prefetch=2,
        grid=(n,),
        in_specs=[pl.BlockSpec((tm,), lambda i, *_: (i,), memory_space=pltpu.SMEM),
                  pl.BlockSpec((tm,), lambda i, *_: (n + i,), memory_space=pltpu.SMEM),
                  pl.BlockSpec((tm * tile, LANES), lambda i, *_: (i, 0))],
        out_specs=pl.BlockSpec(memory_space=pl.ANY),
        scratch_shapes=[pltpu.VMEM((EXPERT_ROWS * tile, LANES), F32), pltpu.SemaphoreType.DMA(())],
    )
    return pl.pallas_call(
        functools.partial(_dispatch_kernel, tile=tile),
        grid_spec=grid_spec,
        out_shape=jax.ShapeDtypeStruct((n_rows * tile, LANES), F32),
        compiler_params=pltpu.CompilerParams(dimension_semantics=("arbitrary",),
                                             vmem_limit_bytes=VMEM_LIMIT),
        name="dispatch",
    )(seg_end, seg_len, pos_flat, pos_flat, h2_tiles)


def _expert_kernel(tile_expert_ref, tile_live_ref, xs_ref, wg_ref, wu_ref, wd_ref, o_ref, wg, wu, wd):
    i = pl.program_id(0)

    @pl.when(tile_live_ref[i] > 0)
    def _():
        changed = jnp.logical_or(i == 0, tile_expert_ref[i] != tile_expert_ref[jnp.maximum(i - 1, 0)])

        @pl.when(changed)
        def _():
            wg[...] = wg_ref[0].astype(BF16)
            wu[...] = wu_ref[0].astype(BF16)
            wd[...] = wd_ref[0].astype(BF16)

        x = _load_token_tiles(xs_ref, EXPERT_ROWS).astype(BF16)
        de = wg.shape[1]
        chunk = 2 * LANES
        pre = [(_dot(x, wg[:, c:c + chunk]), _dot(x, wu[:, c:c + chunk])) for c in range(0, de, chunk)]
        hid = [((gate * jax.nn.sigmoid(gate)) * up).astype(BF16) for gate, up in pre]
        tile = o_ref.shape[0] // EXPERT_ROWS
        for oc in range(0, wd.shape[1], chunk):
            out = None
            for n, h in enumerate(hid):
                part = _dot(h, wd[n * chunk:(n + 1) * chunk, oc:oc + chunk])
                out = part if out is None else out + part
            for c in range(chunk // LANES):
                o_ref[pl.ds(oc // LANES + c, EXPERT_ROWS, stride=tile), :] = out[:, c * LANES:(c + 1) * LANES]

    @pl.when(tile_live_ref[i] == 0)
    def _():
        o_ref[...] = jnp.zeros_like(o_ref)


def _experts(tile_expert, tile_live, xs_tiles, w_gate, w_up, w_down):
    d, de = w_gate.shape[1:]
    tile = d // LANES
    n_rows = xs_tiles.shape[0] // tile
    tm = EXPERT_ROWS
    grid_spec = pltpu.PrefetchScalarGridSpec(
        num_scalar_prefetch=2,
        grid=(n_rows // tm,),
        in_specs=[pl.BlockSpec((tm * tile, LANES), lambda i, te, tl: (i, 0)),
                  pl.BlockSpec((1, d, de), lambda i, te, tl: (te[i], 0, 0)),
                  pl.BlockSpec((1, d, de), lambda i, te, tl: (te[i], 0, 0)),
                  pl.BlockSpec((1, de, d), lambda i, te, tl: (te[i], 0, 0))],
        out_specs=pl.BlockSpec((tm * tile, LANES), lambda i, te, tl: (i, 0)),
        scratch_shapes=[pltpu.VMEM((d, de), BF16), pltpu.VMEM((d, de), BF16), pltpu.VMEM((de, d), BF16)],
    )
    return pl.pallas_call(
        _expert_kernel,
        grid_spec=grid_spec,
        out_shape=jax.ShapeDtypeStruct(xs_tiles.shape, F32),
        compiler_params=_params(1),
        name="experts",
    )(tile_expert, tile_live, xs_tiles, w_gate, w_up, w_down)


def _combine_kernel(pos0_ref, pos1_ref, pos0_next_ref, pos1_next_ref, ys_ref, x1_ref, ew_ref, mod_ref, g_ref,
                    o_ref, buf, sem, *, d, final_norm):
    i = pl.program_id(0)
    n = pl.num_programs(0)
    tm = MOVE_ROWS
    tile = d // LANES

    def fetch(prefs, slot):
        for r in range(tm):
            for k in range(2):
                pltpu.make_async_copy(_tokens(ys_ref, prefs[k][r], 1, tile),
                                      _tokens(buf.at[slot, k], r, 1, tile),
                                      sem.at[slot]).start(priority=k)

    @pl.when(i == 0)
    def _():
        fetch((pos0_ref, pos1_ref), 0)

    for parity in range(2):
        @pl.when((i + 1 < n) & ((i + 1) % 2 == parity))
        def _():
            fetch((pos0_next_ref, pos1_next_ref), parity)

    slot = i % 2

    def drain(r, _):
        pltpu.make_async_copy(_tokens(ys_ref, 0, 1, tile), _tokens(buf.at[slot, 0], 0, 1, tile),
                              sem.at[slot]).wait()
        return 0

    lax.fori_loop(0, 2 * tm, drain, 0, unroll=8)

    ew = ew_ref[...]
    y = (ew[:, 0:1] * _load_token_tiles(buf.at[slot, 0], tm)
         + ew[:, 1:2] * _load_token_tiles(buf.at[slot, 1], tm))
    g2 = mod_ref[0, :, 5 * d:6 * d]
    x2 = x1_ref[...] + g2 * y
    o_ref[...] = _rms(x2, g_ref[...]) if final_norm else x2


def _combine(pos_flat, ys, x1, ew, mod3, final_g, seq, final_norm):
    t, d = x1.shape
    tm = MOVE_ROWS
    n = t // tm
    per_batch = seq // tm
    return pl.pallas_call(
        functools.partial(_combine_kernel, d=d, final_norm=final_norm),
        grid=(n,),
        in_specs=[pl.BlockSpec((tm,), lambda i: (i,), memory_space=pltpu.SMEM),
                  pl.BlockSpec((tm,), lambda i: (n + i,), memory_space=pltpu.SMEM),
                  pl.BlockSpec((tm,), lambda i: (jnp.minimum(i + 1, n - 1),), memory_space=pltpu.SMEM),
                  pl.BlockSpec((tm,), lambda i: (n + jnp.minimum(i + 1, n - 1),), memory_space=pltpu.SMEM),
                  pl.BlockSpec(memory_space=pl.ANY),
                  pl.BlockSpec((tm, d), lambda i: (i, 0)),
                  pl.BlockSpec((tm, LANES), lambda i: (i, 0)),
                  pl.BlockSpec((1, 1, mod3.shape[2]), lambda i: (i // per_batch, 0, 0)),
                  pl.BlockSpec((1, d), lambda i: (0, 0))],
        out_specs=pl.BlockSpec((tm, d), lambda i: (i, 0)),
        out_shape=jax.ShapeDtypeStruct((t, d), F32),
        scratch_shapes=[pltpu.VMEM((2, 2, tm * d // LANES, LANES), F32), pltpu.SemaphoreType.DMA((2,))],
        compiler_params=_params(1),
        name="combine",
    )(pos_flat, pos_flat, pos_flat, pos_flat, ys, x1, ew, mod3, final_g)


def _rope_tables(seq):
    inv = 1.0 / (ROPE_THETA ** (jnp.arange(0, HEAD_DIM, 2, dtype=F32) / HEAD_DIM))
    ang = jnp.arange(seq, dtype=F32)[:, None] * inv[None, :]
    cos, sin = jnp.cos(ang), jnp.sin(ang)
    cos_head = jnp.concatenate([cos, cos], axis=1)
    sin_head = jnp.concatenate([-sin, sin], axis=1)
    reps = SEG // HEAD_DIM
    return jnp.tile(cos_head, (1, reps)), jnp.tile(sin_head, (1, reps))


def kernel(x, c, w_ada, b_ada, norm1_g, w_in, lambda_q1, lambda_k1, lambda_q2, lambda_k2,
           diff_subln_g, w_proj_moba, w_proj_diff, w_out, norm2_g, w_group, b_group,
           w_expert, b_expert, w_gate, w_up, w_down, final_g):
    batch, seq, d = x.shape
    depth = w_ada.shape[0]
    t = batch * seq
    assert seq % PROJ_ROWS == 0 and seq % MOBA_BLOCK == 0 and seq // MOBA_BLOCK <= LANES
    assert ATTN_ROWS == MOBA_BLOCK and d % SEG == 0 and t % SORT_ROWS == 0 and batch <= 8
    assert seq % ATTN_Q_ROWS == 0 and ATTN_Q_ROWS % ATTN_ROWS == 0
    assert EXPERT_ROWS & (EXPERT_ROWS - 1) == 0
    n_rows = 2 * t + N_EXPERTS * EXPERT_ROWS
    n_tiles = n_rows // EXPERT_ROWS
    cos, sin = _rope_tables(seq)
    c_pad = jnp.zeros((8, d), F32).at[:batch].set(c)
    xf = x.reshape(t, d)
    row = lambda v: v.reshape(1, -1)
    for l in range(depth):
        mod = _ada(c_pad, w_ada[l], row(b_ada[l]))
        mod3 = mod[:batch].reshape(batch, 1, 6 * d)
        w_qkv = w_in[l][:, :N_QKV_SEGS * SEG].astype(BF16)
        w_gates = w_in[l][:, N_QKV_SEGS * SEG:].astype(BF16)
        qm, km, vm, qd, kd, vd, kmean = _proj(xf, mod3, row(norm1_g[l]), w_qkv, cos, sin, seq)
        kmean = kmean.reshape(batch, seq // MOBA_BLOCK, SEG)
        om = _moba(qm, km, vm, kmean, batch, seq)
        lam_init = 0.8 - 0.6 * math.exp(-0.3 * l)
        od = _diff(qd, kd, vd, row(lambda_q1[l]), row(lambda_k1[l]), row(lambda_q2[l]),
                   row(lambda_k2[l]), row(diff_subln_g[l]), batch, seq, lam_init)
        w_router = jnp.zeros((d, LANES), F32)
        w_router = w_router.at[:, :N_GROUPS].set(w_group[l])
        w_router = w_router.at[:, N_GROUPS:N_GROUPS + N_EXPERTS].set(w_expert[l])
        b_router = jnp.zeros((1, LANES), F32)
        b_router = b_router.at[0, :N_GROUPS].set(b_group[l])
        b_router = b_router.at[0, N_GROUPS:N_GROUPS + N_EXPERTS].set(b_expert[l])
        x1, h2, eid_t, ew, counts = _merge(xf, om, od, mod3, row(norm1_g[l]), row(norm2_g[l]), w_gates,
                                 w_proj_moba[l].astype(BF16), w_proj_diff[l].astype(BF16),
                                 w_out[l].astype(BF16), w_router, b_router, seq)
        pos_t, tiles, segs = _pos(eid_t, counts, n_tiles)
        pos_flat = pos_t[:2].reshape(2 * t)
        xs = _dispatch(segs[0, :N_EXPERTS], segs[8, :N_EXPERTS], pos_flat, h2, n_rows, d // LANES)
        ys = _experts(tiles[:, 0], tiles[:, 1], xs, w_gate[l], w_up[l], w_down[l])
        xf = _combine(pos_flat, ys, x1, ew, mod3, row(final_g), seq, final_norm=(l == depth - 1))
    return xf.reshape(batch, seq, d)
```

```python
import functools
import math

import jax
import jax.numpy as jnp
from jax import lax
from jax.experimental import pallas as pl
from jax.experimental.pallas import tpu as pltpu

F32 = jnp.float32
BF16 = jnp.bfloat16
I32 = jnp.int32

LANES = 128
HEAD_DIM = 64
HEADS_PER_VREG = LANES // HEAD_DIM
MOBA_HEADS = 8
MOBA_BLOCK = 256
MOBA_TOPK = 3
DIFF_HEADS = 4
ROPE_THETA = 10000.0
N_GROUPS = 4
EXPERTS_PER_GROUP = 8
N_EXPERTS = N_GROUPS * EXPERTS_PER_GROUP
NORM_EPS = 1e-6
NEG_INF = -1e30
SEG = MOBA_HEADS * HEAD_DIM
N_QKV_SEGS = 6
SUM_ROWS = 16

PROJ_ROWS = 512
ATTN_ROWS = 256
ATTN_Q_ROWS = 1024
MERGE_ROWS = 512
SORT_ROWS = 512
EXPERT_ROWS = 512
MOVE_ROWS = 256
VMEM_LIMIT = 56 * 1024 * 1024


def _params(n_axes, vmem=VMEM_LIMIT):
    return pltpu.CompilerParams(dimension_semantics=("arbitrary",) * n_axes,
                                vmem_limit_bytes=vmem)


def _dot(a, b):
    return jnp.dot(a, b, preferred_element_type=F32)


def _dot_nt(a, b):
    return lax.dot_general(a, b, (((1,), (1,)), ((), ())), preferred_element_type=F32)


def _store_token_tiles(ref, x):
    n, d = x.shape
    chunks = d // LANES
    for c in range(chunks):
        ref[pl.ds(c, n, stride=chunks), :] = x[:, c * LANES:(c + 1) * LANES]


def _load_token_tiles(ref, n):
    chunks = ref.shape[0] // n
    return jnp.concatenate([ref[pl.ds(c, n, stride=chunks), :] for c in range(chunks)], axis=1)


def _rms(x, g):
    return x * lax.rsqrt(jnp.mean(x * x, axis=-1, keepdims=True) + NORM_EPS) * g


def _ada_kernel(c_ref, w_ref, b_ref, o_ref):
    c = c_ref[...]
    o_ref[...] = _dot(c * jax.nn.sigmoid(c), w_ref[...]) + b_ref[...]


def _ada(c_pad, w, b):
    rows, d = c_pad.shape
    n = w.shape[1]
    tn = 1536
    return pl.pallas_call(
        _ada_kernel,
        grid=(n // tn,),
        in_specs=[pl.BlockSpec((rows, d), lambda j: (0, 0)),
                  pl.BlockSpec((d, tn), lambda j: (0, j)),
                  pl.BlockSpec((1, tn), lambda j: (0, j))],
        out_specs=pl.BlockSpec((rows, tn), lambda j: (0, j)),
        out_shape=jax.ShapeDtypeStruct((rows, n), F32),
        compiler_params=_params(1),
        name="ada",
    )(c_pad, w, b)


def _rope(x, cos, sin_signed):
    half = HEAD_DIM // 2
    width = x.shape[1]
    lane = lax.broadcasted_iota(I32, x.shape, 1)
    first = (lane & (HEAD_DIM - 1)) < half
    partner = jnp.where(first, pltpu.roll(x, width - half, 1), pltpu.roll(x, half, 1))
    return x * cos + partner * sin_signed


def _proj_kernel(x_ref, mod_ref, g_ref, w_ref, cos_ref, sin_ref,
                 qm_ref, km_ref, vm_ref, qd_ref, kd_ref, vd_ref, kmean_ref, *, d):
    x = x_ref[...]
    sh = mod_ref[0, :, 0:d]
    sc = mod_ref[0, :, d:2 * d]
    h = (_rms(x, g_ref[...]) * (1.0 + sc) + sh).astype(BF16)
    cos = jnp.concatenate([cos_ref[...]] * (SEG // LANES), axis=1)
    sin = jnp.concatenate([sin_ref[...]] * (SEG // LANES), axis=1)
    scale = HEAD_DIM ** -0.5 * math.log2(math.e)
    outs = (qm_ref, km_ref, vm_ref, qd_ref, kd_ref, vd_ref)
    n_blk = x.shape[0] // ATTN_ROWS
    for seg, o_ref in enumerate(outs):
        y = _dot(h, w_ref[:, seg * SEG:(seg + 1) * SEG])
        if seg in (0, 1, 3, 4):
            y = _rope(y, cos, sin)
        if seg in (0, 3):
            y = y * scale
        if seg == 1:
            for blk in range(n_blk):
                rows = y[blk * MOBA_BLOCK:(blk + 1) * MOBA_BLOCK]
                kmean_ref[0, blk:blk + 1, :] = jnp.mean(rows, axis=0, keepdims=True)
        if seg in (2, 5):
            for blk in range(n_blk):
                for part in range(SEG // LANES):
                    piece = y[blk * ATTN_ROWS:(blk + 1) * ATTN_ROWS, part * LANES:(part + 1) * LANES]
                    o_ref[blk, part * LANES:(part + 1) * LANES, :] = piece.T.astype(BF16)
        else:
            o_ref[...] = y.astype(BF16)


def _proj(x2, mod3, g, w_qkv, cos, sin, seq):
    t, d = x2.shape
    tm = PROJ_ROWS
    per_batch = seq // tm
    row_spec = pl.BlockSpec((tm, SEG), lambda i: (i, 0))
    tab_spec = pl.BlockSpec((tm, LANES), lambda i: (i % per_batch, 0))
    act = jax.ShapeDtypeStruct((t, SEG), BF16)
    act_t = jax.ShapeDtypeStruct((t // ATTN_ROWS, SEG, ATTN_ROWS), BF16)
    t_spec = pl.BlockSpec((tm // ATTN_ROWS, SEG, ATTN_ROWS), lambda i: (i, 0, 0))
    return pl.pallas_call(
        functools.partial(_proj_kernel, d=d),
        grid=(t // tm,),
        in_specs=[pl.BlockSpec((tm, d), lambda i: (i, 0)),
                  pl.BlockSpec((1, 1, mod3.shape[2]), lambda i: (i // per_batch, 0, 0)),
                  pl.BlockSpec((1, d), lambda i: (0, 0)),
                  pl.BlockSpec(w_qkv.shape, lambda i: (0, 0)),
                  tab_spec, tab_spec],
        out_specs=[row_spec, row_spec, t_spec, row_spec, row_spec, t_spec,
                   pl.BlockSpec((1, tm // MOBA_BLOCK, SEG), lambda i: (i, 0, 0))],
        out_shape=[act, act, act_t, act, act, act_t,
                   jax.ShapeDtypeStruct((t // tm, tm // MOBA_BLOCK, SEG), F32)],
        compiler_params=_params(1),
        name="proj",
    )(x2, mod3, g, w_qkv, cos, sin)


def _attend(qs, keys_of_tile, k_ref, vt_ref, v_rows, q_tile, scratch):
    tq, tk = ATTN_Q_ROWS, ATTN_ROWS
    sub = tq // tk
    ones = jnp.ones((SUM_ROWS, tk), BF16)
    n_soft = len(qs)
    acc_refs, max_refs = scratch[:n_soft], scratch[n_soft:]
    for acc_ref in acc_refs:
        acc_ref[...] = jnp.zeros(acc_ref.shape, F32)

    def scores(j, first_query=0):
        start = pl.multiple_of(j * tk, tk)
        keys = keys_of_tile(j, k_ref[pl.ds(start, tk), :])
        return tuple(_dot(keys, q[:, first_query:]) for q in qs), vt_ref[j]

    def update(tile, mask, maxes, first_query=0):
        s_all, vt = tile
        new = ()
        for n, s in enumerate(s_all):
            if mask is not None:
                own = jnp.where(mask, s[:, :tk], NEG_INF)
                s = own if s.shape[1] == tk else jnp.concatenate([own, s[:, tk:]], axis=1)
            m = max_refs[n][:, first_query:] if maxes is None else maxes[n]
            m_new = jnp.maximum(m, jnp.max(s, axis=0, keepdims=True))
            alpha = jnp.exp2(m - m_new)
            p = jnp.exp2(s - m_new).astype(BF16)
            if maxes is None:
                max_refs[n][:, first_query:] = m_new
            new += (m_new,)
            vt_sum = jnp.concatenate([vt[v_rows[n]], ones], axis=0)
            acc_refs[n][:, first_query:] = alpha * acc_refs[n][:, first_query:] + _dot(vt_sum, p)
        return new

    first_own = q_tile * sub

    def body(group, maxes):
        for tile in [scores(sub * group + b) for b in range(sub)]:
            maxes = update(tile, None, maxes)
        return maxes

    maxes = lax.fori_loop(0, q_tile, body, (jnp.full((1, tq), NEG_INF, F32),) * n_soft)
    for max_ref, m in zip(max_refs, maxes):
        max_ref[...] = m
    own = [scores(first_own + b, b * tk) for b in range(sub)]
    causal = lax.broadcasted_iota(I32, (tk, tk), 0) <= lax.broadcasted_iota(I32, (tk, tk), 1)
    for b in range(sub):
        update(own[b], causal, None, b * tk)
    out = []
    for n, acc_ref in enumerate(acc_refs):
        rows = v_rows[n].stop - v_rows[n].start
        out.append((acc_ref[rows:rows + 1, :], acc_ref[0:rows, :]))
    return out


def _softmax_scratch(n_softmax, features):
    return ([pltpu.VMEM((features + SUM_ROWS, ATTN_Q_ROWS), F32)] * n_softmax
            + [pltpu.VMEM((1, ATTN_Q_ROWS), F32)] * n_softmax)


def _moba_kernel(q_ref, k_ref, v_ref, kmean_ref, o_ref, *scratch, n_blocks):
    tq = ATTN_Q_ROWS
    q_tile = pl.program_id(2)
    q_t = q_ref[...].astype(F32).T
    feat = lax.broadcasted_iota(I32, (LANES, tq), 0)
    blk = lax.broadcasted_iota(I32, (n_blocks, tq), 0)
    qry = lax.broadcasted_iota(I32, (n_blocks, tq), 1)
    own_block = q_tile * (tq // MOBA_BLOCK) + (qry >> int(math.log2(MOBA_BLOCK)))
    kmean = kmean_ref[0]
    km_head = lax.broadcasted_iota(I32, kmean.shape, 1) >> 6
    pad = jnp.zeros((LANES - n_blocks, tq), BF16)
    lane_k = lax.broadcasted_iota(I32, (ATTN_ROWS, LANES), 1)
    qs = []
    for head in range(HEADS_PER_VREG):
        qh = jnp.where((feat >> 6) == head, q_t, 0.0).astype(BF16)
        km = jnp.where(km_head == head, kmean, 0.0)
        km_hi = km.astype(BF16)
        km_lo = (km - km_hi.astype(F32)).astype(BF16)
        gate = _dot(km_hi, qh) + _dot(km_lo, qh)
        gate = jnp.where(blk < own_block, gate, NEG_INF)
        bias = jnp.where(blk == own_block, 0.0, NEG_INF)
        for _ in range(min(MOBA_TOPK, n_blocks)):
            top = jnp.max(gate, axis=0, keepdims=True)
            first = jnp.min(jnp.where(gate == top, blk, n_blocks), axis=0, keepdims=True)
            picked = blk == first
            bias = jnp.where(picked & (blk < own_block), 0.0, bias)
            gate = jnp.where(picked, -jnp.inf, gate)
        qs.append(jnp.concatenate([qh, bias.astype(BF16), pad], axis=0))

    def rhs_of_tile(j, keys):
        return jnp.concatenate([keys, jnp.where(lane_k == j, 1.0, 0.0).astype(BF16)], axis=1)

    v_rows = [slice(h * HEAD_DIM, (h + 1) * HEAD_DIM) for h in range(HEADS_PER_VREG)]
    stats = _attend(qs, rhs_of_tile, k_ref, v_ref, v_rows, q_tile, scratch)
    out_t = jnp.concatenate([acc / l for l, acc in stats], axis=0)
    o_ref[...] = out_t.T.astype(BF16)


def _moba(qm, km, vm_t, kmean, batch, seq):
    t = qm.shape[0]
    tq = ATTN_Q_ROWS
    nq = seq // tq
    n_blocks = seq // MOBA_BLOCK
    pairs = MOBA_HEADS // HEADS_PER_VREG
    n_kv = seq // ATTN_ROWS
    return pl.pallas_call(
        functools.partial(_moba_kernel, n_blocks=n_blocks),
        grid=(batch, pairs, nq),
        in_specs=[pl.BlockSpec((tq, LANES), lambda b, p, i: (b * nq + i, p)),
                  pl.BlockSpec((seq, LANES), lambda b, p, i: (b, p)),
                  pl.BlockSpec((n_kv, LANES, ATTN_ROWS), lambda b, p, i: (b, p, 0)),
                  pl.BlockSpec((1, n_blocks, LANES), lambda b, p, i: (b, 0, p))],
        out_specs=pl.BlockSpec((tq, LANES), lambda b, p, i: (b * nq + i, p)),
        out_shape=jax.ShapeDtypeStruct((t, SEG), BF16),
        scratch_shapes=_softmax_scratch(HEADS_PER_VREG, HEAD_DIM),
        compiler_params=_params(3),
        name="moba",
    )(qm, km, vm_t, kmean)


def _diff_kernel(q_ref, k_ref, v_ref, lq1_ref, lk1_ref, lq2_ref, lk2_ref, g_ref, o_ref,
                 *scratch, lam_init):
    tq = ATTN_Q_ROWS
    q_t = q_ref[...].astype(F32).T
    comp = lax.broadcasted_iota(I32, (LANES, tq), 0) >> 6
    qs = [jnp.where(comp == c, q_t, 0.0).astype(BF16) for c in range(2)]
    v_rows = [slice(0, LANES)] * 2
    (l1, a1), (l2, a2) = _attend(qs, lambda j, keys: keys, k_ref, v_ref, v_rows,
                                 pl.program_id(2), scratch)
    lam = (jnp.exp(jnp.sum(lq1_ref[...] * lk1_ref[...], axis=1, keepdims=True))
           - jnp.exp(jnp.sum(lq2_ref[...] * lk2_ref[...], axis=1, keepdims=True)) + lam_init)
    o = (a1 / l1 - lam * (a2 / l2)).T
    o_ref[...] = (_rms(o, g_ref[...]) * (1.0 - lam_init)).astype(BF16)


def _diff(qd, kd, vd_t, lq1, lk1, lq2, lk2, g, batch, seq, lam_init):
    t = qd.shape[0]
    tq = ATTN_Q_ROWS
    nq = seq // tq
    n_kv = seq // ATTN_ROWS
    vec = lambda a: pl.BlockSpec(a.shape, lambda b, h, i: (0, 0))
    return pl.pallas_call(
        functools.partial(_diff_kernel, lam_init=lam_init),
        grid=(batch, DIFF_HEADS, nq),
        in_specs=[pl.BlockSpec((tq, LANES), lambda b, h, i: (b * nq + i, h)),
                  pl.BlockSpec((seq, LANES), lambda b, h, i: (b, h)),
                  pl.BlockSpec((n_kv, LANES, ATTN_ROWS), lambda b, h, i: (b, h, 0)),
                  vec(lq1), vec(lk1), vec(lq2), vec(lk2), vec(g)],
        out_specs=pl.BlockSpec((tq, LANES), lambda b, h, i: (b * nq + i, h)),
        out_shape=jax.ShapeDtypeStruct((t, SEG), BF16),
        scratch_shapes=_softmax_scratch(2, LANES),
        compiler_params=_params(3),
        name="diff",
    )(qd, kd, vd_t, lq1, lk1, lq2, lk2, g)


def _merge_kernel(x_ref, om_ref, od_ref, mod_ref, g1n_ref, g2n_ref, wg_ref, wpm_ref, wpd_ref,
                  wout_ref, wr_ref, br_ref, x1_ref, h2_ref, eid_ref, ew_ref, counts_ref, merged, counts,
                  *, d):
    x = x_ref[...]
    sh1 = mod_ref[0, :, 0:d]
    sc1 = mod_ref[0, :, d:2 * d]
    g1 = mod_ref[0, :, 2 * d:3 * d]
    sh2 = mod_ref[0, :, 3 * d:4 * d]
    sc2 = mod_ref[0, :, 4 * d:5 * d]
    h = (_rms(x, g1n_ref[...]) * (1.0 + sc1) + sh1).astype(BF16)
    om = om_ref[...]
    od = od_ref[...]
    chunk = SEG
    for c in range(d // chunk):
        cols = slice(c * chunk, (c + 1) * chunk)
        gm = _dot(h, wg_ref[:, c * chunk:(c + 1) * chunk])
        gd = _dot(h, wg_ref[:, d + c * chunk:d + (c + 1) * chunk])
        ym = _dot(om, wpm_ref[:, cols])
        yd = _dot(od, wpd_ref[:, cols])
        merged[:, cols] = (jax.nn.sigmoid(gm) * ym + jax.nn.sigmoid(gd) * yd).astype(BF16)
    x1 = x + g1 * _dot(merged[...], wout_ref[...])
    x1_ref[...] = x1
    h2 = _rms(x1, g2n_ref[...]) * (1.0 + sc2) + sh2
    _store_token_tiles(h2_ref, h2)

    h2_hi = h2.astype(BF16)
    h2_lo = (h2 - h2_hi.astype(F32)).astype(BF16)
    wr = wr_ref[...]
    wr_hi = wr.astype(BF16)
    wr_lo = (wr - wr_hi.astype(F32)).astype(BF16)
    logits = _dot(h2_hi, wr_hi) + _dot(h2_lo, wr_hi) + _dot(h2_hi, wr_lo) + br_ref[...]
    lane = lax.broadcasted_iota(I32, logits.shape, 1)
    is_group = lane < N_GROUPS
    gl = jnp.where(is_group, logits, -jnp.inf)
    gmax = jnp.max(gl, axis=1, keepdims=True)
    gexp = jnp.exp(gl - gmax)
    g_w = 1.0 / jnp.sum(gexp, axis=1, keepdims=True)
    g_idx = jnp.min(jnp.where(gl == gmax, lane, LANES), axis=1, keepdims=True)
    e_lane = lane - N_GROUPS
    in_group = (e_lane >= g_idx * EXPERTS_PER_GROUP) & (e_lane < (g_idx + 1) * EXPERTS_PER_GROUP)
    el = jnp.where(in_group, logits, -jnp.inf)
    emax = jnp.max(el, axis=1, keepdims=True)
    eexp = jnp.exp(el - emax)
    prob = eexp / jnp.sum(eexp, axis=1, keepdims=True)
    prob = jnp.where(in_group, prob, -1.0)
    p1 = jnp.max(prob, axis=1, keepdims=True)
    i1 = jnp.min(jnp.where(prob == p1, lane, LANES), axis=1, keepdims=True)
    prob2 = jnp.where(lane == i1, -1.0, prob)
    p2 = jnp.max(prob2, axis=1, keepdims=True)
    i2 = jnp.min(jnp.where(prob2 == p2, lane, LANES), axis=1, keepdims=True)
    tot = p1 + p2
    eid = jnp.where(lane == 0, i1 - N_GROUPS, jnp.where(lane == 1, i2 - N_GROUPS, 0))
    eid_ref[...] = eid.T[0:8, :]

    @pl.when(pl.program_id(0) == 0)
    def _():
        counts[...] = jnp.zeros_like(counts)

    chosen = jnp.where((lane == i1) | (lane == i2), 1.0, 0.0)
    per_lane = jnp.broadcast_to(jnp.sum(chosen, axis=0, keepdims=True), counts.shape)
    counts[...] = counts[...] + pltpu.roll(per_lane, LANES - N_GROUPS, 1)
    counts_ref[...] = counts[...].astype(I32)
    ew_ref[...] = jnp.where(lane == 0, g_w * (p1 / tot), jnp.where(lane == 1, g_w * (p2 / tot), 0.0))


def _merge(x2, om, od, mod3, g1n, g2n, wg, wpm, wpd, wout, wr, br, seq):
    t, d = x2.shape
    tm = MERGE_ROWS
    per_batch = seq // tm
    full = lambda a: pl.BlockSpec(a.shape, lambda i: (0,) * a.ndim)
    row = lambda w: pl.BlockSpec((tm, w), lambda i: (i, 0))
    return pl.pallas_call(
        functools.partial(_merge_kernel, d=d),
        grid=(t // tm,),
        in_specs=[row(d), row(SEG), row(SEG),
                  pl.BlockSpec((1, 1, mod3.shape[2]), lambda i: (i // per_batch, 0, 0)),
                  full(g1n), full(g2n), full(wg), full(wpm), full(wpd), full(wout), full(wr), full(br)],
        out_specs=[row(d), pl.BlockSpec((tm * d // LANES, LANES), lambda i: (i, 0)),
                   pl.BlockSpec((8, tm), lambda i: (0, i)), row(LANES),
                   pl.BlockSpec((8, LANES), lambda i: (0, 0))],
        out_shape=[jax.ShapeDtypeStruct((t, d), F32), jax.ShapeDtypeStruct((t * d // LANES, LANES), F32),
                   jax.ShapeDtypeStruct((8, t), I32), jax.ShapeDtypeStruct((t, LANES), F32),
                   jax.ShapeDtypeStruct((8, LANES), I32)],
        scratch_shapes=[pltpu.VMEM((tm, d), BF16), pltpu.VMEM((8, LANES), F32)],
        compiler_params=_params(1),
        name="merge",
    )(x2, om, od, mod3, g1n, g2n, wg, wpm, wpd, wout, wr, br)


def _segment_ends(counts):
    lane = lax.broadcasted_iota(I32, counts.shape, 1)
    padded = (counts + (EXPERT_ROWS - 1)) & (-EXPERT_ROWS)
    padded = jnp.where(lane < N_EXPERTS, padded, 0)
    ends = padded
    shift = 1
    while shift < N_EXPERTS:
        ends = ends + jnp.where(lane >= shift, pltpu.roll(ends, shift, 1), 0)
        shift *= 2
    return padded, ends


def _pos_kernel(eid_ref, counts_ref, pos_ref, tile_ref, seg_ref, carry):
    tm = eid_ref.shape[1]

    @pl.when(pl.program_id(0) == 0)
    def _():
        carry[...] = jnp.zeros_like(carry)

    padded, ends = _segment_ends(counts_ref[...])
    starts = (ends - padded).astype(F32).T[0:N_EXPERTS, 0:1]
    eid = eid_ref[...]
    expert = lax.broadcasted_iota(I32, (N_EXPERTS, tm), 0)
    oh0 = expert == eid[0:1, :]
    oh1 = expert == eid[1:2, :]
    used = jnp.where(oh0 | oh1, 1.0, 0.0)
    r = lax.broadcasted_iota(I32, (tm, tm), 0)
    c = lax.broadcasted_iota(I32, (tm, tm), 1)
    earlier = jnp.where(r < c, 1.0, 0.0).astype(BF16)
    base = starts + carry[:, 0:1] + _dot(used.astype(BF16), earlier)
    p0 = jnp.sum(jnp.where(oh0, base, 0.0), axis=0, keepdims=True)
    p1 = jnp.sum(jnp.where(oh1, base, 0.0), axis=0, keepdims=True)
    slot = lax.broadcasted_iota(I32, pos_ref.shape, 0)
    pos_ref[...] = jnp.where(slot == 0, p0, jnp.where(slot == 1, p1, 0.0)).astype(I32)
    carry[...] = carry[...] + jnp.sum(used, axis=1, keepdims=True)

    @pl.when(pl.program_id(0) == 0)
    def _():
        n_tiles = tile_ref.shape[0]
        first_row = lax.broadcasted_iota(I32, (n_tiles, LANES), 0) * EXPERT_ROWS
        elane = lax.broadcasted_iota(I32, (n_tiles, LANES), 1)
        done = jnp.where((ends[0:1, :] <= first_row) & (elane < N_EXPERTS), 1, 0)
        expert = jnp.minimum(jnp.sum(done, axis=1, keepdims=True), N_EXPERTS - 1)
        total = jnp.max(ends[0:1, :], axis=1, keepdims=True)
        live = jnp.where(first_row < total, 1, 0)
        tile_ref[...] = jnp.where(elane == 0, expert, jnp.where(elane == 1, live, 0))
        seg_ref[...] = jnp.concatenate([ends, padded], axis=0)


def _pos(eid_t, counts, n_tiles):
    t = eid_t.shape[1]
    tm = SORT_ROWS
    col = pl.BlockSpec((8, tm), lambda i: (0, i))
    return pl.pallas_call(
        _pos_kernel,
        grid=(t // tm,),
        in_specs=[col, pl.BlockSpec((8, LANES), lambda i: (0, 0))],
        out_specs=[col, pl.BlockSpec((n_tiles, LANES), lambda i: (0, 0)),
                   pl.BlockSpec((16, LANES), lambda i: (0, 0))],
        out_shape=[jax.ShapeDtypeStruct((8, t), I32),
                   jax.ShapeDtypeStruct((n_tiles, LANES), I32),
                   jax.ShapeDtypeStruct((16, LANES), I32)],
        scratch_shapes=[pltpu.VMEM((N_EXPERTS, LANES), F32)],
        compiler_params=_params(1),
        name="pos",
    )(eid_t, counts)


def _tokens(ref, first, count, tile):
    start = first * tile
    if not isinstance(start, int):
        start = pl.multiple_of(start, tile)
    return ref.at[pl.ds(start, count * tile)]


def _dispatch_kernel(seg_end_ref, seg_len_ref, pos0_ref, pos1_ref, h2_ref, xs_ref, zeros, sem, *, tile):
    i = pl.program_id(0)
    pos_refs = (pos0_ref, pos1_ref)
    tm = MOVE_ROWS

    def clear_copy(first):
        return pltpu.make_async_copy(zeros, _tokens(xs_ref, first, EXPERT_ROWS, tile), sem)

    @pl.when(i == 0)
    def _():
        zeros[...] = jnp.zeros_like(zeros)
        for e in range(N_EXPERTS):
            @pl.when(seg_len_ref[e] > 0)
            def _():
                clear_copy(pl.multiple_of(seg_end_ref[e] - EXPERT_ROWS, EXPERT_ROWS)).start()
        for e in range(N_EXPERTS):
            @pl.when(seg_len_ref[e] > 0)
            def _():
                clear_copy(0).wait()

        first_unused = seg_end_ref[N_EXPERTS - 1] // EXPERT_ROWS
        n_tiles = xs_ref.shape[0] // (EXPERT_ROWS * tile)

        def clear(t, _):
            clear_copy(pl.multiple_of(t * EXPERT_ROWS, EXPERT_ROWS)).start()
            return 0

        def clear_done(t, _):
            clear_copy(0).wait()
            return 0

        lax.fori_loop(first_unused, n_tiles, clear, 0)
        lax.fori_loop(first_unused, n_tiles, clear_done, 0)

    for r in range(tm):
        for k in range(2):
            pltpu.make_async_copy(_tokens(h2_ref, r, 1, tile), _tokens(xs_ref, pos_refs[k][r], 1, tile),
                                  sem).start()

    def drain(r, _):
        pltpu.make_async_copy(_tokens(h2_ref, 0, 1, tile), _tokens(xs_ref, 0, 1, tile), sem).wait()
        return 0

    lax.fori_loop(0, 2 * tm, drain, 0, unroll=8)


def _dispatch(seg_end, seg_len, pos_flat, h2_tiles, n_rows, tile):
    t = h2_tiles.shape[0] // tile
    tm = MOVE_ROWS
    n = t // tm
    grid_spec = pltpu.PrefetchScalarGridSpec(
        num_scalar_prefetch=2,
        grid=(n,),
        in_specs=[pl.BlockSpec((tm,), lambda i, *_: (i,), memory_space=pltpu.SMEM),
                  pl.BlockSpec((tm,), lambda i, *_: (n + i,), memory_space=pltpu.SMEM),
                  pl.BlockSpec((tm * tile, LANES), lambda i, *_: (i, 0))],
        out_specs=pl.BlockSpec(memory_space=pl.ANY),
        scratch_shapes=[pltpu.VMEM((EXPERT_ROWS * tile, LANES), F32), pltpu.SemaphoreType.DMA(())],
    )
    return pl.pallas_call(
        functools.partial(_dispatch_kernel, tile=tile),
        grid_spec=grid_spec,
        out_shape=jax.ShapeDtypeStruct((n_rows * tile, LANES), F32),
        compiler_params=pltpu.CompilerParams(dimension_semantics=("arbitrary",),
                                             vmem_limit_bytes=VMEM_LIMIT),
        name="dispatch",
    )(seg_end, seg_len, pos_flat, pos_flat, h2_tiles)


def _expert_kernel(tile_expert_ref, tile_live_ref, xs_ref, wg_ref, wu_ref, wd_ref, o_ref, wg, wu, wd):
    i = pl.program_id(0)

    @pl.when(tile_live_ref[i] > 0)
    def _():
        changed = jnp.logical_or(i == 0, tile_expert_ref[i] != tile_expert_ref[jnp.maximum(i - 1, 0)])

        @pl.when(changed)
        def _():
            wg[...] = wg_ref[0].astype(BF16)
            wu[...] = wu_ref[0].astype(BF16)
            wd[...] = wd_ref[0].astype(BF16)

        x = _load_token_tiles(xs_ref, EXPERT_ROWS).astype(BF16)
        de = wg.shape[1]
        chunk = 2 * LANES
        pre = [(_dot(x, wg[:, c:c + chunk]), _dot(x, wu[:, c:c + chunk])) for c in range(0, de, chunk)]
        hid = [((gate * jax.nn.sigmoid(gate)) * up).astype(BF16) for gate, up in pre]
        tile = o_ref.shape[0] // EXPERT_ROWS
        for oc in range(0, wd.shape[1], chunk):
            out = None
            for n, h in enumerate(hid):
                part = _dot(h, wd[n * chunk:(n + 1) * chunk, oc:oc + chunk])
                out = part if out is None else out + part
            for c in range(chunk // LANES):
                o_ref[pl.ds(oc // LANES + c, EXPERT_ROWS, stride=tile), :] = out[:, c * LANES:(c + 1) * LANES]

    @pl.when(tile_live_ref[i] == 0)
    def _():
        o_ref[...] = jnp.zeros_like(o_ref)


def _experts(tile_expert, tile_live, xs_tiles, w_gate, w_up, w_down):
    d, de = w_gate.shape[1:]
    tile = d // LANES
    n_rows = xs_tiles.shape[0] // tile
    tm = EXPERT_ROWS
    grid_spec = pltpu.PrefetchScalarGridSpec(
        num_scalar_prefetch=2,
        grid=(n_rows // tm,),
        in_specs=[pl.BlockSpec((tm * tile, LANES), lambda i, te, tl: (i, 0)),
                  pl.BlockSpec((1, d, de), lambda i, te, tl: (te[i], 0, 0)),
                  pl.BlockSpec((1, d, de), lambda i, te, tl: (te[i], 0, 0)),
                  pl.BlockSpec((1, de, d), lambda i, te, tl: (te[i], 0, 0))],
        out_specs=pl.BlockSpec((tm * tile, LANES), lambda i, te, tl: (i, 0)),
        scratch_shapes=[pltpu.VMEM((d, de), BF16), pltpu.VMEM((d, de), BF16), pltpu.VMEM((de, d), BF16)],
    )
    return pl.pallas_call(
        _expert_kernel,
        grid_spec=grid_spec,
        out_shape=jax.ShapeDtypeStruct(xs_tiles.shape, F32),
        compiler_params=_params(1),
        name="experts",
    )(tile_expert, tile_live, xs_tiles, w_gate, w_up, w_down)


def _combine_kernel(pos0_ref, pos1_ref, pos0_next_ref, pos1_next_ref, ys_ref, x1_ref, ew_ref, mod_ref, g_ref,
                    o_ref, buf, sem, *, d, final_norm):
    i = pl.program_id(0)
    n = pl.num_programs(0)
    tm = MOVE_ROWS
    tile = d // LANES

    def fetch(prefs, slot):
        for r in range(tm):
            for k in range(2):
                pltpu.make_async_copy(_tokens(ys_ref, prefs[k][r], 1, tile),
                                      _tokens(buf.at[slot, k], r, 1, tile),
                                      sem.at[slot]).start()

    @pl.when(i == 0)
    def _():
        fetch((pos0_ref, pos1_ref), 0)

    for parity in range(2):
        @pl.when((i + 1 < n) & ((i + 1) % 2 == parity))
        def _():
            fetch((pos0_next_ref, pos1_next_ref), parity)

    slot = i % 2

    def drain(r, _):
        pltpu.make_async_copy(_tokens(ys_ref, 0, 1, tile), _tokens(buf.at[slot, 0], 0, 1, tile),
                              sem.at[slot]).wait()
        return 0

    lax.fori_loop(0, 2 * tm, drain, 0, unroll=8)

    ew = ew_ref[...]
    y = (ew[:, 0:1] * _load_token_tiles(buf.at[slot, 0], tm)
         + ew[:, 1:2] * _load_token_tiles(buf.at[slot, 1], tm))
    g2 = mod_ref[0, :, 5 * d:6 * d]
    x2 = x1_ref[...] + g2 * y
    o_ref[...] = _rms(x2, g_ref[...]) if final_norm else x2


def _combine(pos_flat, ys, x1, ew, mod3, final_g, seq, final_norm):
    t, d = x1.shape
    tm = MOVE_ROWS
    n = t // tm
    per_batch = seq // tm
    return pl.pallas_call(
        functools.partial(_combine_kernel, d=d, final_norm=final_norm),
        grid=(n,),
        in_specs=[pl.BlockSpec((tm,), lambda i: (i,), memory_space=pltpu.SMEM),
                  pl.BlockSpec((tm,), lambda i: (n + i,), memory_space=pltpu.SMEM),
                  pl.BlockSpec((tm,), lambda i: (jnp.minimum(i + 1, n - 1),), memory_space=pltpu.SMEM),
                  pl.BlockSpec((tm,), lambda i: (n + jnp.minimum(i + 1, n - 1),), memory_space=pltpu.SMEM),
                  pl.BlockSpec(memory_space=pl.ANY),
                  pl.BlockSpec((tm, d), lambda i: (i, 0)),
                  pl.BlockSpec((tm, LANES), lambda i: (i, 0)),
                  pl.BlockSpec((1, 1, mod3.shape[2]), lambda i: (i // per_batch, 0, 0)),
                  pl.BlockSpec((1, d), lambda i: (0, 0))],
        out_specs=pl.BlockSpec((tm, d), lambda i: (i, 0)),
        out_shape=jax.ShapeDtypeStruct((t, d), F32),
        scratch_shapes=[pltpu.VMEM((2, 2, tm * d // LANES, LANES), F32), pltpu.SemaphoreType.DMA((2,))],
        compiler_params=_params(1),
        name="combine",
    )(pos_flat, pos_flat, pos_flat, pos_flat, ys, x1, ew, mod3, final_g)


def _rope_tables(seq):
    inv = 1.0 / (ROPE_THETA ** (jnp.arange(0, HEAD_DIM, 2, dtype=F32) / HEAD_DIM))
    ang = jnp.arange(seq, dtype=F32)[:, None] * inv[None, :]
    cos, sin = jnp.cos(ang), jnp.sin(ang)
    cos_head = jnp.concatenate([cos, cos], axis=1)
    sin_head = jnp.concatenate([-sin, sin], axis=1)
    reps = LANES // HEAD_DIM
    return jnp.tile(cos_head, (1, reps)), jnp.tile(sin_head, (1, reps))


def kernel(x, c, w_ada, b_ada, norm1_g, w_in, lambda_q1, lambda_k1, lambda_q2, lambda_k2,
           diff_subln_g, w_proj_moba, w_proj_diff, w_out, norm2_g, w_group, b_group,
           w_expert, b_expert, w_gate, w_up, w_down, final_g):
    batch, seq, d = x.shape
    depth = w_ada.shape[0]
    t = batch * seq
    assert seq % PROJ_ROWS == 0 and seq % MOBA_BLOCK == 0 and seq // MOBA_BLOCK <= LANES
    assert ATTN_ROWS == MOBA_BLOCK and d % SEG == 0 and t % SORT_ROWS == 0 and batch <= 8
    assert seq % ATTN_Q_ROWS == 0 and ATTN_Q_ROWS % ATTN_ROWS == 0
    assert EXPERT_ROWS & (EXPERT_ROWS - 1) == 0
    n_rows = 2 * t + N_EXPERTS * EXPERT_ROWS
    n_tiles = n_rows // EXPERT_ROWS
    cos, sin = _rope_tables(seq)
    c_pad = jnp.zeros((8, d), F32).at[:batch].set(c)
    xf = x.reshape(t, d)
    row = lambda v: v.reshape(1, -1)
    for l in range(depth):
        mod = _ada(c_pad, w_ada[l], row(b_ada[l]))
        mod3 = mod[:batch].reshape(batch, 1, 6 * d)
        w_qkv = w_in[l][:, :N_QKV_SEGS * SEG].astype(BF16)
        w_gates = w_in[l][:, N_QKV_SEGS * SEG:].astype(BF16)
        qm, km, vm, qd, kd, vd, kmean = _proj(xf, mod3, row(norm1_g[l]), w_qkv, cos, sin, seq)
        kmean = kmean.reshape(batch, seq // MOBA_BLOCK, SEG)
        om = _moba(qm, km, vm, kmean, batch, seq)
        lam_init = 0.8 - 0.6 * math.exp(-0.3 * l)
        od = _diff(qd, kd, vd, row(lambda_q1[l]), row(lambda_k1[l]), row(lambda_q2[l]),
                   row(lambda_k2[l]), row(diff_subln_g[l]), batch, seq, lam_init)
        w_router = jnp.zeros((d, LANES), F32)
        w_router = w_router.at[:, :N_GROUPS].set(w_group[l])
        w_router = w_router.at[:, N_GROUPS:N_GROUPS + N_EXPERTS].set(w_expert[l])
        b_router = jnp.zeros((1, LANES), F32)
        b_router = b_router.at[0, :N_GROUPS].set(b_group[l])
        b_router = b_router.at[0, N_GROUPS:N_GROUPS + N_EXPERTS].set(b_expert[l])
        x1, h2, eid_t, ew, counts = _merge(xf, om, od, mod3, row(norm1_g[l]), row(norm2_g[l]), w_gates,
                                 w_proj_moba[l].astype(BF16), w_proj_diff[l].astype(BF16),
                                 w_out[l].astype(BF16), w_router, b_router, seq)
        pos_t, tiles, segs = _pos(eid_t, counts, n_tiles)
        pos_flat = pos_t[:2].reshape(2 * t)
        xs = _dispatch(segs[0, :N_EXPERTS], segs[8, :N_EXPERTS], pos_flat, h2, n_rows, d // LANES)
        ys = _experts(tiles[:, 0], tiles[:, 1], xs, w_gate[l], w_up[l], w_down[l])
        xf = _combine(pos_flat, ys, x1, ew, mod3, row(final_g), seq, final_norm=(l == depth - 1))
    return xf.reshape(batch, seq, d)
```

```python
import functools
import math

import jax
import jax.numpy as jnp
from jax import lax
from jax.experimental import pallas as pl
from jax.experimental.pallas import tpu as pltpu

F32 = jnp.float32
BF16 = jnp.bfloat16
I32 = jnp.int32

LANES = 128
HEAD_DIM = 64
HEADS_PER_VREG = LANES // HEAD_DIM
MOBA_HEADS = 8
MOBA_BLOCK = 256
MOBA_TOPK = 3
DIFF_HEADS = 4
ROPE_THETA = 10000.0
N_GROUPS = 4
EXPERTS_PER_GROUP = 8
N_EXPERTS = N_GROUPS * EXPERTS_PER_GROUP
NORM_EPS = 1e-6
NEG_INF = -1e30
SEG = MOBA_HEADS * HEAD_DIM
N_QKV_SEGS = 6
SUM_ROWS = 16

PROJ_ROWS = 512
ATTN_ROWS = 256
ATTN_Q_ROWS = 1024
MERGE_ROWS = 512
SORT_ROWS = 512
EXPERT_ROWS = 512
MOVE_ROWS = 256
VMEM_LIMIT = 56 * 1024 * 1024


def _params(n_axes, vmem=VMEM_LIMIT):
    return pltpu.CompilerParams(dimension_semantics=("arbitrary",) * n_axes,
                                vmem_limit_bytes=vmem)


def _dot(a, b):
    return jnp.dot(a, b, preferred_element_type=F32)


def _dot_nt(a, b):
    return lax.dot_general(a, b, (((1,), (1,)), ((), ())), preferred_element_type=F32)


def _store_token_tiles(ref, x):
    n, d = x.shape
    chunks = d // LANES
    for c in range(chunks):
        ref[pl.ds(c, n, stride=chunks), :] = x[:, c * LANES:(c + 1) * LANES]


def _load_token_tiles(ref, n):
    chunks = ref.shape[0] // n
    return jnp.concatenate([ref[pl.ds(c, n, stride=chunks), :] for c in range(chunks)], axis=1)


def _rms(x, g):
    return x * lax.rsqrt(jnp.mean(x * x, axis=-1, keepdims=True) + NORM_EPS) * g


def _ada_kernel(c_ref, w_ref, b_ref, o_ref):
    c = c_ref[...]
    o_ref[...] = _dot(c * jax.nn.sigmoid(c), w_ref[...]) + b_ref[...]


def _ada(c_pad, w, b):
    rows, d = c_pad.shape
    n = w.shape[1]
    tn = 1536
    return pl.pallas_call(
        _ada_kernel,
        grid=(n // tn,),
        in_specs=[pl.BlockSpec((rows, d), lambda j: (0, 0)),
                  pl.BlockSpec((d, tn), lambda j: (0, j)),
                  pl.BlockSpec((1, tn), lambda j: (0, j))],
        out_specs=pl.BlockSpec((rows, tn), lambda j: (0, j)),
        out_shape=jax.ShapeDtypeStruct((rows, n), F32),
        compiler_params=_params(1),
        name="ada",
    )(c_pad, w, b)


def _rope(x, cos, sin_signed):
    half = HEAD_DIM // 2
    width = x.shape[1]
    lane = lax.broadcasted_iota(I32, x.shape, 1)
    first = (lane & (HEAD_DIM - 1)) < half
    partner = jnp.where(first, pltpu.roll(x, width - half, 1), pltpu.roll(x, half, 1))
    return x * cos + partner * sin_signed


def _proj_kernel(x_ref, mod_ref, g_ref, w_ref, cos_ref, sin_ref,
                 qm_ref, km_ref, vm_ref, qd_ref, kd_ref, vd_ref, kmean_ref, *, d):
    x = x_ref[...]
    sh = mod_ref[0, :, 0:d]
    sc = mod_ref[0, :, d:2 * d]
    h = (_rms(x, g_ref[...]) * (1.0 + sc) + sh).astype(BF16)
    cos = jnp.concatenate([cos_ref[...]] * (SEG // LANES), axis=1)
    sin = jnp.concatenate([sin_ref[...]] * (SEG // LANES), axis=1)
    scale = HEAD_DIM ** -0.5 * math.log2(math.e)
    outs = (qm_ref, km_ref, vm_ref, qd_ref, kd_ref, vd_ref)
    n_blk = x.shape[0] // ATTN_ROWS
    for seg, o_ref in enumerate(outs):
        y = _dot(h, w_ref[:, seg * SEG:(seg + 1) * SEG])
        if seg in (0, 1, 3, 4):
            y = _rope(y, cos, sin)
        if seg in (0, 3):
            y = y * scale
        if seg == 1:
            for blk in range(n_blk):
                rows = y[blk * MOBA_BLOCK:(blk + 1) * MOBA_BLOCK]
                kmean_ref[0, blk:blk + 1, :] = jnp.mean(rows, axis=0, keepdims=True)
        if seg in (2, 5):
            for blk in range(n_blk):
                for part in range(SEG // LANES):
                    piece = y[blk * ATTN_ROWS:(blk + 1) * ATTN_ROWS, part * LANES:(part + 1) * LANES]
                    o_ref[blk, part * LANES:(part + 1) * LANES, :] = piece.T.astype(BF16)
        else:
            o_ref[...] = y.astype(BF16)


def _proj(x2, mod3, g, w_qkv, cos, sin, seq):
    t, d = x2.shape
    tm = PROJ_ROWS
    per_batch = seq // tm
    row_spec = pl.BlockSpec((tm, SEG), lambda i: (i, 0))
    tab_spec = pl.BlockSpec((tm, LANES), lambda i: (i % per_batch, 0))
    act = jax.ShapeDtypeStruct((t, SEG), BF16)
    act_t = jax.ShapeDtypeStruct((t // ATTN_ROWS, SEG, ATTN_ROWS), BF16)
    t_spec = pl.BlockSpec((tm // ATTN_ROWS, SEG, ATTN_ROWS), lambda i: (i, 0, 0))
    return pl.pallas_call(
        functools.partial(_proj_kernel, d=d),
        grid=(t // tm,),
        in_specs=[pl.BlockSpec((tm, d), lambda i: (i, 0)),
                  pl.BlockSpec((1, 1, mod3.shape[2]), lambda i: (i // per_batch, 0, 0)),
                  pl.BlockSpec((1, d), lambda i: (0, 0)),
                  pl.BlockSpec(w_qkv.shape, lambda i: (0, 0)),
                  tab_spec, tab_spec],
        out_specs=[row_spec, row_spec, t_spec, row_spec, row_spec, t_spec,
                   pl.BlockSpec((1, tm // MOBA_BLOCK, SEG), lambda i: (i, 0, 0))],
        out_shape=[act, act, act_t, act, act, act_t,
                   jax.ShapeDtypeStruct((t // tm, tm // MOBA_BLOCK, SEG), F32)],
        compiler_params=_params(1),
        name="proj",
    )(x2, mod3, g, w_qkv, cos, sin)


def _attend(qs, keys_of_tile, k_ref, vt_ref, v_rows, q_tile, scratch):
    tq, tk = ATTN_Q_ROWS, ATTN_ROWS
    sub = tq // tk
    ones = jnp.ones((SUM_ROWS, tk), BF16)
    n_soft = len(qs)
    acc_refs, max_refs = scratch[:n_soft], scratch[n_soft:]
    for acc_ref in acc_refs:
        acc_ref[...] = jnp.zeros(acc_ref.shape, F32)

    def scores(j, first_query=0):
        start = pl.multiple_of(j * tk, tk)
        keys = keys_of_tile(j, k_ref[pl.ds(start, tk), :])
        return tuple(_dot(keys, q[:, first_query:]) for q in qs), vt_ref[j]

    def update(tile, mask, maxes, first_query=0):
        s_all, vt = tile
        new = ()
        for n, s in enumerate(s_all):
            if mask is not None:
                own = jnp.where(mask, s[:, :tk], NEG_INF)
                s = own if s.shape[1] == tk else jnp.concatenate([own, s[:, tk:]], axis=1)
            m = max_refs[n][:, first_query:] if maxes is None else maxes[n]
            m_new = jnp.maximum(m, jnp.max(s, axis=0, keepdims=True))
            alpha = jnp.exp2(m - m_new)
            p = jnp.exp2(s - m_new).astype(BF16)
            if maxes is None:
                max_refs[n][:, first_query:] = m_new
            new += (m_new,)
            vt_sum = jnp.concatenate([vt[v_rows[n]], ones], axis=0)
            acc_refs[n][:, first_query:] = alpha * acc_refs[n][:, first_query:] + _dot(vt_sum, p)
        return new

    first_own = q_tile * sub

    def body(group, maxes):
        for tile in [scores(sub * group + b) for b in range(sub)]:
            maxes = update(tile, None, maxes)
        return maxes

    maxes = lax.fori_loop(0, q_tile, body, (jnp.full((1, tq), NEG_INF, F32),) * n_soft)
    for max_ref, m in zip(max_refs, maxes):
        max_ref[...] = m
    own = [scores(first_own + b, b * tk) for b in range(sub)]
    causal = lax.broadcasted_iota(I32, (tk, tk), 0) <= lax.broadcasted_iota(I32, (tk, tk), 1)
    for b in range(sub):
        update(own[b], causal, None, b * tk)
    out = []
    for n, acc_ref in enumerate(acc_refs):
        rows = v_rows[n].stop - v_rows[n].start
        out.append((acc_ref[rows:rows + 1, :], acc_ref[0:rows, :]))
    return out


def _softmax_scratch(n_softmax, features):
    return ([pltpu.VMEM((features + SUM_ROWS, ATTN_Q_ROWS), F32)] * n_softmax
            + [pltpu.VMEM((1, ATTN_Q_ROWS), F32)] * n_softmax)


def _moba_kernel(q_ref, k_ref, v_ref, kmean_ref, o_ref, *scratch, n_blocks):
    tq = ATTN_Q_ROWS
    q_tile = pl.program_id(2)
    q_t = q_ref[...].astype(F32).T
    feat = lax.broadcasted_iota(I32, (LANES, tq), 0)
    blk = lax.broadcasted_iota(I32, (n_blocks, tq), 0)
    qry = lax.broadcasted_iota(I32, (n_blocks, tq), 1)
    own_block = q_tile * (tq // MOBA_BLOCK) + (qry >> int(math.log2(MOBA_BLOCK)))
    kmean = kmean_ref[0]
    km_head = lax.broadcasted_iota(I32, kmean.shape, 1) >> 6
    pad = jnp.zeros((LANES - n_blocks, tq), BF16)
    lane_k = lax.broadcasted_iota(I32, (ATTN_ROWS, LANES), 1)
    qs = []
    for head in range(HEADS_PER_VREG):
        qh = jnp.where((feat >> 6) == head, q_t, 0.0).astype(BF16)
        km = jnp.where(km_head == head, kmean, 0.0)
        km_hi = km.astype(BF16)
        km_lo = (km - km_hi.astype(F32)).astype(BF16)
        gate = _dot(km_hi, qh) + _dot(km_lo, qh)
        gate = jnp.where(blk < own_block, gate, NEG_INF)
        bias = jnp.where(blk == own_block, 0.0, NEG_INF)
        for _ in range(min(MOBA_TOPK, n_blocks)):
            top = jnp.max(gate, axis=0, keepdims=True)
            first = jnp.min(jnp.where(gate == top, blk, n_blocks), axis=0, keepdims=True)
            picked = blk == first
            bias = jnp.where(picked & (blk < own_block), 0.0, bias)
            gate = jnp.where(picked, -jnp.inf, gate)
        qs.append(jnp.concatenate([qh, bias.astype(BF16), pad], axis=0))

    def rhs_of_tile(j, keys):
        return jnp.concatenate([keys, jnp.where(lane_k == j, 1.0, 0.0).astype(BF16)], axis=1)

    v_rows = [slice(h * HEAD_DIM, (h + 1) * HEAD_DIM) for h in range(HEADS_PER_VREG)]
    stats = _attend(qs, rhs_of_tile, k_ref, v_ref, v_rows, q_tile, scratch)
    out_t = jnp.concatenate([acc / l for l, acc in stats], axis=0)
    o_ref[...] = out_t.T.astype(BF16)


def _moba(qm, km, vm_t, kmean, batch, seq):
    t = qm.shape[0]
    tq = ATTN_Q_ROWS
    nq = seq // tq
    n_blocks = seq // MOBA_BLOCK
    pairs = MOBA_HEADS // HEADS_PER_VREG
    n_kv = seq // ATTN_ROWS
    return pl.pallas_call(
        functools.partial(_moba_kernel, n_blocks=n_blocks),
        grid=(batch, pairs, nq),
        in_specs=[pl.BlockSpec((tq, LANES), lambda b, p, i: (b * nq + i, p)),
                  pl.BlockSpec((seq, LANES), lambda b, p, i: (b, p)),
                  pl.BlockSpec((n_kv, LANES, ATTN_ROWS), lambda b, p, i: (b, p, 0)),
                  pl.BlockSpec((1, n_blocks, LANES), lambda b, p, i: (b, 0, p))],
        out_specs=pl.BlockSpec((tq, LANES), lambda b, p, i: (b * nq + i, p)),
        out_shape=jax.ShapeDtypeStruct((t, SEG), BF16),
        scratch_shapes=_softmax_scratch(HEADS_PER_VREG, HEAD_DIM),
        compiler_params=_params(3),
        name="moba",
    )(qm, km, vm_t, kmean)


def _diff_kernel(q_ref, k_ref, v_ref, lq1_ref, lk1_ref, lq2_ref, lk2_ref, g_ref, o_ref,
                 *scratch, lam_init):
    tq = ATTN_Q_ROWS
    q_t = q_ref[...].astype(F32).T
    comp = lax.broadcasted_iota(I32, (LANES, tq), 0) >> 6
    qs = [jnp.where(comp == c, q_t, 0.0).astype(BF16) for c in range(2)]
    v_rows = [slice(0, LANES)] * 2
    (l1, a1), (l2, a2) = _attend(qs, lambda j, keys: keys, k_ref, v_ref, v_rows,
                                 pl.program_id(2), scratch)
    lam = (jnp.exp(jnp.sum(lq1_ref[...] * lk1_ref[...], axis=1, keepdims=True))
           - jnp.exp(jnp.sum(lq2_ref[...] * lk2_ref[...], axis=1, keepdims=True)) + lam_init)
    o = (a1 / l1 - lam * (a2 / l2)).T
    o_ref[...] = (_rms(o, g_ref[...]) * (1.0 - lam_init)).astype(BF16)


def _diff(qd, kd, vd_t, lq1, lk1, lq2, lk2, g, batch, seq, lam_init):
    t = qd.shape[0]
    tq = ATTN_Q_ROWS
    nq = seq // tq
    n_kv = seq // ATTN_ROWS
    vec = lambda a: pl.BlockSpec(a.shape, lambda b, h, i: (0, 0))
    return pl.pallas_call(
        functools.partial(_diff_kernel, lam_init=lam_init),
        grid=(batch, DIFF_HEADS, nq),
        in_specs=[pl.BlockSpec((tq, LANES), lambda b, h, i: (b * nq + i, h)),
                  pl.BlockSpec((seq, LANES), lambda b, h, i: (b, h)),
                  pl.BlockSpec((n_kv, LANES, ATTN_ROWS), lambda b, h, i: (b, h, 0)),
                  vec(lq1), vec(lk1), vec(lq2), vec(lk2), vec(g)],
        out_specs=pl.BlockSpec((tq, LANES), lambda b, h, i: (b * nq + i, h)),
        out_shape=jax.ShapeDtypeStruct((t, SEG), BF16),
        scratch_shapes=_softmax_scratch(2, LANES),
        compiler_params=_params(3),
        name="diff",
    )(qd, kd, vd_t, lq1, lk1, lq2, lk2, g)


def _merge_kernel(x_ref, om_ref, od_ref, mod_ref, g1n_ref, g2n_ref, wg_ref, wpm_ref, wpd_ref,
                  wout_ref, wr_ref, br_ref, x1_ref, h2_ref, eid_ref, ew_ref, counts_ref, merged, counts,
                  *, d):
    x = x_ref[...]
    sh1 = mod_ref[0, :, 0:d]
    sc1 = mod_ref[0, :, d:2 * d]
    g1 = mod_ref[0, :, 2 * d:3 * d]
    sh2 = mod_ref[0, :, 3 * d:4 * d]
    sc2 = mod_ref[0, :, 4 * d:5 * d]
    h = (_rms(x, g1n_ref[...]) * (1.0 + sc1) + sh1).astype(BF16)
    om = om_ref[...]
    od = od_ref[...]
    chunk = SEG
    for c in range(d // chunk):
        cols = slice(c * chunk, (c + 1) * chunk)
        gm = _dot(h, wg_ref[:, c * chunk:(c + 1) * chunk])
        gd = _dot(h, wg_ref[:, d + c * chunk:d + (c + 1) * chunk])
        ym = _dot(om, wpm_ref[:, cols])
        yd = _dot(od, wpd_ref[:, cols])
        merged[:, cols] = (jax.nn.sigmoid(gm) * ym + jax.nn.sigmoid(gd) * yd).astype(BF16)
    x1 = x + g1 * _dot(merged[...], wout_ref[...])
    x1_ref[...] = x1
    h2 = _rms(x1, g2n_ref[...]) * (1.0 + sc2) + sh2
    _store_token_tiles(h2_ref, h2)

    h2_hi = h2.astype(BF16)
    h2_lo = (h2 - h2_hi.astype(F32)).astype(BF16)
    wr = wr_ref[...]
    wr_hi = wr.astype(BF16)
    wr_lo = (wr - wr_hi.astype(F32)).astype(BF16)
    logits = _dot(h2_hi, wr_hi) + _dot(h2_lo, wr_hi) + _dot(h2_hi, wr_lo) + br_ref[...]
    lane = lax.broadcasted_iota(I32, logits.shape, 1)
    is_group = lane < N_GROUPS
    gl = jnp.where(is_group, logits, -jnp.inf)
    gmax = jnp.max(gl, axis=1, keepdims=True)
    gexp = jnp.exp(gl - gmax)
    g_w = 1.0 / jnp.sum(gexp, axis=1, keepdims=True)
    g_idx = jnp.min(jnp.where(gl == gmax, lane, LANES), axis=1, keepdims=True)
    e_lane = lane - N_GROUPS
    in_group = (e_lane >= g_idx * EXPERTS_PER_GROUP) & (e_lane < (g_idx + 1) * EXPERTS_PER_GROUP)
    el = jnp.where(in_group, logits, -jnp.inf)
    emax = jnp.max(el, axis=1, keepdims=True)
    eexp = jnp.exp(el - emax)
    prob = eexp / jnp.sum(eexp, axis=1, keepdims=True)
    prob = jnp.where(in_group, prob, -1.0)
    p1 = jnp.max(prob, axis=1, keepdims=True)
    i1 = jnp.min(jnp.where(prob == p1, lane, LANES), axis=1, keepdims=True)
    prob2 = jnp.where(lane == i1, -1.0, prob)
    p2 = jnp.max(prob2, axis=1, keepdims=True)
    i2 = jnp.min(jnp.where(prob2 == p2, lane, LANES), axis=1, keepdims=True)
    tot = p1 + p2
    eid = jnp.where(lane == 0, i1 - N_GROUPS, jnp.where(lane == 1, i2 - N_GROUPS, 0))
    eid_ref[...] = eid.T[0:8, :]

    @pl.when(pl.program_id(0) == 0)
    def _():
        counts[...] = jnp.zeros_like(counts)

    chosen = jnp.where((lane == i1) | (lane == i2), 1.0, 0.0)
    per_lane = jnp.broadcast_to(jnp.sum(chosen, axis=0, keepdims=True), counts.shape)
    counts[...] = counts[...] + pltpu.roll(per_lane, LANES - N_GROUPS, 1)
    counts_ref[...] = counts[...].astype(I32)
    ew_ref[...] = jnp.where(lane == 0, g_w * (p1 / tot), jnp.where(lane == 1, g_w * (p2 / tot), 0.0))


def _merge(x2, om, od, mod3, g1n, g2n, wg, wpm, wpd, wout, wr, br, seq):
    t, d = x2.shape
    tm = MERGE_ROWS
    per_batch = seq // tm
    full = lambda a: pl.BlockSpec(a.shape, lambda i: (0,) * a.ndim)
    row = lambda w: pl.BlockSpec((tm, w), lambda i: (i, 0))
    return pl.pallas_call(
        functools.partial(_merge_kernel, d=d),
        grid=(t // tm,),
        in_specs=[row(d), row(SEG), row(SEG),
                  pl.BlockSpec((1, 1, mod3.shape[2]), lambda i: (i // per_batch, 0, 0)),
                  full(g1n), full(g2n), full(wg), full(wpm), full(wpd), full(wout), full(wr), full(br)],
        out_specs=[row(d), pl.BlockSpec((tm * d // LANES, LANES), lambda i: (i, 0)),
                   pl.BlockSpec((8, tm), lambda i: (0, i)), row(LANES),
                   pl.BlockSpec((8, LANES), lambda i: (0, 0))],
        out_shape=[jax.ShapeDtypeStruct((t, d), F32), jax.ShapeDtypeStruct((t * d // LANES, LANES), F32),
                   jax.ShapeDtypeStruct((8, t), I32), jax.ShapeDtypeStruct((t, LANES), F32),
                   jax.ShapeDtypeStruct((8, LANES), I32)],
        scratch_shapes=[pltpu.VMEM((tm, d), BF16), pltpu.VMEM((8, LANES), F32)],
        compiler_params=_params(1),
        name="merge",
    )(x2, om, od, mod3, g1n, g2n, wg, wpm, wpd, wout, wr, br)


def _segment_ends(counts):
    lane = lax.broadcasted_iota(I32, counts.shape, 1)
    padded = (counts + (EXPERT_ROWS - 1)) & (-EXPERT_ROWS)
    padded = jnp.where(lane < N_EXPERTS, padded, 0)
    ends = padded
    shift = 1
    while shift < N_EXPERTS:
        ends = ends + jnp.where(lane >= shift, pltpu.roll(ends, shift, 1), 0)
        shift *= 2
    return padded, ends


def _pos_kernel(eid_ref, counts_ref, pos_ref, tile_ref, seg_ref, carry):
    tm = eid_ref.shape[1]

    @pl.when(pl.program_id(0) == 0)
    def _():
        carry[...] = jnp.zeros_like(carry)

    padded, ends = _segment_ends(counts_ref[...])
    starts = (ends - padded).astype(F32).T[0:N_EXPERTS, 0:1]
    eid = eid_ref[...]
    expert = lax.broadcasted_iota(I32, (N_EXPERTS, tm), 0)
    oh0 = expert == eid[0:1, :]
    oh1 = expert == eid[1:2, :]
    used = jnp.where(oh0 | oh1, 1.0, 0.0)
    r = lax.broadcasted_iota(I32, (tm, tm), 0)
    c = lax.broadcasted_iota(I32, (tm, tm), 1)
    earlier = jnp.where(r < c, 1.0, 0.0).astype(BF16)
    base = starts + carry[:, 0:1] + _dot(used.astype(BF16), earlier)
    p0 = jnp.sum(jnp.where(oh0, base, 0.0), axis=0, keepdims=True)
    p1 = jnp.sum(jnp.where(oh1, base, 0.0), axis=0, keepdims=True)
    slot = lax.broadcasted_iota(I32, pos_ref.shape, 0)
    pos_ref[...] = jnp.where(slot == 0, p0, jnp.where(slot == 1, p1, 0.0)).astype(I32)
    carry[...] = carry[...] + jnp.sum(used, axis=1, keepdims=True)

    @pl.when(pl.program_id(0) == 0)
    def _():
        n_tiles = tile_ref.shape[0]
        first_row = lax.broadcasted_iota(I32, (n_tiles, LANES), 0) * EXPERT_ROWS
        elane = lax.broadcasted_iota(I32, (n_tiles, LANES), 1)
        done = jnp.where((ends[0:1, :] <= first_row) & (elane < N_EXPERTS), 1, 0)
        expert = jnp.minimum(jnp.sum(done, axis=1, keepdims=True), N_EXPERTS - 1)
        total = jnp.max(ends[0:1, :], axis=1, keepdims=True)
        live = jnp.where(first_row < total, 1, 0)
        tile_ref[...] = jnp.where(elane == 0, expert, jnp.where(elane == 1, live, 0))
        seg_ref[...] = jnp.concatenate([ends, padded], axis=0)


def _pos(eid_t, counts, n_tiles):
    t = eid_t.shape[1]
    tm = SORT_ROWS
    col = pl.BlockSpec((8, tm), lambda i: (0, i))
    return pl.pallas_call(
        _pos_kernel,
        grid=(t // tm,),
        in_specs=[col, pl.BlockSpec((8, LANES), lambda i: (0, 0))],
        out_specs=[col, pl.BlockSpec((n_tiles, LANES), lambda i: (0, 0)),
                   pl.BlockSpec((16, LANES), lambda i: (0, 0))],
        out_shape=[jax.ShapeDtypeStruct((8, t), I32),
                   jax.ShapeDtypeStruct((n_tiles, LANES), I32),
                   jax.ShapeDtypeStruct((16, LANES), I32)],
        scratch_shapes=[pltpu.VMEM((N_EXPERTS, LANES), F32)],
        compiler_params=_params(1),
        name="pos",
    )(eid_t, counts)


def _tokens(ref, first, count, tile):
    start = first * tile
    if not isinstance(start, int):
        start = pl.multiple_of(start, tile)
    return ref.at[pl.ds(start, count * tile)]


def _dispatch_kernel(seg_end_ref, seg_len_ref, pos0_ref, pos1_ref, h2_ref, xs_ref, zeros, sem, *, tile):
    i = pl.program_id(0)
    pos_refs = (pos0_ref, pos1_ref)
    tm = MOVE_ROWS

    def clear_copy(first):
        return pltpu.make_async_copy(zeros, _tokens(xs_ref, first, EXPERT_ROWS, tile), sem)

    @pl.when(i == 0)
    def _():
        zeros[...] = jnp.zeros_like(zeros)
        for e in range(N_EXPERTS):
            @pl.when(seg_len_ref[e] > 0)
            def _():
                clear_copy(pl.multiple_of(seg_end_ref[e] - EXPERT_ROWS, EXPERT_ROWS)).start()
        for e in range(N_EXPERTS):
            @pl.when(seg_len_ref[e] > 0)
            def _():
                clear_copy(0).wait()

        first_unused = seg_end_ref[N_EXPERTS - 1] // EXPERT_ROWS
        n_tiles = xs_ref.shape[0] // (EXPERT_ROWS * tile)

        def clear(t, _):
            clear_copy(pl.multiple_of(t * EXPERT_ROWS, EXPERT_ROWS)).start()
            return 0

        def clear_done(t, _):
            clear_copy(0).wait()
            return 0

        lax.fori_loop(first_unused, n_tiles, clear, 0)
        lax.fori_loop(first_unused, n_tiles, clear_done, 0)

    for r in range(tm):
        for k in range(2):
            pltpu.make_async_copy(_tokens(h2_ref, r, 1, tile), _tokens(xs_ref, pos_refs[k][r], 1, tile),
                                  sem).start(priority=k)

    def drain(r, _):
        pltpu.make_async_copy(_tokens(h2_ref, 0, 1, tile), _tokens(xs_ref, 0, 1, tile), sem).wait()
        return 0

    lax.fori_loop(0, 2 * tm, drain, 0, unroll=8)


def _dispatch(seg_end, seg_len, pos_flat, h2_tiles, n_rows, tile):
    t = h2_tiles.shape[0] // tile
    tm = MOVE_ROWS
    n = t // tm
    grid_spec = pltpu.PrefetchScalarGridSpec(
        num_scalar_prefetch=2,
        grid=(n,),
        in_specs=[pl.BlockSpec((tm,), lambda i, *_: (i,), memory_space=pltpu.SMEM),
                  pl.BlockSpec((tm,), lambda i, *_: (n + i,), memory_space=pltpu.SMEM),
                  pl.BlockSpec((tm * tile, LANES), lambda i, *_: (i, 0))],
        out_specs=pl.BlockSpec(memory_space=pl.ANY),
        scratch_shapes=[pltpu.VMEM((EXPERT_ROWS * tile, LANES), F32), pltpu.SemaphoreType.DMA(())],
    )
    return pl.pallas_call(
        functools.partial(_dispatch_kernel, tile=tile),
        grid_spec=grid_spec,
        out_shape=jax.ShapeDtypeStruct((n_rows * tile, LANES), F32),
        compiler_params=pltpu.CompilerParams(dimension_semantics=("arbitrary",),
                                             vmem_limit_bytes=VMEM_LIMIT),
        name="dispatch",
    )(seg_end, seg_len, pos_flat, pos_flat, h2_tiles)


def _expert_kernel(tile_expert_ref, tile_live_ref, xs_ref, wg_ref, wu_ref, wd_ref, o_ref, wg, wu, wd):
    i = pl.program_id(0)

    @pl.when(tile_live_ref[i] > 0)
    def _():
        changed = jnp.logical_or(i == 0, tile_expert_ref[i] != tile_expert_ref[jnp.maximum(i - 1, 0)])

        @pl.when(changed)
        def _():
            wg[...] = wg_ref[0].astype(BF16)
            wu[...] = wu_ref[0].astype(BF16)
            wd[...] = wd_ref[0].astype(BF16)

        x = _load_token_tiles(xs_ref, EXPERT_ROWS).astype(BF16)
        de = wg.shape[1]
        chunk = 2 * LANES
        pre = [(_dot(x, wg[:, c:c + chunk]), _dot(x, wu[:, c:c + chunk])) for c in range(0, de, chunk)]
        hid = [((gate * jax.nn.sigmoid(gate)) * up).astype(BF16) for gate, up in pre]
        tile = o_ref.shape[0] // EXPERT_ROWS
        for oc in range(0, wd.shape[1], chunk):
            out = None
            for n, h in enumerate(hid):
                part = _dot(h, wd[n * chunk:(n + 1) * chunk, oc:oc + chunk])
                out = part if out is None else out + part
            for c in range(chunk // LANES):
                o_ref[pl.ds(oc // LANES + c, EXPERT_ROWS, stride=tile), :] = out[:, c * LANES:(c + 1) * LANES]

    @pl.when(tile_live_ref[i] == 0)
    def _():
        o_ref[...] = jnp.zeros_like(o_ref)


def _experts(tile_expert, tile_live, xs_tiles, w_gate, w_up, w_down):
    d, de = w_gate.shape[1:]
    tile = d // LANES
    n_rows = xs_tiles.shape[0] // tile
    tm = EXPERT_ROWS
    grid_spec = pltpu.PrefetchScalarGridSpec(
        num_scalar_prefetch=2,
        grid=(n_rows // tm,),
        in_specs=[pl.BlockSpec((tm * tile, LANES), lambda i, te, tl: (i, 0)),
                  pl.BlockSpec((1, d, de), lambda i, te, tl: (te[i], 0, 0)),
                  pl.BlockSpec((1, d, de), lambda i, te, tl: (te[i], 0, 0)),
                  pl.BlockSpec((1, de, d), lambda i, te, tl: (te[i], 0, 0))],
        out_specs=pl.BlockSpec((tm * tile, LANES), lambda i, te, tl: (i, 0)),
        scratch_shapes=[pltpu.VMEM((d, de), BF16), pltpu.VMEM((d, de), BF16), pltpu.VMEM((de, d), BF16)],
    )
    return pl.pallas_call(
        _expert_kernel,
        grid_spec=grid_spec,
        out_shape=jax.ShapeDtypeStruct(xs_tiles.shape, F32),
        compiler_params=_params(1),
        name="experts",
    )(tile_expert, tile_live, xs_tiles, w_gate, w_up, w_down)


def _combine_kernel(pos0_ref, pos1_ref, pos0_next_ref, pos1_next_ref, ys_ref, x1_ref, ew_ref, mod_ref, g_ref,
                    o_ref, buf, sem, *, d, final_norm):
    i = pl.program_id(0)
    n = pl.num_programs(0)
    tm = MOVE_ROWS
    tile = d // LANES

    def fetch(prefs, slot):
        for r in range(tm):
            for k in range(2):
                pltpu.make_async_copy(_tokens(ys_ref, prefs[k][r], 1, tile),
                                      _tokens(buf.at[slot, k], r, 1, tile),
                                      sem.at[slot]).start(priority=k)

    @pl.when(i == 0)
    def _():
        fetch((pos0_ref, pos1_ref), 0)

    for parity in range(2):
        @pl.when((i + 1 < n) & ((i + 1) % 2 == parity))
        def _():
            fetch((pos0_next_ref, pos1_next_ref), parity)

    slot = i % 2

    def drain(r, _):
        pltpu.make_async_copy(_tokens(ys_ref, 0, 1, tile), _tokens(buf.at[slot, 0], 0, 1, tile),
                              sem.at[slot]).wait()
        return 0

    lax.fori_loop(0, 2 * tm, drain, 0, unroll=8)

    ew = ew_ref[...]
    y = (ew[:, 0:1] * _load_token_tiles(buf.at[slot, 0], tm)
         + ew[:, 1:2] * _load_token_tiles(buf.at[slot, 1], tm))
    g2 = mod_ref[0, :, 5 * d:6 * d]
    x2 = x1_ref[...] + g2 * y
    o_ref[...] = _rms(x2, g_ref[...]) if final_norm else x2


def _combine(pos_flat, ys, x1, ew, mod3, final_g, seq, final_norm):
    t, d = x1.shape
    tm = MOVE_ROWS
    n = t // tm
    per_batch = seq // tm
    return pl.pallas_call(
        functools.partial(_combine_kernel, d=d, final_norm=final_norm),
        grid=(n,),
        in_specs=[pl.BlockSpec((tm,), lambda i: (i,), memory_space=pltpu.SMEM),
                  pl.BlockSpec((tm,), lambda i: (n + i,), memory_space=pltpu.SMEM),
                  pl.BlockSpec((tm,), lambda i: (jnp.minimum(i + 1, n - 1),), memory_space=pltpu.SMEM),
                  pl.BlockSpec((tm,), lambda i: (n + jnp.minimum(i + 1, n - 1),), memory_space=pltpu.SMEM),
                  pl.BlockSpec(memory_space=pl.ANY),
                  pl.BlockSpec((tm, d), lambda i: (i, 0)),
                  pl.BlockSpec((tm, LANES), lambda i: (i, 0)),
                  pl.BlockSpec((1, 1, mod3.shape[2]), lambda i: (i // per_batch, 0, 0)),
                  pl.BlockSpec((1, d), lambda i: (0, 0))],
        out_specs=pl.BlockSpec((tm, d), lambda i: (i, 0)),
        out_shape=jax.ShapeDtypeStruct((t, d), F32),
        scratch_shapes=[pltpu.VMEM((2, 2, tm * d // LANES, LANES), F32), pltpu.SemaphoreType.DMA((2,))],
        compiler_params=_params(1),
        name="combine",
    )(pos_flat, pos_flat, pos_flat, pos_flat, ys, x1, ew, mod3, final_g)


def _rope_tables(seq):
    inv = 1.0 / (ROPE_THETA ** (jnp.arange(0, HEAD_DIM, 2, dtype=F32) / HEAD_DIM))
    ang = jnp.arange(seq, dtype=F32)[:, None] * inv[None, :]
    cos, sin = jnp.cos(ang), jnp.sin(ang)
    cos_head = jnp.concatenate([cos, cos], axis=1)
    sin_head = jnp.concatenate([-sin, sin], axis=1)
    reps = LANES // HEAD_DIM
    return jnp.tile(cos_head, (1, reps)), jnp.tile(sin_head, (1, reps))


def kernel(x, c, w_ada, b_ada, norm1_g, w_in, lambda_q1, lambda_k1, lambda_q2, lambda_k2,
           diff_subln_g, w_proj_moba, w_proj_diff, w_out, norm2_g, w_group, b_group,
           w_expert, b_expert, w_gate, w_up, w_down, final_g):
    batch, seq, d = x.shape
    depth = w_ada.shape[0]
    t = batch * seq
    assert seq % PROJ_ROWS == 0 and seq % MOBA_BLOCK == 0 and seq // MOBA_BLOCK <= LANES
    assert ATTN_ROWS == MOBA_BLOCK and d % SEG == 0 and t % SORT_ROWS == 0 and batch <= 8
    assert seq % ATTN_Q_ROWS == 0 and ATTN_Q_ROWS % ATTN_ROWS == 0
    assert EXPERT_ROWS & (EXPERT_ROWS - 1) == 0
    n_rows = 2 * t + N_EXPERTS * EXPERT_ROWS
    n_tiles = n_rows // EXPERT_ROWS
    cos, sin = _rope_tables(seq)
    c_pad = jnp.zeros((8, d), F32).at[:batch].set(c)
    xf = x.reshape(t, d)
    row = lambda v: v.reshape(1, -1)
    for l in range(depth):
        mod = _ada(c_pad, w_ada[l], row(b_ada[l]))
        mod3 = mod[:batch].reshape(batch, 1, 6 * d)
        w_qkv = w_in[l][:, :N_QKV_SEGS * SEG].astype(BF16)
        w_gates = w_in[l][:, N_QKV_SEGS * SEG:].astype(BF16)
        qm, km, vm, qd, kd, vd, kmean = _proj(xf, mod3, row(norm1_g[l]), w_qkv, cos, sin, seq)
        kmean = kmean.reshape(batch, seq // MOBA_BLOCK, SEG)
        om = _moba(qm, km, vm, kmean, batch, seq)
        lam_init = 0.8 - 0.6 * math.exp(-0.3 * l)
        od = _diff(qd, kd, vd, row(lambda_q1[l]), row(lambda_k1[l]), row(lambda_q2[l]),
                   row(lambda_k2[l]), row(diff_subln_g[l]), batch, seq, lam_init)
        w_router = jnp.zeros((d, LANES), F32)
        w_router = w_router.at[:, :N_GROUPS].set(w_group[l])
        w_router = w_router.at[:, N_GROUPS:N_GROUPS + N_EXPERTS].set(w_expert[l])
        b_router = jnp.zeros((1, LANES), F32)
        b_router = b_router.at[0, :N_GROUPS].set(b_group[l])
        b_router = b_router.at[0, N_GROUPS:N_GROUPS + N_EXPERTS].set(b_expert[l])
        x1, h2, eid_t, ew, counts = _merge(xf, om, od, mod3, row(norm1_g[l]), row(norm2_g[l]), w_gates,
                                 w_proj_moba[l].astype(BF16), w_proj_diff[l].astype(BF16),
                                 w_out[l].astype(BF16), w_router, b_router, seq)
        pos_t, tiles, segs = _pos(eid_t, counts, n_tiles)
        pos_flat = pos_t[:2].reshape(2 * t)
        xs = _dispatch(segs[0, :N_EXPERTS], segs[8, :N_EXPERTS], pos_flat, h2, n_rows, d // LANES)
        ys = _experts(tiles[:, 0], tiles[:, 1], xs, w_gate[l], w_up[l], w_down[l])
        xf = _combine(pos_flat, ys, x1, ew, mod3, row(final_g), seq, final_norm=(l == depth - 1))
    return xf.reshape(batch, seq, d)
```

```python
import functools
import math

import jax
import jax.numpy as jnp
from jax import lax
from jax.experimental import pallas as pl
from jax.experimental.pallas import tpu as pltpu

F32 = jnp.float32
BF16 = jnp.bfloat16
I32 = jnp.int32

LANES = 128
HEAD_DIM = 64
HEADS_PER_VREG = LANES // HEAD_DIM
MOBA_HEADS = 8
MOBA_BLOCK = 256
MOBA_TOPK = 3
DIFF_HEADS = 4
ROPE_THETA = 10000.0
N_GROUPS = 4
EXPERTS_PER_GROUP = 8
N_EXPERTS = N_GROUPS * EXPERTS_PER_GROUP
NORM_EPS = 1e-6
NEG_INF = -1e30
SEG = MOBA_HEADS * HEAD_DIM
N_QKV_SEGS = 6
SUM_ROWS = 16

PROJ_ROWS = 512
ATTN_ROWS = 256
ATTN_Q_ROWS = 1024
MERGE_ROWS = 512
SORT_ROWS = 512
EXPERT_ROWS = 512
EXPERT_RING = 3
MOVE_ROWS = 256
VMEM_LIMIT = 56 * 1024 * 1024


def _params(n_axes, vmem=VMEM_LIMIT):
    return pltpu.CompilerParams(dimension_semantics=("arbitrary",) * n_axes,
                                vmem_limit_bytes=vmem)


def _dot(a, b):
    return jnp.dot(a, b, preferred_element_type=F32)


def _dot_nt(a, b):
    return lax.dot_general(a, b, (((1,), (1,)), ((), ())), preferred_element_type=F32)


def _store_token_tiles(ref, x):
    n, d = x.shape
    chunks = d // LANES
    for c in range(chunks):
        ref[pl.ds(c, n, stride=chunks), :] = x[:, c * LANES:(c + 1) * LANES]


def _load_token_tiles(ref, n):
    chunks = ref.shape[0] // n
    return jnp.concatenate([ref[pl.ds(c, n, stride=chunks), :] for c in range(chunks)], axis=1)


def _rms(x, g):
    return x * lax.rsqrt(jnp.mean(x * x, axis=-1, keepdims=True) + NORM_EPS) * g


def _ada_kernel(c_ref, w_ref, b_ref, o_ref):
    c = c_ref[...]
    o_ref[...] = _dot(c * jax.nn.sigmoid(c), w_ref[...]) + b_ref[...]


def _ada(c_pad, w, b):
    rows, d = c_pad.shape
    n = w.shape[1]
    tn = 1536
    return pl.pallas_call(
        _ada_kernel,
        grid=(n // tn,),
        in_specs=[pl.BlockSpec((rows, d), lambda j: (0, 0)),
                  pl.BlockSpec((d, tn), lambda j: (0, j)),
                  pl.BlockSpec((1, tn), lambda j: (0, j))],
        out_specs=pl.BlockSpec((rows, tn), lambda j: (0, j)),
        out_shape=jax.ShapeDtypeStruct((rows, n), F32),
        compiler_params=_params(1),
        name="ada",
    )(c_pad, w, b)


def _rope(x, cos, sin_signed):
    half = HEAD_DIM // 2
    width = x.shape[1]
    lane = lax.broadcasted_iota(I32, x.shape, 1)
    first = (lane & (HEAD_DIM - 1)) < half
    partner = jnp.where(first, pltpu.roll(x, width - half, 1), pltpu.roll(x, half, 1))
    return x * cos + partner * sin_signed


def _proj_kernel(x_ref, mod_ref, g_ref, w_ref, cos_ref, sin_ref,
                 qm_ref, km_ref, vm_ref, qd_ref, kd_ref, vd_ref, kmean_ref, *, d):
    x = x_ref[...]
    sh = mod_ref[0, :, 0:d]
    sc = mod_ref[0, :, d:2 * d]
    h = (_rms(x, g_ref[...]) * (1.0 + sc) + sh).astype(BF16)
    cos = jnp.concatenate([cos_ref[...]] * (SEG // LANES), axis=1)
    sin = jnp.concatenate([sin_ref[...]] * (SEG // LANES), axis=1)
    scale = HEAD_DIM ** -0.5 * math.log2(math.e)
    outs = (qm_ref, km_ref, vm_ref, qd_ref, kd_ref, vd_ref)
    n_blk = x.shape[0] // ATTN_ROWS
    for seg, o_ref in enumerate(outs):
        y = _dot(h, w_ref[:, seg * SEG:(seg + 1) * SEG])
        if seg in (0, 1, 3, 4):
            y = _rope(y, cos, sin)
        if seg in (0, 3):
            y = y * scale
        if seg == 1:
            for blk in range(n_blk):
                rows = y[blk * MOBA_BLOCK:(blk + 1) * MOBA_BLOCK]
                kmean_ref[0, blk:blk + 1, :] = jnp.mean(rows, axis=0, keepdims=True)
        if seg in (2, 5):
            for blk in range(n_blk):
                for part in range(SEG // LANES):
                    piece = y[blk * ATTN_ROWS:(blk + 1) * ATTN_ROWS, part * LANES:(part + 1) * LANES]
                    o_ref[blk, part * LANES:(part + 1) * LANES, :] = piece.T.astype(BF16)
        else:
            o_ref[...] = y.astype(BF16)


def _proj(x2, mod3, g, w_qkv, cos, sin, seq):
    t, d = x2.shape
    tm = PROJ_ROWS
    per_batch = seq // tm
    row_spec = pl.BlockSpec((tm, SEG), lambda i: (i, 0))
    tab_spec = pl.BlockSpec((tm, LANES), lambda i: (i % per_batch, 0))
    act = jax.ShapeDtypeStruct((t, SEG), BF16)
    act_t = jax.ShapeDtypeStruct((t // ATTN_ROWS, SEG, ATTN_ROWS), BF16)
    t_spec = pl.BlockSpec((tm // ATTN_ROWS, SEG, ATTN_ROWS), lambda i: (i, 0, 0))
    return pl.pallas_call(
        functools.partial(_proj_kernel, d=d),
        grid=(t // tm,),
        in_specs=[pl.BlockSpec((tm, d), lambda i: (i, 0)),
                  pl.BlockSpec((1, 1, mod3.shape[2]), lambda i: (i // per_batch, 0, 0)),
                  pl.BlockSpec((1, d), lambda i: (0, 0)),
                  pl.BlockSpec(w_qkv.shape, lambda i: (0, 0)),
                  tab_spec, tab_spec],
        out_specs=[row_spec, row_spec, t_spec, row_spec, row_spec, t_spec,
                   pl.BlockSpec((1, tm // MOBA_BLOCK, SEG), lambda i: (i, 0, 0))],
        out_shape=[act, act, act_t, act, act, act_t,
                   jax.ShapeDtypeStruct((t // tm, tm // MOBA_BLOCK, SEG), F32)],
        compiler_params=_params(1),
        name="proj",
    )(x2, mod3, g, w_qkv, cos, sin)


def _attend(qs, keys_of_tile, k_ref, vt_ref, v_rows, q_tile, scratch):
    tq, tk = ATTN_Q_ROWS, ATTN_ROWS
    sub = tq // tk
    ones = jnp.ones((SUM_ROWS, tk), BF16)
    n_soft = len(qs)
    acc_refs, max_refs = scratch[:n_soft], scratch[n_soft:]
    for acc_ref in acc_refs:
        acc_ref[...] = jnp.zeros(acc_ref.shape, F32)

    def scores(j, first_query=0):
        start = pl.multiple_of(j * tk, tk)
        keys = keys_of_tile(j, k_ref[pl.ds(start, tk), :])
        return tuple(_dot(keys, q[:, first_query:]) for q in qs), vt_ref[j]

    def update(tile, mask, maxes, first_query=0):
        s_all, vt = tile
        new = ()
        for n, s in enumerate(s_all):
            if mask is not None:
                own = jnp.where(mask, s[:, :tk], NEG_INF)
                s = own if s.shape[1] == tk else jnp.concatenate([own, s[:, tk:]], axis=1)
            m = max_refs[n][:, first_query:] if maxes is None else maxes[n]
            m_new = jnp.maximum(m, jnp.max(s, axis=0, keepdims=True))
            alpha = jnp.exp2(m - m_new)
            p = jnp.exp2(s - m_new).astype(BF16)
            if maxes is None:
                max_refs[n][:, first_query:] = m_new
            new += (m_new,)
            vt_sum = jnp.concatenate([vt[v_rows[n]], ones], axis=0)
            acc_refs[n][:, first_query:] = alpha * acc_refs[n][:, first_query:] + _dot(vt_sum, p)
        return new

    first_own = q_tile * sub

    def body(group, maxes):
        for tile in [scores(sub * group + b) for b in range(sub)]:
            maxes = update(tile, None, maxes)
        return maxes

    maxes = lax.fori_loop(0, q_tile, body, (jnp.full((1, tq), NEG_INF, F32),) * n_soft)
    for max_ref, m in zip(max_refs, maxes):
        max_ref[...] = m
    own = [scores(first_own + b, b * tk) for b in range(sub)]
    causal = lax.broadcasted_iota(I32, (tk, tk), 0) <= lax.broadcasted_iota(I32, (tk, tk), 1)
    for b in range(sub):
        update(own[b], causal, None, b * tk)
    out = []
    for n, acc_ref in enumerate(acc_refs):
        rows = v_rows[n].stop - v_rows[n].start
        out.append((acc_ref[rows:rows + 1, :], acc_ref[0:rows, :]))
    return out


def _softmax_scratch(n_softmax, features):
    return ([pltpu.VMEM((features + SUM_ROWS, ATTN_Q_ROWS), F32)] * n_softmax
            + [pltpu.VMEM((1, ATTN_Q_ROWS), F32)] * n_softmax)


def _moba_kernel(q_ref, k_ref, v_ref, kmean_ref, o_ref, *scratch, n_blocks):
    tq = ATTN_Q_ROWS
    q_tile = pl.program_id(2)
    q_t = q_ref[...].astype(F32).T
    feat = lax.broadcasted_iota(I32, (LANES, tq), 0)
    blk = lax.broadcasted_iota(I32, (n_blocks, tq), 0)
    qry = lax.broadcasted_iota(I32, (n_blocks, tq), 1)
    own_block = q_tile * (tq // MOBA_BLOCK) + (qry >> int(math.log2(MOBA_BLOCK)))
    kmean = kmean_ref[0]
    km_head = lax.broadcasted_iota(I32, kmean.shape, 1) >> 6
    pad = jnp.zeros((LANES - n_blocks, tq), BF16)
    lane_k = lax.broadcasted_iota(I32, (ATTN_ROWS, LANES), 1)
    qs = []
    for head in range(HEADS_PER_VREG):
        qh = jnp.where((feat >> 6) == head, q_t, 0.0).astype(BF16)
        km = jnp.where(km_head == head, kmean, 0.0)
        km_hi = km.astype(BF16)
        km_lo = (km - km_hi.astype(F32)).astype(BF16)
        gate = _dot(km_hi, qh) + _dot(km_lo, qh)
        gate = jnp.where(blk < own_block, gate, NEG_INF)
        bias = jnp.where(blk == own_block, 0.0, NEG_INF)
        for _ in range(min(MOBA_TOPK, n_blocks)):
            top = jnp.max(gate, axis=0, keepdims=True)
            first = jnp.min(jnp.where(gate == top, blk, n_blocks), axis=0, keepdims=True)
            picked = blk == first
            bias = jnp.where(picked & (blk < own_block), 0.0, bias)
            gate = jnp.where(picked, -jnp.inf, gate)
        qs.append(jnp.concatenate([qh, bias.astype(BF16), pad], axis=0))

    def rhs_of_tile(j, keys):
        return jnp.concatenate([keys, jnp.where(lane_k == j, 1.0, 0.0).astype(BF16)], axis=1)

    v_rows = [slice(h * HEAD_DIM, (h + 1) * HEAD_DIM) for h in range(HEADS_PER_VREG)]
    stats = _attend(qs, rhs_of_tile, k_ref, v_ref, v_rows, q_tile, scratch)
    out_t = jnp.concatenate([acc / l for l, acc in stats], axis=0)
    o_ref[...] = out_t.T.astype(BF16)


def _moba(qm, km, vm_t, kmean, batch, seq):
    t = qm.shape[0]
    tq = ATTN_Q_ROWS
    nq = seq // tq
    n_blocks = seq // MOBA_BLOCK
    pairs = MOBA_HEADS // HEADS_PER_VREG
    n_kv = seq // ATTN_ROWS
    return pl.pallas_call(
        functools.partial(_moba_kernel, n_blocks=n_blocks),
        grid=(batch, pairs, nq),
        in_specs=[pl.BlockSpec((tq, LANES), lambda b, p, i: (b * nq + i, p)),
                  pl.BlockSpec((seq, LANES), lambda b, p, i: (b, p)),
                  pl.BlockSpec((n_kv, LANES, ATTN_ROWS), lambda b, p, i: (b, p, 0)),
                  pl.BlockSpec((1, n_blocks, LANES), lambda b, p, i: (b, 0, p))],
        out_specs=pl.BlockSpec((tq, LANES), lambda b, p, i: (b * nq + i, p)),
        out_shape=jax.ShapeDtypeStruct((t, SEG), BF16),
        scratch_shapes=_softmax_scratch(HEADS_PER_VREG, HEAD_DIM),
        compiler_params=_params(3),
        name="moba",
    )(qm, km, vm_t, kmean)


def _diff_kernel(q_ref, k_ref, v_ref, lq1_ref, lk1_ref, lq2_ref, lk2_ref, g_ref, o_ref,
                 *scratch, lam_init):
    tq = ATTN_Q_ROWS
    q_t = q_ref[...].astype(F32).T
    comp = lax.broadcasted_iota(I32, (LANES, tq), 0) >> 6
    qs = [jnp.where(comp == c, q_t, 0.0).astype(BF16) for c in range(2)]
    v_rows = [slice(0, LANES)] * 2
    (l1, a1), (l2, a2) = _attend(qs, lambda j, keys: keys, k_ref, v_ref, v_rows,
                                 pl.program_id(2), scratch)
    lam = (jnp.exp(jnp.sum(lq1_ref[...] * lk1_ref[...], axis=1, keepdims=True))
           - jnp.exp(jnp.sum(lq2_ref[...] * lk2_ref[...], axis=1, keepdims=True)) + lam_init)
    o = (a1 / l1 - lam * (a2 / l2)).T
    o_ref[...] = (_rms(o, g_ref[...]) * (1.0 - lam_init)).astype(BF16)


def _diff(qd, kd, vd_t, lq1, lk1, lq2, lk2, g, batch, seq, lam_init):
    t = qd.shape[0]
    tq = ATTN_Q_ROWS
    nq = seq // tq
    n_kv = seq // ATTN_ROWS
    vec = lambda a: pl.BlockSpec(a.shape, lambda b, h, i: (0, 0))
    return pl.pallas_call(
        functools.partial(_diff_kernel, lam_init=lam_init),
        grid=(batch, DIFF_HEADS, nq),
        in_specs=[pl.BlockSpec((tq, LANES), lambda b, h, i: (b * nq + i, h)),
                  pl.BlockSpec((seq, LANES), lambda b, h, i: (b, h)),
                  pl.BlockSpec((n_kv, LANES, ATTN_ROWS), lambda b, h, i: (b, h, 0)),
                  vec(lq1), vec(lk1), vec(lq2), vec(lk2), vec(g)],
        out_specs=pl.BlockSpec((tq, LANES), lambda b, h, i: (b * nq + i, h)),
        out_shape=jax.ShapeDtypeStruct((t, SEG), BF16),
        scratch_shapes=_softmax_scratch(2, LANES),
        compiler_params=_params(3),
        name="diff",
    )(qd, kd, vd_t, lq1, lk1, lq2, lk2, g)


def _merge_kernel(x_ref, om_ref, od_ref, mod_ref, g1n_ref, g2n_ref, wg_ref, wpm_ref, wpd_ref,
                  wout_ref, wr_ref, br_ref, x1_ref, h2_ref, eid_ref, ew_ref, counts_ref, merged, counts,
                  *, d):
    x = x_ref[...]
    sh1 = mod_ref[0, :, 0:d]
    sc1 = mod_ref[0, :, d:2 * d]
    g1 = mod_ref[0, :, 2 * d:3 * d]
    sh2 = mod_ref[0, :, 3 * d:4 * d]
    sc2 = mod_ref[0, :, 4 * d:5 * d]
    h = (_rms(x, g1n_ref[...]) * (1.0 + sc1) + sh1).astype(BF16)
    om = om_ref[...]
    od = od_ref[...]
    chunk = SEG
    for c in range(d // chunk):
        cols = slice(c * chunk, (c + 1) * chunk)
        gm = _dot(h, wg_ref[:, c * chunk:(c + 1) * chunk])
        gd = _dot(h, wg_ref[:, d + c * chunk:d + (c + 1) * chunk])
        ym = _dot(om, wpm_ref[:, cols])
        yd = _dot(od, wpd_ref[:, cols])
        merged[:, cols] = (jax.nn.sigmoid(gm) * ym + jax.nn.sigmoid(gd) * yd).astype(BF16)
    x1 = x + g1 * _dot(merged[...], wout_ref[...])
    x1_ref[...] = x1
    h2 = _rms(x1, g2n_ref[...]) * (1.0 + sc2) + sh2
    _store_token_tiles(h2_ref, h2)

    h2_hi = h2.astype(BF16)
    h2_lo = (h2 - h2_hi.astype(F32)).astype(BF16)
    wr = wr_ref[...]
    wr_hi = wr.astype(BF16)
    wr_lo = (wr - wr_hi.astype(F32)).astype(BF16)
    logits = _dot(h2_hi, wr_hi) + _dot(h2_lo, wr_hi) + _dot(h2_hi, wr_lo) + br_ref[...]
    lane = lax.broadcasted_iota(I32, logits.shape, 1)
    is_group = lane < N_GROUPS
    gl = jnp.where(is_group, logits, -jnp.inf)
    gmax = jnp.max(gl, axis=1, keepdims=True)
    gexp = jnp.exp(gl - gmax)
    g_w = 1.0 / jnp.sum(gexp, axis=1, keepdims=True)
    g_idx = jnp.min(jnp.where(gl == gmax, lane, LANES), axis=1, keepdims=True)
    e_lane = lane - N_GROUPS
    in_group = (e_lane >= g_idx * EXPERTS_PER_GROUP) & (e_lane < (g_idx + 1) * EXPERTS_PER_GROUP)
    el = jnp.where(in_group, logits, -jnp.inf)
    emax = jnp.max(el, axis=1, keepdims=True)
    eexp = jnp.exp(el - emax)
    prob = eexp / jnp.sum(eexp, axis=1, keepdims=True)
    prob = jnp.where(in_group, prob, -1.0)
    p1 = jnp.max(prob, axis=1, keepdims=True)
    i1 = jnp.min(jnp.where(prob == p1, lane, LANES), axis=1, keepdims=True)
    prob2 = jnp.where(lane == i1, -1.0, prob)
    p2 = jnp.max(prob2, axis=1, keepdims=True)
    i2 = jnp.min(jnp.where(prob2 == p2, lane, LANES), axis=1, keepdims=True)
    tot = p1 + p2
    eid = jnp.where(lane == 0, i1 - N_GROUPS, jnp.where(lane == 1, i2 - N_GROUPS, 0))
    eid_ref[...] = eid.T[0:8, :]

    @pl.when(pl.program_id(0) == 0)
    def _():
        counts[...] = jnp.zeros_like(counts)

    chosen = jnp.where((lane == i1) | (lane == i2), 1.0, 0.0)
    per_lane = jnp.broadcast_to(jnp.sum(chosen, axis=0, keepdims=True), counts.shape)
    counts[...] = counts[...] + pltpu.roll(per_lane, LANES - N_GROUPS, 1)
    counts_ref[...] = counts[...].astype(I32)
    ew_ref[...] = jnp.where(lane == 0, g_w * (p1 / tot), jnp.where(lane == 1, g_w * (p2 / tot), 0.0))


def _merge(x2, om, od, mod3, g1n, g2n, wg, wpm, wpd, wout, wr, br, seq):
    t, d = x2.shape
    tm = MERGE_ROWS
    per_batch = seq // tm
    full = lambda a: pl.BlockSpec(a.shape, lambda i: (0,) * a.ndim)
    row = lambda w: pl.BlockSpec((tm, w), lambda i: (i, 0))
    return pl.pallas_call(
        functools.partial(_merge_kernel, d=d),
        grid=(t // tm,),
        in_specs=[row(d), row(SEG), row(SEG),
                  pl.BlockSpec((1, 1, mod3.shape[2]), lambda i: (i // per_batch, 0, 0)),
                  full(g1n), full(g2n), full(wg), full(wpm), full(wpd), full(wout), full(wr), full(br)],
        out_specs=[row(d), pl.BlockSpec((tm * d // LANES, LANES), lambda i: (i, 0)),
                   pl.BlockSpec((8, tm), lambda i: (0, i)), row(LANES),
                   pl.BlockSpec((8, LANES), lambda i: (0, 0))],
        out_shape=[jax.ShapeDtypeStruct((t, d), F32), jax.ShapeDtypeStruct((t * d // LANES, LANES), F32),
                   jax.ShapeDtypeStruct((8, t), I32), jax.ShapeDtypeStruct((t, LANES), F32),
                   jax.ShapeDtypeStruct((8, LANES), I32)],
        scratch_shapes=[pltpu.VMEM((tm, d), BF16), pltpu.VMEM((8, LANES), F32)],
        compiler_params=_params(1),
        name="merge",
    )(x2, om, od, mod3, g1n, g2n, wg, wpm, wpd, wout, wr, br)


def _segment_ends(counts):
    lane = lax.broadcasted_iota(I32, counts.shape, 1)
    padded = (counts + (EXPERT_ROWS - 1)) & (-EXPERT_ROWS)
    padded = jnp.where(lane < N_EXPERTS, padded, 0)
    ends = padded
    shift = 1
    while shift < N_EXPERTS:
        ends = ends + jnp.where(lane >= shift, pltpu.roll(ends, shift, 1), 0)
        shift *= 2
    return padded, ends


def _pos_kernel(eid_ref, counts_ref, pos_ref, tile_ref, seg_ref, carry):
    tm = eid_ref.shape[1]

    @pl.when(pl.program_id(0) == 0)
    def _():
        carry[...] = jnp.zeros_like(carry)

    padded, ends = _segment_ends(counts_ref[...])
    starts = (ends - padded).astype(F32).T[0:N_EXPERTS, 0:1]
    eid = eid_ref[...]
    expert = lax.broadcasted_iota(I32, (N_EXPERTS, tm), 0)
    oh0 = expert == eid[0:1, :]
    oh1 = expert == eid[1:2, :]
    used = jnp.where(oh0 | oh1, 1.0, 0.0)
    r = lax.broadcasted_iota(I32, (tm, tm), 0)
    c = lax.broadcasted_iota(I32, (tm, tm), 1)
    earlier = jnp.where(r < c, 1.0, 0.0).astype(BF16)
    base = starts + carry[:, 0:1] + _dot(used.astype(BF16), earlier)
    p0 = jnp.sum(jnp.where(oh0, base, 0.0), axis=0, keepdims=True)
    p1 = jnp.sum(jnp.where(oh1, base, 0.0), axis=0, keepdims=True)
    slot = lax.broadcasted_iota(I32, pos_ref.shape, 0)
    pos_ref[...] = jnp.where(slot == 0, p0, jnp.where(slot == 1, p1, 0.0)).astype(I32)
    carry[...] = carry[...] + jnp.sum(used, axis=1, keepdims=True)

    @pl.when(pl.program_id(0) == 0)
    def _():
        n_tiles = tile_ref.shape[0]
        first_row = lax.broadcasted_iota(I32, (n_tiles, LANES), 0) * EXPERT_ROWS
        elane = lax.broadcasted_iota(I32, (n_tiles, LANES), 1)
        done = jnp.where((ends[0:1, :] <= first_row) & (elane < N_EXPERTS), 1, 0)
        expert = jnp.minimum(jnp.sum(done, axis=1, keepdims=True), N_EXPERTS - 1)
        total = jnp.max(ends[0:1, :], axis=1, keepdims=True)
        live = jnp.where(first_row < total, 1, 0)
        tile_ref[...] = jnp.where(elane == 0, expert, jnp.where(elane == 1, live, 0))
        seg_ref[...] = jnp.concatenate([ends, padded], axis=0)


def _pos(eid_t, counts, n_tiles):
    t = eid_t.shape[1]
    tm = SORT_ROWS
    col = pl.BlockSpec((8, tm), lambda i: (0, i))
    return pl.pallas_call(
        _pos_kernel,
        grid=(t // tm,),
        in_specs=[col, pl.BlockSpec((8, LANES), lambda i: (0, 0))],
        out_specs=[col, pl.BlockSpec((n_tiles, LANES), lambda i: (0, 0)),
                   pl.BlockSpec((16, LANES), lambda i: (0, 0))],
        out_shape=[jax.ShapeDtypeStruct((8, t), I32),
                   jax.ShapeDtypeStruct((n_tiles, LANES), I32),
                   jax.ShapeDtypeStruct((16, LANES), I32)],
        scratch_shapes=[pltpu.VMEM((N_EXPERTS, LANES), F32)],
        compiler_params=_params(1),
        name="pos",
    )(eid_t, counts)


def _tokens(ref, first, count, tile):
    start = first * tile
    if not isinstance(start, int):
        start = pl.multiple_of(start, tile)
    return ref.at[pl.ds(start, count * tile)]


def _dispatch_kernel(seg_end_ref, seg_len_ref, pos0_ref, pos1_ref, h2_ref, xs_ref, zeros, sem, *, tile):
    i = pl.program_id(0)
    pos_refs = (pos0_ref, pos1_ref)
    tm = MOVE_ROWS

    def clear_copy(first):
        return pltpu.make_async_copy(zeros, _tokens(xs_ref, first, EXPERT_ROWS, tile), sem)

    @pl.when(i == 0)
    def _():
        zeros[...] = jnp.zeros_like(zeros)
        for e in range(N_EXPERTS):
            @pl.when(seg_len_ref[e] > 0)
            def _():
                clear_copy(pl.multiple_of(seg_end_ref[e] - EXPERT_ROWS, EXPERT_ROWS)).start()
        for e in range(N_EXPERTS):
            @pl.when(seg_len_ref[e] > 0)
            def _():
                clear_copy(0).wait()

        first_unused = seg_end_ref[N_EXPERTS - 1] // EXPERT_ROWS
        n_tiles = xs_ref.shape[0] // (EXPERT_ROWS * tile)

        def clear(t, _):
            clear_copy(pl.multiple_of(t * EXPERT_ROWS, EXPERT_ROWS)).start()
            return 0

        def clear_done(t, _):
            clear_copy(0).wait()
            return 0

        lax.fori_loop(first_unused, n_tiles, clear, 0)
        lax.fori_loop(first_unused, n_tiles, clear_done, 0)

    for r in range(tm):
        for k in range(2):
            pltpu.make_async_copy(_tokens(h2_ref, r, 1, tile), _tokens(xs_ref, pos_refs[k][r], 1, tile),
                                  sem).start(priority=k)

    def drain(r, _):
        pltpu.make_async_copy(_tokens(h2_ref, 0, 1, tile), _tokens(xs_ref, 0, 1, tile), sem).wait()
        return 0

    lax.fori_loop(0, 2 * tm, drain, 0, unroll=8)


def _dispatch(seg_end, seg_len, pos_flat, h2_tiles, n_rows, tile):
    t = h2_tiles.shape[0] // tile
    tm = MOVE_ROWS
    n = t // tm
    grid_spec = pltpu.PrefetchScalarGridSpec(
        num_scalar_prefetch=2,
        grid=(n,),
        in_specs=[pl.BlockSpec((tm,), lambda i, *_: (i,), memory_space=pltpu.SMEM),
                  pl.BlockSpec((tm,), lambda i, *_: (n + i,), memory_space=pltpu.SMEM),
                  pl.BlockSpec((tm * tile, LANES), lambda i, *_: (i, 0))],
        out_specs=pl.BlockSpec(memory_space=pl.ANY),
        scratch_shapes=[pltpu.VMEM((EXPERT_ROWS * tile, LANES), F32), pltpu.SemaphoreType.DMA(())],
    )
    return pl.pallas_call(
        functools.partial(_dispatch_kernel, tile=tile),
        grid_spec=grid_spec,
        out_shape=jax.ShapeDtypeStruct((n_rows * tile, LANES), F32),
        compiler_params=pltpu.CompilerParams(dimension_semantics=("arbitrary",),
                                             vmem_limit_bytes=VMEM_LIMIT),
        name="dispatch",
    )(seg_end, seg_len, pos_flat, pos_flat, h2_tiles)


def _expert_kernel(tile_expert_ref, tile_live_ref, xs_hbm, wg_ref, wu_ref, wd_ref, o_ref, wg, wu, wd,
                   xbuf, xsem):
    i = pl.program_id(0)
    n = pl.num_programs(0)
    rows = xbuf.shape[1]

    def fetch(step):
        slot = step % EXPERT_RING
        start = pl.multiple_of(step * rows, rows)
        return pltpu.make_async_copy(xs_hbm.at[pl.ds(start, rows)], xbuf.at[slot], xsem.at[slot])

    @pl.when(i == 0)
    def _():
        for step in range(EXPERT_RING - 1):
            fetch(step).start()

    @pl.when(i + EXPERT_RING - 1 < n)
    def _():
        fetch(i + EXPERT_RING - 1).start()

    fetch(i).wait()
    xs_ref = xbuf.at[i % EXPERT_RING]

    @pl.when(tile_live_ref[i] > 0)
    def _():
        changed = jnp.logical_or(i == 0, tile_expert_ref[i] != tile_expert_ref[jnp.maximum(i - 1, 0)])

        @pl.when(changed)
        def _():
            wg[...] = wg_ref[0].astype(BF16)
            wu[...] = wu_ref[0].astype(BF16)
            wd[...] = wd_ref[0].astype(BF16)

        x = _load_token_tiles(xs_ref, EXPERT_ROWS).astype(BF16)
        de = wg.shape[1]
        chunk = 2 * LANES
        pre = [(_dot(x, wg[:, c:c + chunk]), _dot(x, wu[:, c:c + chunk])) for c in range(0, de, chunk)]
        hid = [((gate * jax.nn.sigmoid(gate)) * up).astype(BF16) for gate, up in pre]
        tile = o_ref.shape[0] // EXPERT_ROWS
        for oc in range(0, wd.shape[1], chunk):
            out = None
            for n, h in enumerate(hid):
                part = _dot(h, wd[n * chunk:(n + 1) * chunk, oc:oc + chunk])
                out = part if out is None else out + part
            for c in range(chunk // LANES):
                o_ref[pl.ds(oc // LANES + c, EXPERT_ROWS, stride=tile), :] = out[:, c * LANES:(c + 1) * LANES]

    @pl.when(tile_live_ref[i] == 0)
    def _():
        o_ref[...] = jnp.zeros_like(o_ref)


def _experts(tile_expert, tile_live, xs_tiles, w_gate, w_up, w_down):
    d, de = w_gate.shape[1:]
    tile = d // LANES
    n_rows = xs_tiles.shape[0] // tile
    tm = EXPERT_ROWS
    grid_spec = pltpu.PrefetchScalarGridSpec(
        num_scalar_prefetch=2,
        grid=(n_rows // tm,),
        in_specs=[pl.BlockSpec(memory_space=pl.ANY),
                  pl.BlockSpec((1, d, de), lambda i, te, tl: (te[i], 0, 0)),
                  pl.BlockSpec((1, d, de), lambda i, te, tl: (te[i], 0, 0)),
                  pl.BlockSpec((1, de, d), lambda i, te, tl: (te[i], 0, 0))],
        out_specs=pl.BlockSpec((tm * tile, LANES), lambda i, te, tl: (i, 0)),
        scratch_shapes=[pltpu.VMEM((d, de), BF16), pltpu.VMEM((d, de), BF16), pltpu.VMEM((de, d), BF16),
                        pltpu.VMEM((EXPERT_RING, tm * tile, LANES), F32),
                        pltpu.SemaphoreType.DMA((EXPERT_RING,))],
    )
    return pl.pallas_call(
        _expert_kernel,
        grid_spec=grid_spec,
        out_shape=jax.ShapeDtypeStruct(xs_tiles.shape, F32),
        compiler_params=_params(1),
        name="experts",
    )(tile_expert, tile_live, xs_tiles, w_gate, w_up, w_down)


def _combine_kernel(pos0_ref, pos1_ref, pos0_next_ref, pos1_next_ref, ys_ref, x1_ref, ew_ref, mod_ref, g_ref,
                    o_ref, buf, sem, *, d, final_norm):
    i = pl.program_id(0)
    n = pl.num_programs(0)
    tm = MOVE_ROWS
    tile = d // LANES

    def fetch(prefs, slot):
        for r in range(tm):
            for k in range(2):
                pltpu.make_async_copy(_tokens(ys_ref, prefs[k][r], 1, tile),
                                      _tokens(buf.at[slot, k], r, 1, tile),
                                      sem.at[slot]).start(priority=k)

    @pl.when(i == 0)
    def _():
        fetch((pos0_ref, pos1_ref), 0)

    for parity in range(2):
        @pl.when((i + 1 < n) & ((i + 1) % 2 == parity))
        def _():
            fetch((pos0_next_ref, pos1_next_ref), parity)

    slot = i % 2

    def drain(r, _):
        pltpu.make_async_copy(_tokens(ys_ref, 0, 1, tile), _tokens(buf.at[slot, 0], 0, 1, tile),
                              sem.at[slot]).wait()
        return 0

    lax.fori_loop(0, 2 * tm, drain, 0, unroll=8)

    ew = ew_ref[...]
    y = (ew[:, 0:1] * _load_token_tiles(buf.at[slot, 0], tm)
         + ew[:, 1:2] * _load_token_tiles(buf.at[slot, 1], tm))
    g2 = mod_ref[0, :, 5 * d:6 * d]
    x2 = x1_ref[...] + g2 * y
    o_ref[...] = _rms(x2, g_ref[...]) if final_norm else x2


def _combine(pos_flat, ys, x1, ew, mod3, final_g, seq, final_norm):
    t, d = x1.shape
    tm = MOVE_ROWS
    n = t // tm
    per_batch = seq // tm
    return pl.pallas_call(
        functools.partial(_combine_kernel, d=d, final_norm=final_norm),
        grid=(n,),
        in_specs=[pl.BlockSpec((tm,), lambda i: (i,), memory_space=pltpu.SMEM),
                  pl.BlockSpec((tm,), lambda i: (n + i,), memory_space=pltpu.SMEM),
                  pl.BlockSpec((tm,), lambda i: (jnp.minimum(i + 1, n - 1),), memory_space=pltpu.SMEM),
                  pl.BlockSpec((tm,), lambda i: (n + jnp.minimum(i + 1, n - 1),), memory_space=pltpu.SMEM),
                  pl.BlockSpec(memory_space=pl.ANY),
                  pl.BlockSpec((tm, d), lambda i: (i, 0)),
                  pl.BlockSpec((tm, LANES), lambda i: (i, 0)),
                  pl.BlockSpec((1, 1, mod3.shape[2]), lambda i: (i // per_batch, 0, 0)),
                  pl.BlockSpec((1, d), lambda i: (0, 0))],
        out_specs=pl.BlockSpec((tm, d), lambda i: (i, 0)),
        out_shape=jax.ShapeDtypeStruct((t, d), F32),
        scratch_shapes=[pltpu.VMEM((2, 2, tm * d // LANES, LANES), F32), pltpu.SemaphoreType.DMA((2,))],
        compiler_params=_params(1),
        name="combine",
    )(pos_flat, pos_flat, pos_flat, pos_flat, ys, x1, ew, mod3, final_g)


def _rope_tables(seq):
    inv = 1.0 / (ROPE_THETA ** (jnp.arange(0, HEAD_DIM, 2, dtype=F32) / HEAD_DIM))
    ang = jnp.arange(seq, dtype=F32)[:, None] * inv[None, :]
    cos, sin = jnp.cos(ang), jnp.sin(ang)
    cos_head = jnp.concatenate([cos, cos], axis=1)
    sin_head = jnp.concatenate([-sin, sin], axis=1)
    reps = LANES // HEAD_DIM
    return jnp.tile(cos_head, (1, reps)), jnp.tile(sin_head, (1, reps))


def kernel(x, c, w_ada, b_ada, norm1_g, w_in, lambda_q1, lambda_k1, lambda_q2, lambda_k2,
           diff_subln_g, w_proj_moba, w_proj_diff, w_out, norm2_g, w_group, b_group,
           w_expert, b_expert, w_gate, w_up, w_down, final_g):
    batch, seq, d = x.shape
    depth = w_ada.shape[0]
    t = batch * seq
    assert seq % PROJ_ROWS == 0 and seq % MOBA_BLOCK == 0 and seq // MOBA_BLOCK <= LANES
    assert ATTN_ROWS == MOBA_BLOCK and d % SEG == 0 and t % SORT_ROWS == 0 and batch <= 8
    assert seq % ATTN_Q_ROWS == 0 and ATTN_Q_ROWS % ATTN_ROWS == 0
    assert EXPERT_ROWS & (EXPERT_ROWS - 1) == 0
    n_rows = 2 * t + N_EXPERTS * EXPERT_ROWS
    n_tiles = n_rows // EXPERT_ROWS
    cos, sin = _rope_tables(seq)
    c_pad = jnp.zeros((8, d), F32).at[:batch].set(c)
    xf = x.reshape(t, d)
    row = lambda v: v.reshape(1, -1)
    for l in range(depth):
        mod = _ada(c_pad, w_ada[l], row(b_ada[l]))
        mod3 = mod[:batch].reshape(batch, 1, 6 * d)
        w_qkv = w_in[l][:, :N_QKV_SEGS * SEG].astype(BF16)
        w_gates = w_in[l][:, N_QKV_SEGS * SEG:].astype(BF16)
        qm, km, vm, qd, kd, vd, kmean = _proj(xf, mod3, row(norm1_g[l]), w_qkv, cos, sin, seq)
        kmean = kmean.reshape(batch, seq // MOBA_BLOCK, SEG)
        om = _moba(qm, km, vm, kmean, batch, seq)
        lam_init = 0.8 - 0.6 * math.exp(-0.3 * l)
        od = _diff(qd, kd, vd, row(lambda_q1[l]), row(lambda_k1[l]), row(lambda_q2[l]),
                   row(lambda_k2[l]), row(diff_subln_g[l]), batch, seq, lam_init)
        w_router = jnp.zeros((d, LANES), F32)
        w_router = w_router.at[:, :N_GROUPS].set(w_group[l])
        w_router = w_router.at[:, N_GROUPS:N_GROUPS + N_EXPERTS].set(w_expert[l])
        b_router = jnp.zeros((1, LANES), F32)
        b_router = b_router.at[0, :N_GROUPS].set(b_group[l])
        b_router = b_router.at[0, N_GROUPS:N_GROUPS + N_EXPERTS].set(b_expert[l])
        x1, h2, eid_t, ew, counts = _merge(xf, om, od, mod3, row(norm1_g[l]), row(norm2_g[l]), w_gates,
                                 w_proj_moba[l].astype(BF16), w_proj_diff[l].astype(BF16),
                                 w_out[l].astype(BF16), w_router, b_router, seq)
        pos_t, tiles, segs = _pos(eid_t, counts, n_tiles)
        pos_flat = pos_t[:2].reshape(2 * t)
        xs = _dispatch(segs[0, :N_EXPERTS], segs[8, :N_EXPERTS], pos_flat, h2, n_rows, d // LANES)
        ys = _experts(tiles[:, 0], tiles[:, 1], xs, w_gate[l], w_up[l], w_down[l])
        xf = _combine(pos_flat, ys, x1, ew, mod3, row(final_g), seq, final_norm=(l == depth - 1))
    return xf.reshape(batch, seq, d)
```

```python
import functools
import math

import jax
import jax.numpy as jnp
from jax import lax
from jax.experimental import pallas as pl
from jax.experimental.pallas import tpu as pltpu

F32 = jnp.float32
BF16 = jnp.bfloat16
I32 = jnp.int32

LANES = 128
HEAD_DIM = 64
HEADS_PER_VREG = LANES // HEAD_DIM
MOBA_HEADS = 8
MOBA_BLOCK = 256
MOBA_TOPK = 3
DIFF_HEADS = 4
ROPE_THETA = 10000.0
N_GROUPS = 4
EXPERTS_PER_GROUP = 8
N_EXPERTS = N_GROUPS * EXPERTS_PER_GROUP
NORM_EPS = 1e-6
NEG_INF = -1e30
SEG = MOBA_HEADS * HEAD_DIM
N_QKV_SEGS = 6
SUM_ROWS = 16

PROJ_ROWS = 512
ATTN_ROWS = 256
ATTN_Q_ROWS = 1024
MERGE_ROWS = 512
SORT_ROWS = 512
EXPERT_ROWS = 512
EXPERT_RING = 3
MOVE_ROWS = 256
VMEM_LIMIT = 56 * 1024 * 1024


def _params(n_axes, vmem=VMEM_LIMIT):
    return pltpu.CompilerParams(dimension_semantics=("arbitrary",) * n_axes,
                                vmem_limit_bytes=vmem)


def _dot(a, b):
    return jnp.dot(a, b, preferred_element_type=F32)


def _dot_nt(a, b):
    return lax.dot_general(a, b, (((1,), (1,)), ((), ())), preferred_element_type=F32)


def _store_token_tiles(ref, x):
    n, d = x.shape
    chunks = d // LANES
    for c in range(chunks):
        ref[pl.ds(c, n, stride=chunks), :] = x[:, c * LANES:(c + 1) * LANES]


def _load_token_tiles(ref, n):
    chunks = ref.shape[0] // n
    return jnp.concatenate([ref[pl.ds(c, n, stride=chunks), :] for c in range(chunks)], axis=1)


def _rms(x, g):
    return x * lax.rsqrt(jnp.mean(x * x, axis=-1, keepdims=True) + NORM_EPS) * g


def _ada_kernel(c_ref, w_ref, b_ref, o_ref):
    c = c_ref[...]
    o_ref[...] = _dot(c * jax.nn.sigmoid(c), w_ref[...]) + b_ref[...]


def _ada(c_pad, w, b):
    rows, d = c_pad.shape
    n = w.shape[1]
    tn = 1536
    return pl.pallas_call(
        _ada_kernel,
        grid=(n // tn,),
        in_specs=[pl.BlockSpec((rows, d), lambda j: (0, 0)),
                  pl.BlockSpec((d, tn), lambda j: (0, j)),
                  pl.BlockSpec((1, tn), lambda j: (0, j))],
        out_specs=pl.BlockSpec((rows, tn), lambda j: (0, j)),
        out_shape=jax.ShapeDtypeStruct((rows, n), F32),
        compiler_params=_params(1),
        name="ada",
    )(c_pad, w, b)


def _rope(x, cos, sin_signed):
    half = HEAD_DIM // 2
    width = x.shape[1]
    lane = lax.broadcasted_iota(I32, x.shape, 1)
    first = (lane & (HEAD_DIM - 1)) < half
    partner = jnp.where(first, pltpu.roll(x, width - half, 1), pltpu.roll(x, half, 1))
    return x * cos + partner * sin_signed


def _proj_kernel(x_ref, mod_ref, g_ref, w_ref, cos_ref, sin_ref,
                 qm_ref, km_ref, vm_ref, qd_ref, kd_ref, vd_ref, kmean_ref, *, d):
    x = x_ref[...]
    sh = mod_ref[0, :, 0:d]
    sc = mod_ref[0, :, d:2 * d]
    h = (_rms(x, g_ref[...]) * (1.0 + sc) + sh).astype(BF16)
    cos = jnp.concatenate([cos_ref[...]] * (SEG // LANES), axis=1)
    sin = jnp.concatenate([sin_ref[...]] * (SEG // LANES), axis=1)
    scale = HEAD_DIM ** -0.5 * math.log2(math.e)
    outs = (qm_ref, km_ref, vm_ref, qd_ref, kd_ref, vd_ref)
    n_blk = x.shape[0] // ATTN_ROWS
    for seg, o_ref in enumerate(outs):
        y = _dot(h, w_ref[:, seg * SEG:(seg + 1) * SEG])
        if seg in (0, 1, 3, 4):
            y = _rope(y, cos, sin)
        if seg in (0, 3):
            y = y * scale
        if seg == 1:
            for blk in range(n_blk):
                rows = y[blk * MOBA_BLOCK:(blk + 1) * MOBA_BLOCK]
                kmean_ref[0, blk:blk + 1, :] = jnp.mean(rows, axis=0, keepdims=True)
        if seg in (0, 2, 3, 5):
            for blk in range(n_blk):
                for part in range(SEG // LANES):
                    piece = y[blk * ATTN_ROWS:(blk + 1) * ATTN_ROWS, part * LANES:(part + 1) * LANES]
                    o_ref[blk, part * LANES:(part + 1) * LANES, :] = piece.T.astype(BF16)
        else:
            o_ref[...] = y.astype(BF16)


def _proj(x2, mod3, g, w_qkv, cos, sin, seq):
    t, d = x2.shape
    tm = PROJ_ROWS
    per_batch = seq // tm
    row_spec = pl.BlockSpec((tm, SEG), lambda i: (i, 0))
    tab_spec = pl.BlockSpec((tm, LANES), lambda i: (i % per_batch, 0))
    act = jax.ShapeDtypeStruct((t, SEG), BF16)
    act_t = jax.ShapeDtypeStruct((t // ATTN_ROWS, SEG, ATTN_ROWS), BF16)
    t_spec = pl.BlockSpec((tm // ATTN_ROWS, SEG, ATTN_ROWS), lambda i: (i, 0, 0))
    return pl.pallas_call(
        functools.partial(_proj_kernel, d=d),
        grid=(t // tm,),
        in_specs=[pl.BlockSpec((tm, d), lambda i: (i, 0)),
                  pl.BlockSpec((1, 1, mod3.shape[2]), lambda i: (i // per_batch, 0, 0)),
                  pl.BlockSpec((1, d), lambda i: (0, 0)),
                  pl.BlockSpec(w_qkv.shape, lambda i: (0, 0)),
                  tab_spec, tab_spec],
        out_specs=[t_spec, row_spec, t_spec, t_spec, row_spec, t_spec,
                   pl.BlockSpec((1, tm // MOBA_BLOCK, SEG), lambda i: (i, 0, 0))],
        out_shape=[act_t, act, act_t, act_t, act, act_t,
                   jax.ShapeDtypeStruct((t // tm, tm // MOBA_BLOCK, SEG), F32)],
        compiler_params=_params(1),
        name="proj",
    )(x2, mod3, g, w_qkv, cos, sin)


def _attend(qs, keys_of_tile, k_ref, vt_ref, v_rows, q_tile, scratch):
    tq, tk = ATTN_Q_ROWS, ATTN_ROWS
    sub = tq // tk
    ones = jnp.ones((SUM_ROWS, tk), BF16)
    n_soft = len(qs)
    acc_refs, max_refs = scratch[:n_soft], scratch[n_soft:]
    for acc_ref in acc_refs:
        acc_ref[...] = jnp.zeros(acc_ref.shape, F32)

    def scores(j, first_query=0):
        start = pl.multiple_of(j * tk, tk)
        keys = keys_of_tile(j, k_ref[pl.ds(start, tk), :])
        return tuple(_dot(keys, q[:, first_query:]) for q in qs), vt_ref[j]

    def update(tile, mask, maxes, first_query=0):
        s_all, vt = tile
        new = ()
        for n, s in enumerate(s_all):
            if mask is not None:
                own = jnp.where(mask, s[:, :tk], NEG_INF)
                s = own if s.shape[1] == tk else jnp.concatenate([own, s[:, tk:]], axis=1)
            m = max_refs[n][:, first_query:] if maxes is None else maxes[n]
            m_new = jnp.maximum(m, jnp.max(s, axis=0, keepdims=True))
            alpha = jnp.exp2(m - m_new)
            p = jnp.exp2(s - m_new).astype(BF16)
            if maxes is None:
                max_refs[n][:, first_query:] = m_new
            new += (m_new,)
            vt_sum = jnp.concatenate([vt[v_rows[n]], ones], axis=0)
            acc_refs[n][:, first_query:] = alpha * acc_refs[n][:, first_query:] + _dot(vt_sum, p)
        return new

    first_own = q_tile * sub

    def body(group, maxes):
        for tile in [scores(sub * group + b) for b in range(sub)]:
            maxes = update(tile, None, maxes)
        return maxes

    maxes = lax.fori_loop(0, q_tile, body, (jnp.full((1, tq), NEG_INF, F32),) * n_soft)
    for max_ref, m in zip(max_refs, maxes):
        max_ref[...] = m
    own = [scores(first_own + b, b * tk) for b in range(sub)]
    causal = lax.broadcasted_iota(I32, (tk, tk), 0) <= lax.broadcasted_iota(I32, (tk, tk), 1)
    for b in range(sub):
        update(own[b], causal, None, b * tk)
    out = []
    for n, acc_ref in enumerate(acc_refs):
        rows = v_rows[n].stop - v_rows[n].start
        out.append((acc_ref[rows:rows + 1, :], acc_ref[0:rows, :]))
    return out


def _store_position_tiles(ref, x_t):
    for b in range(ref.shape[0]):
        ref[b] = x_t[:, b * ATTN_ROWS:(b + 1) * ATTN_ROWS]


def _softmax_scratch(n_softmax, features):
    return ([pltpu.VMEM((features + SUM_ROWS, ATTN_Q_ROWS), F32)] * n_softmax
            + [pltpu.VMEM((1, ATTN_Q_ROWS), F32)] * n_softmax)


def _moba_kernel(q_ref, k_ref, v_ref, kmean_ref, o_ref, *scratch, n_blocks):
    tq = ATTN_Q_ROWS
    q_tile = pl.program_id(2)
    q_t = jnp.concatenate([q_ref[b] for b in range(q_ref.shape[0])], axis=1)
    feat = lax.broadcasted_iota(I32, (LANES, tq), 0)
    blk = lax.broadcasted_iota(I32, (n_blocks, tq), 0)
    qry = lax.broadcasted_iota(I32, (n_blocks, tq), 1)
    own_block = q_tile * (tq // MOBA_BLOCK) + (qry >> int(math.log2(MOBA_BLOCK)))
    kmean = kmean_ref[0]
    km_head = lax.broadcasted_iota(I32, kmean.shape, 1) >> 6
    pad = jnp.zeros((LANES - n_blocks, tq), BF16)
    lane_k = lax.broadcasted_iota(I32, (ATTN_ROWS, LANES), 1)
    qs = []
    for head in range(HEADS_PER_VREG):
        qh = jnp.where((feat >> 6) == head, q_t, jnp.zeros_like(q_t))
        km = jnp.where(km_head == head, kmean, 0.0)
        km_hi = km.astype(BF16)
        km_lo = (km - km_hi.astype(F32)).astype(BF16)
        gate = _dot(km_hi, qh) + _dot(km_lo, qh)
        gate = jnp.where(blk < own_block, gate, NEG_INF)
        bias = jnp.where(blk == own_block, 0.0, NEG_INF)
        for _ in range(min(MOBA_TOPK, n_blocks)):
            top = jnp.max(gate, axis=0, keepdims=True)
            first = jnp.min(jnp.where(gate == top, blk, n_blocks), axis=0, keepdims=True)
            picked = blk == first
            bias = jnp.where(picked & (blk < own_block), 0.0, bias)
            gate = jnp.where(picked, -jnp.inf, gate)
        qs.append(jnp.concatenate([qh, bias.astype(BF16), pad], axis=0))

    def rhs_of_tile(j, keys):
        return jnp.concatenate([keys, jnp.where(lane_k == j, 1.0, 0.0).astype(BF16)], axis=1)

    v_rows = [slice(h * HEAD_DIM, (h + 1) * HEAD_DIM) for h in range(HEADS_PER_VREG)]
    stats = _attend(qs, rhs_of_tile, k_ref, v_ref, v_rows, q_tile, scratch)
    out_t = jnp.concatenate([acc / l for l, acc in stats], axis=0)
    _store_position_tiles(o_ref, out_t.astype(BF16))


def _moba(qm, km, vm_t, kmean, batch, seq):
    t = batch * seq
    tq = ATTN_Q_ROWS
    nq = seq // tq
    n_blocks = seq // MOBA_BLOCK
    pairs = MOBA_HEADS // HEADS_PER_VREG
    n_kv = seq // ATTN_ROWS
    return pl.pallas_call(
        functools.partial(_moba_kernel, n_blocks=n_blocks),
        grid=(batch, pairs, nq),
        in_specs=[pl.BlockSpec((tq // ATTN_ROWS, LANES, ATTN_ROWS), lambda b, p, i: (b * nq + i, p, 0)),
                  pl.BlockSpec((seq, LANES), lambda b, p, i: (b, p)),
                  pl.BlockSpec((n_kv, LANES, ATTN_ROWS), lambda b, p, i: (b, p, 0)),
                  pl.BlockSpec((1, n_blocks, LANES), lambda b, p, i: (b, 0, p))],
        out_specs=pl.BlockSpec((tq // ATTN_ROWS, LANES, ATTN_ROWS), lambda b, p, i: (b * nq + i, p, 0)),
        out_shape=jax.ShapeDtypeStruct((t // ATTN_ROWS, SEG, ATTN_ROWS), BF16),
        scratch_shapes=_softmax_scratch(HEADS_PER_VREG, HEAD_DIM),
        compiler_params=_params(3),
        name="moba",
    )(qm, km, vm_t, kmean)


def _diff_kernel(q_ref, k_ref, v_ref, lq1_ref, lk1_ref, lq2_ref, lk2_ref, g_ref, o_ref,
                 *scratch, lam_init):
    tq = ATTN_Q_ROWS
    q_t = jnp.concatenate([q_ref[b] for b in range(q_ref.shape[0])], axis=1)
    comp = lax.broadcasted_iota(I32, (LANES, tq), 0) >> 6
    qs = [jnp.where(comp == c, q_t, jnp.zeros_like(q_t)) for c in range(2)]
    v_rows = [slice(0, LANES)] * 2
    (l1, a1), (l2, a2) = _attend(qs, lambda j, keys: keys, k_ref, v_ref, v_rows,
                                 pl.program_id(2), scratch)
    lam = (jnp.exp(jnp.sum(lq1_ref[...] * lk1_ref[...], axis=1, keepdims=True))
           - jnp.exp(jnp.sum(lq2_ref[...] * lk2_ref[...], axis=1, keepdims=True)) + lam_init)
    o = a1 / l1 - lam * (a2 / l2)
    o = o * lax.rsqrt(jnp.mean(o * o, axis=0, keepdims=True) + NORM_EPS) * g_ref[...]
    _store_position_tiles(o_ref, (o * (1.0 - lam_init)).astype(BF16))


def _diff(qd, kd, vd_t, lq1, lk1, lq2, lk2, g, batch, seq, lam_init):
    t = batch * seq
    tq = ATTN_Q_ROWS
    nq = seq // tq
    n_kv = seq // ATTN_ROWS
    vec = lambda a: pl.BlockSpec(a.shape, lambda b, h, i: (0, 0))
    return pl.pallas_call(
        functools.partial(_diff_kernel, lam_init=lam_init),
        grid=(batch, DIFF_HEADS, nq),
        in_specs=[pl.BlockSpec((tq // ATTN_ROWS, LANES, ATTN_ROWS), lambda b, h, i: (b * nq + i, h, 0)),
                  pl.BlockSpec((seq, LANES), lambda b, h, i: (b, h)),
                  pl.BlockSpec((n_kv, LANES, ATTN_ROWS), lambda b, h, i: (b, h, 0)),
                  vec(lq1), vec(lk1), vec(lq2), vec(lk2), vec(g)],
        out_specs=pl.BlockSpec((tq // ATTN_ROWS, LANES, ATTN_ROWS), lambda b, h, i: (b * nq + i, h, 0)),
        out_shape=jax.ShapeDtypeStruct((t // ATTN_ROWS, SEG, ATTN_ROWS), BF16),
        scratch_shapes=_softmax_scratch(2, LANES),
        compiler_params=_params(3),
        name="diff",
    )(qd, kd, vd_t, lq1, lk1, lq2, lk2, g)


def _merge_kernel(x_ref, om_ref, od_ref, mod_ref, g1n_ref, g2n_ref, wg_ref, wpm_ref, wpd_ref,
                  wout_ref, wr_ref, br_ref, x1_ref, h2_ref, eid_ref, ew_ref, counts_ref, merged, counts,
                  *, d):
    x = x_ref[...]
    sh1 = mod_ref[0, :, 0:d]
    sc1 = mod_ref[0, :, d:2 * d]
    g1 = mod_ref[0, :, 2 * d:3 * d]
    sh2 = mod_ref[0, :, 3 * d:4 * d]
    sc2 = mod_ref[0, :, 4 * d:5 * d]
    h = (_rms(x, g1n_ref[...]) * (1.0 + sc1) + sh1).astype(BF16)
    om, od = (jnp.concatenate([ref[b].astype(F32).T for b in range(ref.shape[0])], axis=0).astype(BF16)
              for ref in (om_ref, od_ref))
    chunk = SEG
    for c in range(d // chunk):
        cols = slice(c * chunk, (c + 1) * chunk)
        gm = _dot(h, wg_ref[:, c * chunk:(c + 1) * chunk])
        gd = _dot(h, wg_ref[:, d + c * chunk:d + (c + 1) * chunk])
        ym = _dot(om, wpm_ref[:, cols])
        yd = _dot(od, wpd_ref[:, cols])
        merged[:, cols] = (jax.nn.sigmoid(gm) * ym + jax.nn.sigmoid(gd) * yd).astype(BF16)
    x1 = x + g1 * _dot(merged[...], wout_ref[...])
    x1_ref[...] = x1
    h2 = _rms(x1, g2n_ref[...]) * (1.0 + sc2) + sh2
    _store_token_tiles(h2_ref, h2)

    h2_hi = h2.astype(BF16)
    h2_lo = (h2 - h2_hi.astype(F32)).astype(BF16)
    wr = wr_ref[...]
    wr_hi = wr.astype(BF16)
    wr_lo = (wr - wr_hi.astype(F32)).astype(BF16)
    logits = _dot(h2_hi, wr_hi) + _dot(h2_lo, wr_hi) + _dot(h2_hi, wr_lo) + br_ref[...]
    lane = lax.broadcasted_iota(I32, logits.shape, 1)
    is_group = lane < N_GROUPS
    gl = jnp.where(is_group, logits, -jnp.inf)
    gmax = jnp.max(gl, axis=1, keepdims=True)
    gexp = jnp.exp(gl - gmax)
    g_w = 1.0 / jnp.sum(gexp, axis=1, keepdims=True)
    g_idx = jnp.min(jnp.where(gl == gmax, lane, LANES), axis=1, keepdims=True)
    e_lane = lane - N_GROUPS
    in_group = (e_lane >= g_idx * EXPERTS_PER_GROUP) & (e_lane < (g_idx + 1) * EXPERTS_PER_GROUP)
    el = jnp.where(in_group, logits, -jnp.inf)
    emax = jnp.max(el, axis=1, keepdims=True)
    eexp = jnp.exp(el - emax)
    prob = eexp / jnp.sum(eexp, axis=1, keepdims=True)
    prob = jnp.where(in_group, prob, -1.0)
    p1 = jnp.max(prob, axis=1, keepdims=True)
    i1 = jnp.min(jnp.where(prob == p1, lane, LANES), axis=1, keepdims=True)
    prob2 = jnp.where(lane == i1, -1.0, prob)
    p2 = jnp.max(prob2, axis=1, keepdims=True)
    i2 = jnp.min(jnp.where(prob2 == p2, lane, LANES), axis=1, keepdims=True)
    tot = p1 + p2
    eid = jnp.where(lane == 0, i1 - N_GROUPS, jnp.where(lane == 1, i2 - N_GROUPS, 0))
    eid_ref[...] = eid.T[0:8, :]

    @pl.when(pl.program_id(0) == 0)
    def _():
        counts[...] = jnp.zeros_like(counts)

    chosen = jnp.where((lane == i1) | (lane == i2), 1.0, 0.0)
    per_lane = jnp.broadcast_to(jnp.sum(chosen, axis=0, keepdims=True), counts.shape)
    counts[...] = counts[...] + pltpu.roll(per_lane, LANES - N_GROUPS, 1)
    counts_ref[...] = counts[...].astype(I32)
    ew_ref[...] = jnp.where(lane == 0, g_w * (p1 / tot), jnp.where(lane == 1, g_w * (p2 / tot), 0.0))


def _merge(x2, om, od, mod3, g1n, g2n, wg, wpm, wpd, wout, wr, br, seq):
    t, d = x2.shape
    tm = MERGE_ROWS
    per_batch = seq // tm
    full = lambda a: pl.BlockSpec(a.shape, lambda i: (0,) * a.ndim)
    row = lambda w: pl.BlockSpec((tm, w), lambda i: (i, 0))
    tiles_t = pl.BlockSpec((tm // ATTN_ROWS, SEG, ATTN_ROWS), lambda i: (i, 0, 0))
    return pl.pallas_call(
        functools.partial(_merge_kernel, d=d),
        grid=(t // tm,),
        in_specs=[row(d), tiles_t, tiles_t,
                  pl.BlockSpec((1, 1, mod3.shape[2]), lambda i: (i // per_batch, 0, 0)),
                  full(g1n), full(g2n), full(wg), full(wpm), full(wpd), full(wout), full(wr), full(br)],
        out_specs=[row(d), pl.BlockSpec((tm * d // LANES, LANES), lambda i: (i, 0)),
                   pl.BlockSpec((8, tm), lambda i: (0, i)), row(LANES),
                   pl.BlockSpec((8, LANES), lambda i: (0, 0))],
        out_shape=[jax.ShapeDtypeStruct((t, d), F32), jax.ShapeDtypeStruct((t * d // LANES, LANES), F32),
                   jax.ShapeDtypeStruct((8, t), I32), jax.ShapeDtypeStruct((t, LANES), F32),
                   jax.ShapeDtypeStruct((8, LANES), I32)],
        scratch_shapes=[pltpu.VMEM((tm, d), BF16), pltpu.VMEM((8, LANES), F32)],
        compiler_params=_params(1),
        name="merge",
    )(x2, om, od, mod3, g1n, g2n, wg, wpm, wpd, wout, wr, br)


def _segment_ends(counts):
    lane = lax.broadcasted_iota(I32, counts.shape, 1)
    padded = (counts + (EXPERT_ROWS - 1)) & (-EXPERT_ROWS)
    padded = jnp.where(lane < N_EXPERTS, padded, 0)
    ends = padded
    shift = 1
    while shift < N_EXPERTS:
        ends = ends + jnp.where(lane >= shift, pltpu.roll(ends, shift, 1), 0)
        shift *= 2
    return padded, ends


def _pos_kernel(eid_ref, counts_ref, pos_ref, tile_ref, seg_ref, carry):
    tm = eid_ref.shape[1]

    @pl.when(pl.program_id(0) == 0)
    def _():
        carry[...] = jnp.zeros_like(carry)

    padded, ends = _segment_ends(counts_ref[...])
    starts = (ends - padded).astype(F32).T[0:N_EXPERTS, 0:1]
    eid = eid_ref[...]
    expert = lax.broadcasted_iota(I32, (N_EXPERTS, tm), 0)
    oh0 = expert == eid[0:1, :]
    oh1 = expert == eid[1:2, :]
    used = jnp.where(oh0 | oh1, 1.0, 0.0)
    r = lax.broadcasted_iota(I32, (tm, tm), 0)
    c = lax.broadcasted_iota(I32, (tm, tm), 1)
    earlier = jnp.where(r < c, 1.0, 0.0).astype(BF16)
    base = starts + carry[:, 0:1] + _dot(used.astype(BF16), earlier)
    p0 = jnp.sum(jnp.where(oh0, base, 0.0), axis=0, keepdims=True)
    p1 = jnp.sum(jnp.where(oh1, base, 0.0), axis=0, keepdims=True)
    slot = lax.broadcasted_iota(I32, pos_ref.shape, 0)
    pos_ref[...] = jnp.where(slot == 0, p0, jnp.where(slot == 1, p1, 0.0)).astype(I32)
    carry[...] = carry[...] + jnp.sum(used, axis=1, keepdims=True)

    @pl.when(pl.program_id(0) == 0)
    def _():
        n_tiles = tile_ref.shape[0]
        first_row = lax.broadcasted_iota(I32, (n_tiles, LANES), 0) * EXPERT_ROWS
        elane = lax.broadcasted_iota(I32, (n_tiles, LANES), 1)
        done = jnp.where((ends[0:1, :] <= first_row) & (elane < N_EXPERTS), 1, 0)
        expert = jnp.minimum(jnp.sum(done, axis=1, keepdims=True), N_EXPERTS - 1)
        total = jnp.max(ends[0:1, :], axis=1, keepdims=True)
        live = jnp.where(first_row < total, 1, 0)
        tile_ref[...] = jnp.where(elane == 0, expert, jnp.where(elane == 1, live, 0))
        seg_ref[...] = jnp.concatenate([ends, padded], axis=0)


def _pos(eid_t, counts, n_tiles):
    t = eid_t.shape[1]
    tm = SORT_ROWS
    col = pl.BlockSpec((8, tm), lambda i: (0, i))
    return pl.pallas_call(
        _pos_kernel,
        grid=(t // tm,),
        in_specs=[col, pl.BlockSpec((8, LANES), lambda i: (0, 0))],
        out_specs=[col, pl.BlockSpec((n_tiles, LANES), lambda i: (0, 0)),
                   pl.BlockSpec((16, LANES), lambda i: (0, 0))],
        out_shape=[jax.ShapeDtypeStruct((8, t), I32),
                   jax.ShapeDtypeStruct((n_tiles, LANES), I32),
                   jax.ShapeDtypeStruct((16, LANES), I32)],
        scratch_shapes=[pltpu.VMEM((N_EXPERTS, LANES), F32)],
        compiler_params=_params(1),
        name="pos",
    )(eid_t, counts)


def _tokens(ref, first, count, tile):
    start = first * tile
    if not isinstance(start, int):
        start = pl.multiple_of(start, tile)
    return ref.at[pl.ds(start, count * tile)]


def _dispatch_kernel(seg_end_ref, seg_len_ref, pos0_ref, pos1_ref, h2_ref, xs_ref, zeros, sem, *, tile):
    i = pl.program_id(0)
    pos_refs = (pos0_ref, pos1_ref)
    tm = MOVE_ROWS

    def clear_copy(first):
        return pltpu.make_async_copy(zeros, _tokens(xs_ref, first, EXPERT_ROWS, tile), sem)

    @pl.when(i == 0)
    def _():
        zeros[...] = jnp.zeros_like(zeros)
        for e in range(N_EXPERTS):
            @pl.when(seg_len_ref[e] > 0)
            def _():
                clear_copy(pl.multiple_of(seg_end_ref[e] - EXPERT_ROWS, EXPERT_ROWS)).start()
        for e in range(N_EXPERTS):
            @pl.when(seg_len_ref[e] > 0)
            def _():
                clear_copy(0).wait()

        first_unused = seg_end_ref[N_EXPERTS - 1] // EXPERT_ROWS
        n_tiles = xs_ref.shape[0] // (EXPERT_ROWS * tile)

        def clear(t, _):
            clear_copy(pl.multiple_of(t * EXPERT_ROWS, EXPERT_ROWS)).start()
            return 0

        def clear_done(t, _):
            clear_copy(0).wait()
            return 0

        lax.fori_loop(first_unused, n_tiles, clear, 0)
        lax.fori_loop(first_unused, n_tiles, clear_done, 0)

    for r in range(tm):
        for k in range(2):
            pltpu.make_async_copy(_tokens(h2_ref, r, 1, tile), _tokens(xs_ref, pos_refs[k][r], 1, tile),
                                  sem).start(priority=k)

    def drain(r, _):
        pltpu.make_async_copy(_tokens(h2_ref, 0, 1, tile), _tokens(xs_ref, 0, 1, tile), sem).wait()
        return 0

    lax.fori_loop(0, 2 * tm, drain, 0, unroll=8)


def _dispatch(seg_end, seg_len, pos_flat, h2_tiles, n_rows, tile):
    t = h2_tiles.shape[0] // tile
    tm = MOVE_ROWS
    n = t // tm
    grid_spec = pltpu.PrefetchScalarGridSpec(
        num_scalar_prefetch=2,
        grid=(n,),
        in_specs=[pl.BlockSpec((tm,), lambda i, *_: (i,), memory_space=pltpu.SMEM),
                  pl.BlockSpec((tm,), lambda i, *_: (n + i,), memory_space=pltpu.SMEM),
                  pl.BlockSpec((tm * tile, LANES), lambda i, *_: (i, 0))],
        out_specs=pl.BlockSpec(memory_space=pl.ANY),
        scratch_shapes=[pltpu.VMEM((EXPERT_ROWS * tile, LANES), F32), pltpu.SemaphoreType.DMA(())],
    )
    return pl.pallas_call(
        functools.partial(_dispatch_kernel, tile=tile),
        grid_spec=grid_spec,
        out_shape=jax.ShapeDtypeStruct((n_rows * tile, LANES), F32),
        compiler_params=pltpu.CompilerParams(dimension_semantics=("arbitrary",),
                                             vmem_limit_bytes=VMEM_LIMIT),
        name="dispatch",
    )(seg_end, seg_len, pos_flat, pos_flat, h2_tiles)


def _expert_kernel(tile_expert_ref, tile_live_ref, xs_hbm, wg_ref, wu_ref, wd_ref, o_ref, wg, wu, wd,
                   xbuf, xsem):
    i = pl.program_id(0)
    n = pl.num_programs(0)
    rows = xbuf.shape[1]

    def fetch(step):
        slot = step % EXPERT_RING
        start = pl.multiple_of(step * rows, rows)
        return pltpu.make_async_copy(xs_hbm.at[pl.ds(start, rows)], xbuf.at[slot], xsem.at[slot])

    @pl.when(i == 0)
    def _():
        for step in range(EXPERT_RING - 1):
            fetch(step).start()

    @pl.when(i + EXPERT_RING - 1 < n)
    def _():
        fetch(i + EXPERT_RING - 1).start()

    fetch(i).wait()
    xs_ref = xbuf.at[i % EXPERT_RING]

    @pl.when(tile_live_ref[i] > 0)
    def _():
        changed = jnp.logical_or(i == 0, tile_expert_ref[i] != tile_expert_ref[jnp.maximum(i - 1, 0)])

        @pl.when(changed)
        def _():
            wg[...] = wg_ref[0].astype(BF16)
            wu[...] = wu_ref[0].astype(BF16)
            wd[...] = wd_ref[0].astype(BF16)

        x = _load_token_tiles(xs_ref, EXPERT_ROWS).astype(BF16)
        de = wg.shape[1]
        chunk = 2 * LANES
        pre = [(_dot(x, wg[:, c:c + chunk]), _dot(x, wu[:, c:c + chunk])) for c in range(0, de, chunk)]
        hid = [((gate * jax.nn.sigmoid(gate)) * up).astype(BF16) for gate, up in pre]
        tile = o_ref.shape[0] // EXPERT_ROWS
        for oc in range(0, wd.shape[1], chunk):
            out = None
            for n, h in enumerate(hid):
                part = _dot(h, wd[n * chunk:(n + 1) * chunk, oc:oc + chunk])
                out = part if out is None else out + part
            for c in range(chunk // LANES):
                o_ref[pl.ds(oc // LANES + c, EXPERT_ROWS, stride=tile), :] = out[:, c * LANES:(c + 1) * LANES]

    @pl.when(tile_live_ref[i] == 0)
    def _():
        o_ref[...] = jnp.zeros_like(o_ref)


def _experts(tile_expert, tile_live, xs_tiles, w_gate, w_up, w_down):
    d, de = w_gate.shape[1:]
    tile = d // LANES
    n_rows = xs_tiles.shape[0] // tile
    tm = EXPERT_ROWS
    grid_spec = pltpu.PrefetchScalarGridSpec(
        num_scalar_prefetch=2,
        grid=(n_rows // tm,),
        in_specs=[pl.BlockSpec(memory_space=pl.ANY),
                  pl.BlockSpec((1, d, de), lambda i, te, tl: (te[i], 0, 0)),
                  pl.BlockSpec((1, d, de), lambda i, te, tl: (te[i], 0, 0)),
                  pl.BlockSpec((1, de, d), lambda i, te, tl: (te[i], 0, 0))],
        out_specs=pl.BlockSpec((tm * tile, LANES), lambda i, te, tl: (i, 0)),
        scratch_shapes=[pltpu.VMEM((d, de), BF16), pltpu.VMEM((d, de), BF16), pltpu.VMEM((de, d), BF16),
                        pltpu.VMEM((EXPERT_RING, tm * tile, LANES), F32),
                        pltpu.SemaphoreType.DMA((EXPERT_RING,))],
    )
    return pl.pallas_call(
        _expert_kernel,
        grid_spec=grid_spec,
        out_shape=jax.ShapeDtypeStruct(xs_tiles.shape, F32),
        compiler_params=_params(1),
        name="experts",
    )(tile_expert, tile_live, xs_tiles, w_gate, w_up, w_down)


def _combine_kernel(pos0_ref, pos1_ref, pos0_next_ref, pos1_next_ref, ys_ref, x1_ref, ew_ref, mod_ref, g_ref,
                    o_ref, buf, sem, *, d, final_norm):
    i = pl.program_id(0)
    n = pl.num_programs(0)
    tm = MOVE_ROWS
    tile = d // LANES

    def fetch(prefs, slot):
        for r in range(tm):
            for k in range(2):
                pltpu.make_async_copy(_tokens(ys_ref, prefs[k][r], 1, tile),
                                      _tokens(buf.at[slot, k], r, 1, tile),
                                      sem.at[slot]).start(priority=k)

    @pl.when(i == 0)
    def _():
        fetch((pos0_ref, pos1_ref), 0)

    for parity in range(2):
        @pl.when((i + 1 < n) & ((i + 1) % 2 == parity))
        def _():
            fetch((pos0_next_ref, pos1_next_ref), parity)

    slot = i % 2

    def drain(r, _):
        pltpu.make_async_copy(_tokens(ys_ref, 0, 1, tile), _tokens(buf.at[slot, 0], 0, 1, tile),
                              sem.at[slot]).wait()
        return 0

    lax.fori_loop(0, 2 * tm, drain, 0, unroll=8)

    ew = ew_ref[...]
    y = (ew[:, 0:1] * _load_token_tiles(buf.at[slot, 0], tm)
         + ew[:, 1:2] * _load_token_tiles(buf.at[slot, 1], tm))
    g2 = mod_ref[0, :, 5 * d:6 * d]
    x2 = x1_ref[...] + g2 * y
    o_ref[...] = _rms(x2, g_ref[...]) if final_norm else x2


def _combine(pos_flat, ys, x1, ew, mod3, final_g, seq, final_norm):
    t, d = x1.shape
    tm = MOVE_ROWS
    n = t // tm
    per_batch = seq // tm
    return pl.pallas_call(
        functools.partial(_combine_kernel, d=d, final_norm=final_norm),
        grid=(n,),
        in_specs=[pl.BlockSpec((tm,), lambda i: (i,), memory_space=pltpu.SMEM),
                  pl.BlockSpec((tm,), lambda i: (n + i,), memory_space=pltpu.SMEM),
                  pl.BlockSpec((tm,), lambda i: (jnp.minimum(i + 1, n - 1),), memory_space=pltpu.SMEM),
                  pl.BlockSpec((tm,), lambda i: (n + jnp.minimum(i + 1, n - 1),), memory_space=pltpu.SMEM),
                  pl.BlockSpec(memory_space=pl.ANY),
                  pl.BlockSpec((tm, d), lambda i: (i, 0)),
                  pl.BlockSpec((tm, LANES), lambda i: (i, 0)),
                  pl.BlockSpec((1, 1, mod3.shape[2]), lambda i: (i // per_batch, 0, 0)),
                  pl.BlockSpec((1, d), lambda i: (0, 0))],
        out_specs=pl.BlockSpec((tm, d), lambda i: (i, 0)),
        out_shape=jax.ShapeDtypeStruct((t, d), F32),
        scratch_shapes=[pltpu.VMEM((2, 2, tm * d // LANES, LANES), F32), pltpu.SemaphoreType.DMA((2,))],
        compiler_params=_params(1),
        name="combine",
    )(pos_flat, pos_flat, pos_flat, pos_flat, ys, x1, ew, mod3, final_g)


def _rope_tables(seq):
    inv = 1.0 / (ROPE_THETA ** (jnp.arange(0, HEAD_DIM, 2, dtype=F32) / HEAD_DIM))
    ang = jnp.arange(seq, dtype=F32)[:, None] * inv[None, :]
    cos, sin = jnp.cos(ang), jnp.sin(ang)
    cos_head = jnp.concatenate([cos, cos], axis=1)
    sin_head = jnp.concatenate([-sin, sin], axis=1)
    reps = LANES // HEAD_DIM
    return jnp.tile(cos_head, (1, reps)), jnp.tile(sin_head, (1, reps))


def kernel(x, c, w_ada, b_ada, norm1_g, w_in, lambda_q1, lambda_k1, lambda_q2, lambda_k2,
           diff_subln_g, w_proj_moba, w_proj_diff, w_out, norm2_g, w_group, b_group,
           w_expert, b_expert, w_gate, w_up, w_down, final_g):
    batch, seq, d = x.shape
    depth = w_ada.shape[0]
    t = batch * seq
    assert seq % PROJ_ROWS == 0 and seq % MOBA_BLOCK == 0 and seq // MOBA_BLOCK <= LANES
    assert ATTN_ROWS == MOBA_BLOCK and d % SEG == 0 and t % SORT_ROWS == 0 and batch <= 8
    assert seq % ATTN_Q_ROWS == 0 and ATTN_Q_ROWS % ATTN_ROWS == 0
    assert EXPERT_ROWS & (EXPERT_ROWS - 1) == 0
    n_rows = 2 * t + N_EXPERTS * EXPERT_ROWS
    n_tiles = n_rows // EXPERT_ROWS
    cos, sin = _rope_tables(seq)
    c_pad = jnp.zeros((8, d), F32).at[:batch].set(c)
    xf = x.reshape(t, d)
    row = lambda v: v.reshape(1, -1)
    for l in range(depth):
        mod = _ada(c_pad, w_ada[l], row(b_ada[l]))
        mod3 = mod[:batch].reshape(batch, 1, 6 * d)
        w_qkv = w_in[l][:, :N_QKV_SEGS * SEG].astype(BF16)
        w_gates = w_in[l][:, N_QKV_SEGS * SEG:].astype(BF16)
        qm, km, vm, qd, kd, vd, kmean = _proj(xf, mod3, row(norm1_g[l]), w_qkv, cos, sin, seq)
        kmean = kmean.reshape(batch, seq // MOBA_BLOCK, SEG)
        om = _moba(qm, km, vm, kmean, batch, seq)
        lam_init = 0.8 - 0.6 * math.exp(-0.3 * l)
        od = _diff(qd, kd, vd, row(lambda_q1[l]), row(lambda_k1[l]), row(lambda_q2[l]),
                   row(lambda_k2[l]), diff_subln_g[l].reshape(-1, 1), batch, seq, lam_init)
        w_router = jnp.zeros((d, LANES), F32)
        w_router = w_router.at[:, :N_GROUPS].set(w_group[l])
        w_router = w_router.at[:, N_GROUPS:N_GROUPS + N_EXPERTS].set(w_expert[l])
        b_router = jnp.zeros((1, LANES), F32)
        b_router = b_router.at[0, :N_GROUPS].set(b_group[l])
        b_router = b_router.at[0, N_GROUPS:N_GROUPS + N_EXPERTS].set(b_expert[l])
        x1, h2, eid_t, ew, counts = _merge(xf, om, od, mod3, row(norm1_g[l]), row(norm2_g[l]), w_gates,
                                 w_proj_moba[l].astype(BF16), w_proj_diff[l].astype(BF16),
                                 w_out[l].astype(BF16), w_router, b_router, seq)
        pos_t, tiles, segs = _pos(eid_t, counts, n_tiles)
        pos_flat = pos_t[:2].reshape(2 * t)
        xs = _dispatch(segs[0, :N_EXPERTS], segs[8, :N_EXPERTS], pos_flat, h2, n_rows, d // LANES)
        ys = _experts(tiles[:, 0], tiles[:, 1], xs, w_gate[l], w_up[l], w_down[l])
        xf = _combine(pos_flat, ys, x1, ew, mod3, row(final_g), seq, final_norm=(l == depth - 1))
    return xf.reshape(batch, seq, d)
```

```python
import functools
import math

import jax
import jax.numpy as jnp
from jax import lax
from jax.experimental import pallas as pl
from jax.experimental.pallas import tpu as pltpu

F32 = jnp.float32
BF16 = jnp.bfloat16
I32 = jnp.int32

LANES = 128
HEAD_DIM = 64
HEADS_PER_VREG = LANES // HEAD_DIM
MOBA_HEADS = 8
MOBA_BLOCK = 256
MOBA_TOPK = 3
DIFF_HEADS = 4
ROPE_THETA = 10000.0
N_GROUPS = 4
EXPERTS_PER_GROUP = 8
N_EXPERTS = N_GROUPS * EXPERTS_PER_GROUP
NORM_EPS = 1e-6
NEG_INF = -1e30
SEG = MOBA_HEADS * HEAD_DIM
N_QKV_SEGS = 6
SUM_ROWS = 16

PROJ_ROWS = 512
ATTN_ROWS = 256
ATTN_Q_ROWS = 1024
MERGE_ROWS = 512
SORT_ROWS = 512
EXPERT_ROWS = 512
EXPERT_RING = 3
MOVE_ROWS = 256
VMEM_LIMIT = 56 * 1024 * 1024


def _params(n_axes, vmem=VMEM_LIMIT):
    return pltpu.CompilerParams(dimension_semantics=("arbitrary",) * n_axes,
                                vmem_limit_bytes=vmem)


def _dot(a, b):
    return jnp.dot(a, b, preferred_element_type=F32)


def _dot_nt(a, b):
    return lax.dot_general(a, b, (((1,), (1,)), ((), ())), preferred_element_type=F32)


def _store_token_tiles(ref, x):
    n, d = x.shape
    chunks = d // LANES
    for c in range(chunks):
        ref[pl.ds(c, n, stride=chunks), :] = x[:, c * LANES:(c + 1) * LANES]


def _load_token_tiles(ref, n):
    chunks = ref.shape[0] // n
    return jnp.concatenate([ref[pl.ds(c, n, stride=chunks), :] for c in range(chunks)], axis=1)


def _rms(x, g):
    return x * lax.rsqrt(jnp.mean(x * x, axis=-1, keepdims=True) + NORM_EPS) * g


def _ada_kernel(c_ref, w_ref, b_ref, o_ref):
    c = c_ref[...]
    o_ref[...] = _dot(c * jax.nn.sigmoid(c), w_ref[...]) + b_ref[...]


def _ada(c_pad, w, b):
    rows, d = c_pad.shape
    n = w.shape[1]
    tn = 1536
    return pl.pallas_call(
        _ada_kernel,
        grid=(n // tn,),
        in_specs=[pl.BlockSpec((rows, d), lambda j: (0, 0)),
                  pl.BlockSpec((d, tn), lambda j: (0, j)),
                  pl.BlockSpec((1, tn), lambda j: (0, j))],
        out_specs=pl.BlockSpec((rows, tn), lambda j: (0, j)),
        out_shape=jax.ShapeDtypeStruct((rows, n), F32),
        compiler_params=_params(1),
        name="ada",
    )(c_pad, w, b)


def _rope(x, cos, sin_signed):
    half = HEAD_DIM // 2
    width = x.shape[1]
    lane = lax.broadcasted_iota(I32, x.shape, 1)
    first = (lane & (HEAD_DIM - 1)) < half
    partner = jnp.where(first, pltpu.roll(x, width - half, 1), pltpu.roll(x, half, 1))
    return x * cos + partner * sin_signed


def _proj_kernel(x_ref, mod_ref, g_ref, w_ref, cos_ref, sin_ref,
                 qm_ref, km_ref, vm_ref, qd_ref, kd_ref, vd_ref, kmean_ref, *, d):
    x = x_ref[...]
    sh = mod_ref[0, :, 0:d]
    sc = mod_ref[0, :, d:2 * d]
    h = (_rms(x, g_ref[...]) * (1.0 + sc) + sh).astype(BF16)
    cos = jnp.concatenate([cos_ref[...]] * (SEG // LANES), axis=1)
    sin = jnp.concatenate([sin_ref[...]] * (SEG // LANES), axis=1)
    scale = HEAD_DIM ** -0.5 * math.log2(math.e)
    outs = (qm_ref, km_ref, vm_ref, qd_ref, kd_ref, vd_ref)
    n_blk = x.shape[0] // ATTN_ROWS
    for seg, o_ref in enumerate(outs):
        y = _dot(h, w_ref[:, seg * SEG:(seg + 1) * SEG])
        if seg in (0, 1, 3, 4):
            y = _rope(y, cos, sin)
        if seg in (0, 3):
            y = y * scale
        if seg == 1:
            for blk in range(n_blk):
                rows = y[blk * MOBA_BLOCK:(blk + 1) * MOBA_BLOCK]
                kmean_ref[0, blk:blk + 1, :] = jnp.mean(rows, axis=0, keepdims=True)
        if seg in (0, 2, 3, 5):
            for blk in range(n_blk):
                for part in range(SEG // LANES):
                    piece = y[blk * ATTN_ROWS:(blk + 1) * ATTN_ROWS, part * LANES:(part + 1) * LANES]
                    o_ref[blk, part * LANES:(part + 1) * LANES, :] = piece.T.astype(BF16)
        else:
            o_ref[...] = y.astype(BF16)


def _proj(x2, mod3, g, w_qkv, cos, sin, seq):
    t, d = x2.shape
    tm = PROJ_ROWS
    per_batch = seq // tm
    row_spec = pl.BlockSpec((tm, SEG), lambda i: (i, 0))
    tab_spec = pl.BlockSpec((tm, LANES), lambda i: (i % per_batch, 0))
    act = jax.ShapeDtypeStruct((t, SEG), BF16)
    act_t = jax.ShapeDtypeStruct((t // ATTN_ROWS, SEG, ATTN_ROWS), BF16)
    t_spec = pl.BlockSpec((tm // ATTN_ROWS, SEG, ATTN_ROWS), lambda i: (i, 0, 0))
    return pl.pallas_call(
        functools.partial(_proj_kernel, d=d),
        grid=(t // tm,),
        in_specs=[pl.BlockSpec((tm, d), lambda i: (i, 0)),
                  pl.BlockSpec((1, 1, mod3.shape[2]), lambda i: (i // per_batch, 0, 0)),
                  pl.BlockSpec((1, d), lambda i: (0, 0)),
                  pl.BlockSpec(w_qkv.shape, lambda i: (0, 0)),
                  tab_spec, tab_spec],
        out_specs=[t_spec, row_spec, t_spec, t_spec, row_spec, t_spec,
                   pl.BlockSpec((1, tm // MOBA_BLOCK, SEG), lambda i: (i, 0, 0))],
        out_shape=[act_t, act, act_t, act_t, act, act_t,
                   jax.ShapeDtypeStruct((t // tm, tm // MOBA_BLOCK, SEG), F32)],
        compiler_params=_params(1),
        name="proj",
    )(x2, mod3, g, w_qkv, cos, sin)


def _attend(qs, keys_of_tile, k_ref, vt_ref, v_rows, q_tile, scratch):
    tq, tk = ATTN_Q_ROWS, ATTN_ROWS
    sub = tq // tk
    ones = jnp.ones((SUM_ROWS, tk), BF16)
    n_soft = len(qs)
    acc_refs, max_refs = scratch[:n_soft], scratch[n_soft:]
    for acc_ref in acc_refs:
        acc_ref[...] = jnp.zeros(acc_ref.shape, F32)

    def scores(j, first_query=0):
        start = pl.multiple_of(j * tk, tk)
        keys = keys_of_tile(j, k_ref[pl.ds(start, tk), :])
        return tuple(_dot(keys, q[:, first_query:]) for q in qs), vt_ref[j]

    def update(tile, mask, maxes, first_query=0):
        s_all, vt = tile
        new = ()
        for n, s in enumerate(s_all):
            if mask is not None:
                own = jnp.where(mask, s[:, :tk], NEG_INF)
                s = own if s.shape[1] == tk else jnp.concatenate([own, s[:, tk:]], axis=1)
            m = max_refs[n][:, first_query:] if maxes is None else maxes[n]
            m_new = jnp.maximum(m, jnp.max(s, axis=0, keepdims=True))
            alpha = jnp.exp2(m - m_new)
            p = jnp.exp2(s - m_new).astype(BF16)
            if maxes is None:
                max_refs[n][:, first_query:] = m_new
            new += (m_new,)
            vt_sum = jnp.concatenate([vt[v_rows[n]], ones], axis=0)
            acc_refs[n][:, first_query:] = alpha * acc_refs[n][:, first_query:] + _dot(vt_sum, p)
        return new

    first_own = q_tile * sub

    def body(group, maxes):
        for tile in [scores(sub * group + b) for b in range(sub)]:
            maxes = update(tile, None, maxes)
        return maxes

    maxes = lax.fori_loop(0, q_tile, body, (jnp.full((1, tq), NEG_INF, F32),) * n_soft)
    for max_ref, m in zip(max_refs, maxes):
        max_ref[...] = m
    own = [scores(first_own + b, b * tk) for b in range(sub)]
    causal = lax.broadcasted_iota(I32, (tk, tk), 0) <= lax.broadcasted_iota(I32, (tk, tk), 1)
    for b in range(sub):
        update(own[b], causal, None, b * tk)
    out = []
    for n, acc_ref in enumerate(acc_refs):
        rows = v_rows[n].stop - v_rows[n].start
        out.append((acc_ref[rows:rows + 1, :], acc_ref[0:rows, :]))
    return out


def _store_position_tiles(ref, x_t):
    for b in range(ref.shape[0]):
        ref[b] = x_t[:, b * ATTN_ROWS:(b + 1) * ATTN_ROWS]


def _softmax_scratch(n_softmax, features):
    return ([pltpu.VMEM((features + SUM_ROWS, ATTN_Q_ROWS), F32)] * n_softmax
            + [pltpu.VMEM((1, ATTN_Q_ROWS), F32)] * n_softmax)


def _moba_kernel(q_ref, k_ref, v_ref, kmean_ref, o_ref, *scratch, n_blocks):
    tq = ATTN_Q_ROWS
    q_tile = pl.program_id(2)
    q_t = jnp.concatenate([q_ref[b] for b in range(q_ref.shape[0])], axis=1)
    feat = lax.broadcasted_iota(I32, (LANES, tq), 0)
    blk = lax.broadcasted_iota(I32, (n_blocks, tq), 0)
    qry = lax.broadcasted_iota(I32, (n_blocks, tq), 1)
    own_block = q_tile * (tq // MOBA_BLOCK) + (qry >> int(math.log2(MOBA_BLOCK)))
    kmean = kmean_ref[0]
    km_head = lax.broadcasted_iota(I32, kmean.shape, 1) >> 6
    pad = jnp.zeros((LANES - n_blocks, tq), BF16)
    lane_k = lax.broadcasted_iota(I32, (ATTN_ROWS, LANES), 1)
    qs = []
    for head in range(HEADS_PER_VREG):
        qh = jnp.where((feat >> 6) == head, q_t, jnp.zeros_like(q_t))
        km = jnp.where(km_head == head, kmean, 0.0)
        km_hi = km.astype(BF16)
        km_lo = (km - km_hi.astype(F32)).astype(BF16)
        gate = _dot(km_hi, qh) + _dot(km_lo, qh)
        gate = jnp.where(blk < own_block, gate, NEG_INF)
        bias = jnp.where(blk == own_block, 0.0, NEG_INF)
        for _ in range(min(MOBA_TOPK, n_blocks)):
            top = jnp.max(gate, axis=0, keepdims=True)
            first = jnp.min(jnp.where(gate == top, blk, n_blocks), axis=0, keepdims=True)
            picked = blk == first
            bias = jnp.where(picked & (blk < own_block), 0.0, bias)
            gate = jnp.where(picked, -jnp.inf, gate)
        qs.append(jnp.concatenate([qh, bias.astype(BF16), pad], axis=0))

    def rhs_of_tile(j, keys):
        return jnp.concatenate([keys, jnp.where(lane_k == j, 1.0, 0.0).astype(BF16)], axis=1)

    v_rows = [slice(h * HEAD_DIM, (h + 1) * HEAD_DIM) for h in range(HEADS_PER_VREG)]
    stats = _attend(qs, rhs_of_tile, k_ref, v_ref, v_rows, q_tile, scratch)
    out_t = jnp.concatenate([acc / l for l, acc in stats], axis=0)
    _store_position_tiles(o_ref, out_t.astype(BF16))


def _moba(qm, km, vm_t, kmean, batch, seq):
    t = batch * seq
    tq = ATTN_Q_ROWS
    nq = seq // tq
    n_blocks = seq // MOBA_BLOCK
    pairs = MOBA_HEADS // HEADS_PER_VREG
    n_kv = seq // ATTN_ROWS
    return pl.pallas_call(
        functools.partial(_moba_kernel, n_blocks=n_blocks),
        grid=(batch, pairs, nq),
        in_specs=[pl.BlockSpec((tq // ATTN_ROWS, LANES, ATTN_ROWS), lambda b, p, i: (b * nq + i, p, 0)),
                  pl.BlockSpec((seq, LANES), lambda b, p, i: (b, p)),
                  pl.BlockSpec((n_kv, LANES, ATTN_ROWS), lambda b, p, i: (b, p, 0)),
                  pl.BlockSpec((1, n_blocks, LANES), lambda b, p, i: (b, 0, p))],
        out_specs=pl.BlockSpec((tq // ATTN_ROWS, LANES, ATTN_ROWS), lambda b, p, i: (b * nq + i, p, 0)),
        out_shape=jax.ShapeDtypeStruct((t // ATTN_ROWS, SEG, ATTN_ROWS), BF16),
        scratch_shapes=_softmax_scratch(HEADS_PER_VREG, HEAD_DIM),
        compiler_params=_params(3),
        name="moba",
    )(qm, km, vm_t, kmean)


def _diff_kernel(q_ref, k_ref, v_ref, lq1_ref, lk1_ref, lq2_ref, lk2_ref, g_ref, o_ref,
                 *scratch, lam_init):
    tq = ATTN_Q_ROWS
    q_t = jnp.concatenate([q_ref[b] for b in range(q_ref.shape[0])], axis=1)
    comp = lax.broadcasted_iota(I32, (LANES, tq), 0) >> 6
    qs = [jnp.where(comp == c, q_t, jnp.zeros_like(q_t)) for c in range(2)]
    v_rows = [slice(0, LANES)] * 2
    (l1, a1), (l2, a2) = _attend(qs, lambda j, keys: keys, k_ref, v_ref, v_rows,
                                 pl.program_id(2), scratch)
    lam = (jnp.exp(jnp.sum(lq1_ref[...] * lk1_ref[...], axis=1, keepdims=True))
           - jnp.exp(jnp.sum(lq2_ref[...] * lk2_ref[...], axis=1, keepdims=True)) + lam_init)
    o = a1 / l1 - lam * (a2 / l2)
    o = o * lax.rsqrt(jnp.mean(o * o, axis=0, keepdims=True) + NORM_EPS) * g_ref[...]
    _store_position_tiles(o_ref, (o * (1.0 - lam_init)).astype(BF16))


def _diff(qd, kd, vd_t, lq1, lk1, lq2, lk2, g, batch, seq, lam_init):
    t = batch * seq
    tq = ATTN_Q_ROWS
    nq = seq // tq
    n_kv = seq // ATTN_ROWS
    vec = lambda a: pl.BlockSpec(a.shape, lambda b, h, i: (0, 0))
    return pl.pallas_call(
        functools.partial(_diff_kernel, lam_init=lam_init),
        grid=(batch, DIFF_HEADS, nq),
        in_specs=[pl.BlockSpec((tq // ATTN_ROWS, LANES, ATTN_ROWS), lambda b, h, i: (b * nq + i, h, 0)),
                  pl.BlockSpec((seq, LANES), lambda b, h, i: (b, h)),
                  pl.BlockSpec((n_kv, LANES, ATTN_ROWS), lambda b, h, i: (b, h, 0)),
                  vec(lq1), vec(lk1), vec(lq2), vec(lk2), vec(g)],
        out_specs=pl.BlockSpec((tq // ATTN_ROWS, LANES, ATTN_ROWS), lambda b, h, i: (b * nq + i, h, 0)),
        out_shape=jax.ShapeDtypeStruct((t // ATTN_ROWS, SEG, ATTN_ROWS), BF16),
        scratch_shapes=_softmax_scratch(2, LANES),
        compiler_params=_params(3),
        name="diff",
    )(qd, kd, vd_t, lq1, lk1, lq2, lk2, g)


def _merge_kernel(x_ref, om_ref, od_ref, mod_ref, g1n_ref, g2n_ref, wg_ref, wpm_ref, wpd_ref,
                  wout_ref, wr_ref, br_ref, x1_ref, h2_ref, eid_ref, ew_ref, counts_ref, merged, counts,
                  *, d):
    x = x_ref[...]
    sh1 = mod_ref[0, :, 0:d]
    sc1 = mod_ref[0, :, d:2 * d]
    g1 = mod_ref[0, :, 2 * d:3 * d]
    sh2 = mod_ref[0, :, 3 * d:4 * d]
    sc2 = mod_ref[0, :, 4 * d:5 * d]
    h = (_rms(x, g1n_ref[...]) * (1.0 + sc1) + sh1).astype(BF16)
    om, od = (jnp.concatenate([ref[b].astype(F32).T for b in range(ref.shape[0])], axis=0).astype(BF16)
              for ref in (om_ref, od_ref))
    chunk = SEG
    for c in range(d // chunk):
        cols = slice(c * chunk, (c + 1) * chunk)
        gm = _dot(h, wg_ref[:, c * chunk:(c + 1) * chunk])
        gd = _dot(h, wg_ref[:, d + c * chunk:d + (c + 1) * chunk])
        ym = _dot(om, wpm_ref[:, cols])
        yd = _dot(od, wpd_ref[:, cols])
        merged[:, cols] = (jax.nn.sigmoid(gm) * ym + jax.nn.sigmoid(gd) * yd).astype(BF16)
    x1 = x + g1 * _dot(merged[...], wout_ref[...])
    x1_ref[...] = x1
    h2 = _rms(x1, g2n_ref[...]) * (1.0 + sc2) + sh2
    _store_token_tiles(h2_ref, h2)

    h2_hi = h2.astype(BF16)
    h2_lo = (h2 - h2_hi.astype(F32)).astype(BF16)
    wr = wr_ref[...]
    wr_hi = wr.astype(BF16)
    wr_lo = (wr - wr_hi.astype(F32)).astype(BF16)
    logits = _dot(h2_hi, wr_hi) + _dot(h2_lo, wr_hi) + _dot(h2_hi, wr_lo) + br_ref[...]
    lane = lax.broadcasted_iota(I32, logits.shape, 1)
    is_group = lane < N_GROUPS
    gl = jnp.where(is_group, logits, -jnp.inf)
    gmax = jnp.max(gl, axis=1, keepdims=True)
    gexp = jnp.exp(gl - gmax)
    g_w = 1.0 / jnp.sum(gexp, axis=1, keepdims=True)
    g_idx = jnp.min(jnp.where(gl == gmax, lane, LANES), axis=1, keepdims=True)
    e_lane = lane - N_GROUPS
    in_group = (e_lane >= g_idx * EXPERTS_PER_GROUP) & (e_lane < (g_idx + 1) * EXPERTS_PER_GROUP)
    el = jnp.where(in_group, logits, -jnp.inf)
    emax = jnp.max(el, axis=1, keepdims=True)
    eexp = jnp.exp(el - emax)
    prob = eexp / jnp.sum(eexp, axis=1, keepdims=True)
    prob = jnp.where(in_group, prob, -1.0)
    p1 = jnp.max(prob, axis=1, keepdims=True)
    i1 = jnp.min(jnp.where(prob == p1, lane, LANES), axis=1, keepdims=True)
    prob2 = jnp.where(lane == i1, -1.0, prob)
    p2 = jnp.max(prob2, axis=1, keepdims=True)
    i2 = jnp.min(jnp.where(prob2 == p2, lane, LANES), axis=1, keepdims=True)
    tot = p1 + p2
    eid = jnp.where(lane == 0, i1 - N_GROUPS, jnp.where(lane == 1, i2 - N_GROUPS, 0))
    eid_ref[...] = eid.T[0:8, :]

    @pl.when(pl.program_id(0) == 0)
    def _():
        counts[...] = jnp.zeros_like(counts)

    chosen = jnp.where((lane == i1) | (lane == i2), 1.0, 0.0)
    per_lane = jnp.broadcast_to(jnp.sum(chosen, axis=0, keepdims=True), counts.shape)
    counts[...] = counts[...] + pltpu.roll(per_lane, LANES - N_GROUPS, 1)
    counts_ref[...] = counts[...].astype(I32)
    ew_ref[...] = jnp.where(lane == 0, g_w * (p1 / tot), jnp.where(lane == 1, g_w * (p2 / tot), 0.0))


def _merge(x2, om, od, mod3, g1n, g2n, wg, wpm, wpd, wout, wr, br, seq):
    t, d = x2.shape
    tm = MERGE_ROWS
    per_batch = seq // tm
    full = lambda a: pl.BlockSpec(a.shape, lambda i: (0,) * a.ndim)
    row = lambda w: pl.BlockSpec((tm, w), lambda i: (i, 0))
    tiles_t = pl.BlockSpec((tm // ATTN_ROWS, SEG, ATTN_ROWS), lambda i: (i, 0, 0))
    return pl.pallas_call(
        functools.partial(_merge_kernel, d=d),
        grid=(t // tm,),
        in_specs=[row(d), tiles_t, tiles_t,
                  pl.BlockSpec((1, 1, mod3.shape[2]), lambda i: (i // per_batch, 0, 0)),
                  full(g1n), full(g2n), full(wg), full(wpm), full(wpd), full(wout), full(wr), full(br)],
        out_specs=[row(d), pl.BlockSpec((tm * d // LANES, LANES), lambda i: (i, 0)),
                   pl.BlockSpec((8, tm), lambda i: (0, i)), row(LANES),
                   pl.BlockSpec((8, LANES), lambda i: (0, 0))],
        out_shape=[jax.ShapeDtypeStruct((t, d), F32), jax.ShapeDtypeStruct((t * d // LANES, LANES), F32),
                   jax.ShapeDtypeStruct((8, t), I32), jax.ShapeDtypeStruct((t, LANES), F32),
                   jax.ShapeDtypeStruct((8, LANES), I32)],
        scratch_shapes=[pltpu.VMEM((tm, d), BF16), pltpu.VMEM((8, LANES), F32)],
        compiler_params=_params(1),
        name="merge",
    )(x2, om, od, mod3, g1n, g2n, wg, wpm, wpd, wout, wr, br)


def _segment_ends(counts):
    lane = lax.broadcasted_iota(I32, counts.shape, 1)
    padded = (counts + (EXPERT_ROWS - 1)) & (-EXPERT_ROWS)
    padded = jnp.where(lane < N_EXPERTS, padded, 0)
    ends = padded
    shift = 1
    while shift < N_EXPERTS:
        ends = ends + jnp.where(lane >= shift, pltpu.roll(ends, shift, 1), 0)
        shift *= 2
    return padded, ends


def _pos_kernel(eid_ref, counts_ref, pos_ref, tile_ref, seg_ref, carry):
    tm = eid_ref.shape[1]

    @pl.when(pl.program_id(0) == 0)
    def _():
        carry[...] = jnp.zeros_like(carry)

    padded, ends = _segment_ends(counts_ref[...])
    starts = (ends - padded).astype(F32).T[0:N_EXPERTS, 0:1]
    eid = eid_ref[...]
    expert = lax.broadcasted_iota(I32, (N_EXPERTS, tm), 0)
    oh0 = expert == eid[0:1, :]
    oh1 = expert == eid[1:2, :]
    used = jnp.where(oh0 | oh1, 1.0, 0.0)
    r = lax.broadcasted_iota(I32, (tm, tm), 0)
    c = lax.broadcasted_iota(I32, (tm, tm), 1)
    earlier = jnp.where(r < c, 1.0, 0.0).astype(BF16)
    base = starts + carry[:, 0:1] + _dot(used.astype(BF16), earlier)
    p0 = jnp.sum(jnp.where(oh0, base, 0.0), axis=0, keepdims=True)
    p1 = jnp.sum(jnp.where(oh1, base, 0.0), axis=0, keepdims=True)
    slot = lax.broadcasted_iota(I32, pos_ref.shape, 0)
    pos_ref[...] = jnp.where(slot == 0, p0, jnp.where(slot == 1, p1, 0.0)).astype(I32)
    carry[...] = carry[...] + jnp.sum(used, axis=1, keepdims=True)

    @pl.when(pl.program_id(0) == 0)
    def _():
        n_tiles = tile_ref.shape[0]
        first_row = lax.broadcasted_iota(I32, (n_tiles, LANES), 0) * EXPERT_ROWS
        elane = lax.broadcasted_iota(I32, (n_tiles, LANES), 1)
        done = jnp.where((ends[0:1, :] <= first_row) & (elane < N_EXPERTS), 1, 0)
        expert = jnp.minimum(jnp.sum(done, axis=1, keepdims=True), N_EXPERTS - 1)
        total = jnp.max(ends[0:1, :], axis=1, keepdims=True)
        live = jnp.where(first_row < total, 1, 0)
        own_end = jnp.sum(jnp.where(elane == expert, ends[0:1, :], 0), axis=1, keepdims=True)
        upcoming = jnp.sum(jnp.where((ends[0:1, :] <= own_end) & (elane < N_EXPERTS), 1, 0),
                           axis=1, keepdims=True)
        upcoming = jnp.where(upcoming < N_EXPERTS, upcoming, -1)
        tile_ref[...] = jnp.where(elane == 0, expert,
                                  jnp.where(elane == 1, live, jnp.where(elane == 2, upcoming, 0)))
        seg_ref[...] = jnp.concatenate([ends, padded], axis=0)


def _pos(eid_t, counts, n_tiles):
    t = eid_t.shape[1]
    tm = SORT_ROWS
    col = pl.BlockSpec((8, tm), lambda i: (0, i))
    return pl.pallas_call(
        _pos_kernel,
        grid=(t // tm,),
        in_specs=[col, pl.BlockSpec((8, LANES), lambda i: (0, 0))],
        out_specs=[col, pl.BlockSpec((n_tiles, LANES), lambda i: (0, 0)),
                   pl.BlockSpec((16, LANES), lambda i: (0, 0))],
        out_shape=[jax.ShapeDtypeStruct((8, t), I32),
                   jax.ShapeDtypeStruct((n_tiles, LANES), I32),
                   jax.ShapeDtypeStruct((16, LANES), I32)],
        scratch_shapes=[pltpu.VMEM((N_EXPERTS, LANES), F32)],
        compiler_params=_params(1),
        name="pos",
    )(eid_t, counts)


def _tokens(ref, first, count, tile):
    start = first * tile
    if not isinstance(start, int):
        start = pl.multiple_of(start, tile)
    return ref.at[pl.ds(start, count * tile)]


def _dispatch_kernel(seg_end_ref, seg_len_ref, pos0_ref, pos1_ref, h2_ref, xs_ref, zeros, sem, *, tile):
    i = pl.program_id(0)
    pos_refs = (pos0_ref, pos1_ref)
    tm = MOVE_ROWS

    def clear_copy(first):
        return pltpu.make_async_copy(zeros, _tokens(xs_ref, first, EXPERT_ROWS, tile), sem)

    @pl.when(i == 0)
    def _():
        zeros[...] = jnp.zeros_like(zeros)
        for e in range(N_EXPERTS):
            @pl.when(seg_len_ref[e] > 0)
            def _():
                clear_copy(pl.multiple_of(seg_end_ref[e] - EXPERT_ROWS, EXPERT_ROWS)).start()
        for e in range(N_EXPERTS):
            @pl.when(seg_len_ref[e] > 0)
            def _():
                clear_copy(0).wait()

        first_unused = seg_end_ref[N_EXPERTS - 1] // EXPERT_ROWS
        n_tiles = xs_ref.shape[0] // (EXPERT_ROWS * tile)

        def clear(t, _):
            clear_copy(pl.multiple_of(t * EXPERT_ROWS, EXPERT_ROWS)).start()
            return 0

        def clear_done(t, _):
            clear_copy(0).wait()
            return 0

        lax.fori_loop(first_unused, n_tiles, clear, 0)
        lax.fori_loop(first_unused, n_tiles, clear_done, 0)

    for r in range(tm):
        for k in range(2):
            pltpu.make_async_copy(_tokens(h2_ref, r, 1, tile), _tokens(xs_ref, pos_refs[k][r], 1, tile),
                                  sem).start(priority=k)

    def drain(r, _):
        pltpu.make_async_copy(_tokens(h2_ref, 0, 1, tile), _tokens(xs_ref, 0, 1, tile), sem).wait()
        return 0

    lax.fori_loop(0, 2 * tm, drain, 0, unroll=8)


def _dispatch(seg_end, seg_len, pos_flat, h2_tiles, n_rows, tile):
    t = h2_tiles.shape[0] // tile
    tm = MOVE_ROWS
    n = t // tm
    grid_spec = pltpu.PrefetchScalarGridSpec(
        num_scalar_prefetch=2,
        grid=(n,),
        in_specs=[pl.BlockSpec((tm,), lambda i, *_: (i,), memory_space=pltpu.SMEM),
                  pl.BlockSpec((tm,), lambda i, *_: (n + i,), memory_space=pltpu.SMEM),
                  pl.BlockSpec((tm * tile, LANES), lambda i, *_: (i, 0))],
        out_specs=pl.BlockSpec(memory_space=pl.ANY),
        scratch_shapes=[pltpu.VMEM((EXPERT_ROWS * tile, LANES), F32), pltpu.SemaphoreType.DMA(())],
    )
    return pl.pallas_call(
        functools.partial(_dispatch_kernel, tile=tile),
        grid_spec=grid_spec,
        out_shape=jax.ShapeDtypeStruct((n_rows * tile, LANES), F32),
        compiler_params=pltpu.CompilerParams(dimension_semantics=("arbitrary",),
                                             vmem_limit_bytes=VMEM_LIMIT),
        name="dispatch",
    )(seg_end, seg_len, pos_flat, pos_flat, h2_tiles)


def _expert_kernel(tile_expert_ref, tile_live_ref, tile_next_ref, xs_hbm, wg_hbm, wu_hbm, wd_hbm, o_ref,
                   wg, wu, wd, xbuf, xsem, wbuf_g, wbuf_u, wbuf_d, wsem, wslot):
    i = pl.program_id(0)
    n = pl.num_programs(0)
    rows = xbuf.shape[1]

    def weight_copies(expert, slot):
        return [pltpu.make_async_copy(src.at[expert], dst.at[slot], wsem.at[slot, k])
                for k, (src, dst) in enumerate(((wg_hbm, wbuf_g), (wu_hbm, wbuf_u), (wd_hbm, wbuf_d)))]

    def fetch(step):
        slot = step % EXPERT_RING
        start = pl.multiple_of(step * rows, rows)
        return pltpu.make_async_copy(xs_hbm.at[pl.ds(start, rows)], xbuf.at[slot], xsem.at[slot])

    @pl.when(i == 0)
    def _():
        for step in range(EXPERT_RING - 1):
            fetch(step).start()

    @pl.when(i + EXPERT_RING - 1 < n)
    def _():
        fetch(i + EXPERT_RING - 1).start()

    fetch(i).wait()
    xs_ref = xbuf.at[i % EXPERT_RING]

    @pl.when(tile_live_ref[i] > 0)
    def _():
        changed = jnp.logical_or(i == 0, tile_expert_ref[i] != tile_expert_ref[jnp.maximum(i - 1, 0)])

        @pl.when(changed)
        def _():
            @pl.when(i == 0)
            def _():
                wslot[0] = 0
                for copy in weight_copies(tile_expert_ref[0], 0):
                    copy.start()

            slot = wslot[0]
            for copy in weight_copies(tile_expert_ref[i], slot):
                copy.wait()
            wg[...] = wbuf_g[slot].astype(BF16)
            wu[...] = wbuf_u[slot].astype(BF16)
            wd[...] = wbuf_d[slot].astype(BF16)
            upcoming = tile_next_ref[i]

            @pl.when(upcoming >= 0)
            def _():
                for copy in weight_copies(upcoming, 1 - slot):
                    copy.start()

            wslot[0] = 1 - slot

        x = _load_token_tiles(xs_ref, EXPERT_ROWS).astype(BF16)
        de = wg.shape[1]
        chunk = 2 * LANES
        pre = [(_dot(x, wg[:, c:c + chunk]), _dot(x, wu[:, c:c + chunk])) for c in range(0, de, chunk)]
        hid = [((gate * jax.nn.sigmoid(gate)) * up).astype(BF16) for gate, up in pre]
        tile = o_ref.shape[0] // EXPERT_ROWS
        for oc in range(0, wd.shape[1], chunk):
            out = None
            for n, h in enumerate(hid):
                part = _dot(h, wd[n * chunk:(n + 1) * chunk, oc:oc + chunk])
                out = part if out is None else out + part
            for c in range(chunk // LANES):
                o_ref[pl.ds(oc // LANES + c, EXPERT_ROWS, stride=tile), :] = out[:, c * LANES:(c + 1) * LANES]

    @pl.when(tile_live_ref[i] == 0)
    def _():
        o_ref[...] = jnp.zeros_like(o_ref)


def _experts(tile_expert, tile_live, tile_next, xs_tiles, w_gate, w_up, w_down):
    d, de = w_gate.shape[1:]
    tile = d // LANES
    n_rows = xs_tiles.shape[0] // tile
    tm = EXPERT_ROWS
    anywhere = pl.BlockSpec(memory_space=pl.ANY)
    grid_spec = pltpu.PrefetchScalarGridSpec(
        num_scalar_prefetch=3,
        grid=(n_rows // tm,),
        in_specs=[anywhere, anywhere, anywhere, anywhere],
        out_specs=pl.BlockSpec((tm * tile, LANES), lambda i, *_: (i, 0)),
        scratch_shapes=[pltpu.VMEM((d, de), BF16), pltpu.VMEM((d, de), BF16), pltpu.VMEM((de, d), BF16),
                        pltpu.VMEM((EXPERT_RING, tm * tile, LANES), F32),
                        pltpu.SemaphoreType.DMA((EXPERT_RING,)),
                        pltpu.VMEM((2, d, de), F32), pltpu.VMEM((2, d, de), F32), pltpu.VMEM((2, de, d), F32),
                        pltpu.SemaphoreType.DMA((2, 3)), pltpu.SMEM((1,), I32)],
    )
    return pl.pallas_call(
        _expert_kernel,
        grid_spec=grid_spec,
        out_shape=jax.ShapeDtypeStruct(xs_tiles.shape, F32),
        compiler_params=_params(1),
        name="experts",
    )(tile_expert, tile_live, tile_next, xs_tiles, w_gate, w_up, w_down)


def _combine_kernel(pos0_ref, pos1_ref, pos0_next_ref, pos1_next_ref, ys_ref, x1_ref, ew_ref, mod_ref, g_ref,
                    o_ref, buf, sem, *, d, final_norm):
    i = pl.program_id(0)
    n = pl.num_programs(0)
    tm = MOVE_ROWS
    tile = d // LANES

    def fetch(prefs, slot):
        for r in range(tm):
            for k in range(2):
                pltpu.make_async_copy(_tokens(ys_ref, prefs[k][r], 1, tile),
                                      _tokens(buf.at[slot, k], r, 1, tile),
                                      sem.at[slot]).start(priority=k)

    @pl.when(i == 0)
    def _():
        fetch((pos0_ref, pos1_ref), 0)

    for parity in range(2):
        @pl.when((i + 1 < n) & ((i + 1) % 2 == parity))
        def _():
            fetch((pos0_next_ref, pos1_next_ref), parity)

    slot = i % 2

    def drain(r, _):
        pltpu.make_async_copy(_tokens(ys_ref, 0, 1, tile), _tokens(buf.at[slot, 0], 0, 1, tile),
                              sem.at[slot]).wait()
        return 0

    lax.fori_loop(0, 2 * tm, drain, 0, unroll=8)

    ew = ew_ref[...]
    y = (ew[:, 0:1] * _load_token_tiles(buf.at[slot, 0], tm)
         + ew[:, 1:2] * _load_token_tiles(buf.at[slot, 1], tm))
    g2 = mod_ref[0, :, 5 * d:6 * d]
    x2 = x1_ref[...] + g2 * y
    o_ref[...] = _rms(x2, g_ref[...]) if final_norm else x2


def _combine(pos_flat, ys, x1, ew, mod3, final_g, seq, final_norm):
    t, d = x1.shape
    tm = MOVE_ROWS
    n = t // tm
    per_batch = seq // tm
    return pl.pallas_call(
        functools.partial(_combine_kernel, d=d, final_norm=final_norm),
        grid=(n,),
        in_specs=[pl.BlockSpec((tm,), lambda i: (i,), memory_space=pltpu.SMEM),
                  pl.BlockSpec((tm,), lambda i: (n + i,), memory_space=pltpu.SMEM),
                  pl.BlockSpec((tm,), lambda i: (jnp.minimum(i + 1, n - 1),), memory_space=pltpu.SMEM),
                  pl.BlockSpec((tm,), lambda i: (n + jnp.minimum(i + 1, n - 1),), memory_space=pltpu.SMEM),
                  pl.BlockSpec(memory_space=pl.ANY),
                  pl.BlockSpec((tm, d), lambda i: (i, 0)),
                  pl.BlockSpec((tm, LANES), lambda i: (i, 0)),
                  pl.BlockSpec((1, 1, mod3.shape[2]), lambda i: (i // per_batch, 0, 0)),
                  pl.BlockSpec((1, d), lambda i: (0, 0))],
        out_specs=pl.BlockSpec((tm, d), lambda i: (i, 0)),
        out_shape=jax.ShapeDtypeStruct((t, d), F32),
        scratch_shapes=[pltpu.VMEM((2, 2, tm * d // LANES, LANES), F32), pltpu.SemaphoreType.DMA((2,))],
        compiler_params=_params(1),
        name="combine",
    )(pos_flat, pos_flat, pos_flat, pos_flat, ys, x1, ew, mod3, final_g)


def _rope_tables(seq):
    inv = 1.0 / (ROPE_THETA ** (jnp.arange(0, HEAD_DIM, 2, dtype=F32) / HEAD_DIM))
    ang = jnp.arange(seq, dtype=F32)[:, None] * inv[None, :]
    cos, sin = jnp.cos(ang), jnp.sin(ang)
    cos_head = jnp.concatenate([cos, cos], axis=1)
    sin_head = jnp.concatenate([-sin, sin], axis=1)
    reps = LANES // HEAD_DIM
    return jnp.tile(cos_head, (1, reps)), jnp.tile(sin_head, (1, reps))


def kernel(x, c, w_ada, b_ada, norm1_g, w_in, lambda_q1, lambda_k1, lambda_q2, lambda_k2,
           diff_subln_g, w_proj_moba, w_proj_diff, w_out, norm2_g, w_group, b_group,
           w_expert, b_expert, w_gate, w_up, w_down, final_g):
    batch, seq, d = x.shape
    depth = w_ada.shape[0]
    t = batch * seq
    assert seq % PROJ_ROWS == 0 and seq % MOBA_BLOCK == 0 and seq // MOBA_BLOCK <= LANES
    assert ATTN_ROWS == MOBA_BLOCK and d % SEG == 0 and t % SORT_ROWS == 0 and batch <= 8
    assert seq % ATTN_Q_ROWS == 0 and ATTN_Q_ROWS % ATTN_ROWS == 0
    assert EXPERT_ROWS & (EXPERT_ROWS - 1) == 0
    n_rows = 2 * t + N_EXPERTS * EXPERT_ROWS
    n_tiles = n_rows // EXPERT_ROWS
    cos, sin = _rope_tables(seq)
    c_pad = jnp.zeros((8, d), F32).at[:batch].set(c)
    xf = x.reshape(t, d)
    row = lambda v: v.reshape(1, -1)
    for l in range(depth):
        mod = _ada(c_pad, w_ada[l], row(b_ada[l]))
        mod3 = mod[:batch].reshape(batch, 1, 6 * d)
        w_qkv = w_in[l][:, :N_QKV_SEGS * SEG].astype(BF16)
        w_gates = w_in[l][:, N_QKV_SEGS * SEG:].astype(BF16)
        qm, km, vm, qd, kd, vd, kmean = _proj(xf, mod3, row(norm1_g[l]), w_qkv, cos, sin, seq)
        kmean = kmean.reshape(batch, seq // MOBA_BLOCK, SEG)
        om = _moba(qm, km, vm, kmean, batch, seq)
        lam_init = 0.8 - 0.6 * math.exp(-0.3 * l)
        od = _diff(qd, kd, vd, row(lambda_q1[l]), row(lambda_k1[l]), row(lambda_q2[l]),
                   row(lambda_k2[l]), diff_subln_g[l].reshape(-1, 1), batch, seq, lam_init)
        w_router = jnp.zeros((d, LANES), F32)
        w_router = w_router.at[:, :N_GROUPS].set(w_group[l])
        w_router = w_router.at[:, N_GROUPS:N_GROUPS + N_EXPERTS].set(w_expert[l])
        b_router = jnp.zeros((1, LANES), F32)
        b_router = b_router.at[0, :N_GROUPS].set(b_group[l])
        b_router = b_router.at[0, N_GROUPS:N_GROUPS + N_EXPERTS].set(b_expert[l])
        x1, h2, eid_t, ew, counts = _merge(xf, om, od, mod3, row(norm1_g[l]), row(norm2_g[l]), w_gates,
                                 w_proj_moba[l].astype(BF16), w_proj_diff[l].astype(BF16),
                                 w_out[l].astype(BF16), w_router, b_router, seq)
        pos_t, tiles, segs = _pos(eid_t, counts, n_tiles)
        pos_flat = pos_t[:2].reshape(2 * t)
        xs = _dispatch(segs[0, :N_EXPERTS], segs[8, :N_EXPERTS], pos_flat, h2, n_rows, d // LANES)
        ys = _experts(tiles[:, 0], tiles[:, 1], tiles[:, 2], xs, w_gate[l], w_up[l], w_down[l])
        xf = _combine(pos_flat, ys, x1, ew, mod3, row(final_g), seq, final_norm=(l == depth - 1))
    return xf.reshape(batch, seq, d)
```

```python
import functools
import math

import jax
import jax.numpy as jnp
from jax import lax
from jax.experimental import pallas as pl
from jax.experimental.pallas import tpu as pltpu

F32 = jnp.float32
BF16 = jnp.bfloat16
I32 = jnp.int32

LANES = 128
HEAD_DIM = 64
HEADS_PER_VREG = LANES // HEAD_DIM
MOBA_HEADS = 8
MOBA_BLOCK = 256
MOBA_TOPK = 3
DIFF_HEADS = 4
ROPE_THETA = 10000.0
N_GROUPS = 4
EXPERTS_PER_GROUP = 8
N_EXPERTS = N_GROUPS * EXPERTS_PER_GROUP
NORM_EPS = 1e-6
NEG_INF = -1e30
SEG = MOBA_HEADS * HEAD_DIM
N_QKV_SEGS = 6
SUM_ROWS = 16

PROJ_ROWS = 512
ATTN_ROWS = 256
ATTN_Q_ROWS = 1024
MERGE_ROWS = 512
SORT_ROWS = 512
EXPERT_ROWS = 512
EXPERT_RING = 3
MOVE_ROWS = 512
VMEM_LIMIT = 56 * 1024 * 1024


def _params(n_axes, vmem=VMEM_LIMIT):
    return pltpu.CompilerParams(dimension_semantics=("arbitrary",) * n_axes,
                                vmem_limit_bytes=vmem)


def _dot(a, b):
    return jnp.dot(a, b, preferred_element_type=F32)


def _dot_nt(a, b):
    return lax.dot_general(a, b, (((1,), (1,)), ((), ())), preferred_element_type=F32)


def _store_token_tiles(ref, x):
    n, d = x.shape
    chunks = d // LANES
    for c in range(chunks):
        ref[pl.ds(c, n, stride=chunks), :] = x[:, c * LANES:(c + 1) * LANES]


def _load_token_tiles(ref, n):
    chunks = ref.shape[0] // n
    return jnp.concatenate([ref[pl.ds(c, n, stride=chunks), :] for c in range(chunks)], axis=1)


def _rms(x, g):
    return x * lax.rsqrt(jnp.mean(x * x, axis=-1, keepdims=True) + NORM_EPS) * g


def _ada_kernel(c_ref, w_ref, b_ref, o_ref):
    c = c_ref[...]
    o_ref[...] = _dot(c * jax.nn.sigmoid(c), w_ref[...]) + b_ref[...]


def _ada(c_pad, w, b):
    rows, d = c_pad.shape
    n = w.shape[1]
    tn = 1536
    return pl.pallas_call(
        _ada_kernel,
        grid=(n // tn,),
        in_specs=[pl.BlockSpec((rows, d), lambda j: (0, 0)),
                  pl.BlockSpec((d, tn), lambda j: (0, j)),
                  pl.BlockSpec((1, tn), lambda j: (0, j))],
        out_specs=pl.BlockSpec((rows, tn), lambda j: (0, j)),
        out_shape=jax.ShapeDtypeStruct((rows, n), F32),
        compiler_params=_params(1),
        name="ada",
    )(c_pad, w, b)


def _rope(x, cos, sin_signed):
    half = HEAD_DIM // 2
    width = x.shape[1]
    lane = lax.broadcasted_iota(I32, x.shape, 1)
    first = (lane & (HEAD_DIM - 1)) < half
    partner = jnp.where(first, pltpu.roll(x, width - half, 1), pltpu.roll(x, half, 1))
    return x * cos + partner * sin_signed


def _proj_kernel(x_ref, mod_ref, g_ref, w_ref, cos_ref, sin_ref,
                 qm_ref, km_ref, vm_ref, qd_ref, kd_ref, vd_ref, kmean_ref, *, d):
    x = x_ref[...]
    sh = mod_ref[0, :, 0:d]
    sc = mod_ref[0, :, d:2 * d]
    h = (_rms(x, g_ref[...]) * (1.0 + sc) + sh).astype(BF16)
    cos = jnp.concatenate([cos_ref[...]] * (SEG // LANES), axis=1)
    sin = jnp.concatenate([sin_ref[...]] * (SEG // LANES), axis=1)
    scale = HEAD_DIM ** -0.5 * math.log2(math.e)
    outs = (qm_ref, km_ref, vm_ref, qd_ref, kd_ref, vd_ref)
    n_blk = x.shape[0] // ATTN_ROWS
    for seg, o_ref in enumerate(outs):
        y = _dot(h, w_ref[:, seg * SEG:(seg + 1) * SEG])
        if seg in (0, 1, 3, 4):
            y = _rope(y, cos, sin)
        if seg in (0, 3):
            y = y * scale
        if seg == 1:
            for blk in range(n_blk):
                rows = y[blk * MOBA_BLOCK:(blk + 1) * MOBA_BLOCK]
                kmean_ref[0, blk:blk + 1, :] = jnp.mean(rows, axis=0, keepdims=True)
        if seg in (0, 2, 3, 5):
            for blk in range(n_blk):
                for part in range(SEG // LANES):
                    piece = y[blk * ATTN_ROWS:(blk + 1) * ATTN_ROWS, part * LANES:(part + 1) * LANES]
                    o_ref[blk, part * LANES:(part + 1) * LANES, :] = piece.T.astype(BF16)
        else:
            o_ref[...] = y.astype(BF16)


def _proj(x2, mod3, g, w_qkv, cos, sin, seq):
    t, d = x2.shape
    tm = PROJ_ROWS
    per_batch = seq // tm
    row_spec = pl.BlockSpec((tm, SEG), lambda i: (i, 0))
    tab_spec = pl.BlockSpec((tm, LANES), lambda i: (i % per_batch, 0))
    act = jax.ShapeDtypeStruct((t, SEG), BF16)
    act_t = jax.ShapeDtypeStruct((t // ATTN_ROWS, SEG, ATTN_ROWS), BF16)
    t_spec = pl.BlockSpec((tm // ATTN_ROWS, SEG, ATTN_ROWS), lambda i: (i, 0, 0))
    return pl.pallas_call(
        functools.partial(_proj_kernel, d=d),
        grid=(t // tm,),
        in_specs=[pl.BlockSpec((tm, d), lambda i: (i, 0)),
                  pl.BlockSpec((1, 1, mod3.shape[2]), lambda i: (i // per_batch, 0, 0)),
                  pl.BlockSpec((1, d), lambda i: (0, 0)),
                  pl.BlockSpec(w_qkv.shape, lambda i: (0, 0)),
                  tab_spec, tab_spec],
        out_specs=[t_spec, row_spec, t_spec, t_spec, row_spec, t_spec,
                   pl.BlockSpec((1, tm // MOBA_BLOCK, SEG), lambda i: (i, 0, 0))],
        out_shape=[act_t, act, act_t, act_t, act, act_t,
                   jax.ShapeDtypeStruct((t // tm, tm // MOBA_BLOCK, SEG), F32)],
        compiler_params=_params(1),
        name="proj",
    )(x2, mod3, g, w_qkv, cos, sin)


def _attend(qs, keys_of_tile, k_ref, vt_ref, v_rows, q_tile, scratch):
    tq, tk = ATTN_Q_ROWS, ATTN_ROWS
    sub = tq // tk
    ones = jnp.ones((SUM_ROWS, tk), BF16)
    n_soft = len(qs)
    acc_refs, max_refs = scratch[:n_soft], scratch[n_soft:]
    for acc_ref in acc_refs:
        acc_ref[...] = jnp.zeros(acc_ref.shape, F32)

    def scores(j, first_query=0):
        start = pl.multiple_of(j * tk, tk)
        keys = keys_of_tile(j, k_ref[pl.ds(start, tk), :])
        return tuple(_dot(keys, q[:, first_query:]) for q in qs), vt_ref[j]

    def update(tile, mask, maxes, first_query=0):
        s_all, vt = tile
        new = ()
        for n, s in enumerate(s_all):
            if mask is not None:
                own = jnp.where(mask, s[:, :tk], NEG_INF)
                s = own if s.shape[1] == tk else jnp.concatenate([own, s[:, tk:]], axis=1)
            m = max_refs[n][:, first_query:] if maxes is None else maxes[n]
            m_new = jnp.maximum(m, jnp.max(s, axis=0, keepdims=True))
            alpha = jnp.exp2(m - m_new)
            p = jnp.exp2(s - m_new).astype(BF16)
            if maxes is None:
                max_refs[n][:, first_query:] = m_new
            new += (m_new,)
            vt_sum = jnp.concatenate([vt[v_rows[n]], ones], axis=0)
            acc_refs[n][:, first_query:] = alpha * acc_refs[n][:, first_query:] + _dot(vt_sum, p)
        return new

    first_own = q_tile * sub

    def body(group, maxes):
        for tile in [scores(sub * group + b) for b in range(sub)]:
            maxes = update(tile, None, maxes)
        return maxes

    maxes = lax.fori_loop(0, q_tile, body, (jnp.full((1, tq), NEG_INF, F32),) * n_soft)
    for max_ref, m in zip(max_refs, maxes):
        max_ref[...] = m
    own = [scores(first_own + b, b * tk) for b in range(sub)]
    causal = lax.broadcasted_iota(I32, (tk, tk), 0) <= lax.broadcasted_iota(I32, (tk, tk), 1)
    for b in range(sub):
        update(own[b], causal, None, b * tk)
    out = []
    for n, acc_ref in enumerate(acc_refs):
        rows = v_rows[n].stop - v_rows[n].start
        out.append((acc_ref[rows:rows + 1, :], acc_ref[0:rows, :]))
    return out


def _store_position_tiles(ref, x_t):
    for b in range(ref.shape[0]):
        ref[b] = x_t[:, b * ATTN_ROWS:(b + 1) * ATTN_ROWS]


def _softmax_scratch(n_softmax, features):
    return ([pltpu.VMEM((features + SUM_ROWS, ATTN_Q_ROWS), F32)] * n_softmax
            + [pltpu.VMEM((1, ATTN_Q_ROWS), F32)] * n_softmax)


def _moba_kernel(q_ref, k_ref, v_ref, kmean_ref, o_ref, *scratch, n_blocks):
    tq = ATTN_Q_ROWS
    q_tile = pl.program_id(2)
    q_t = jnp.concatenate([q_ref[b] for b in range(q_ref.shape[0])], axis=1)
    feat = lax.broadcasted_iota(I32, (LANES, tq), 0)
    blk = lax.broadcasted_iota(I32, (n_blocks, tq), 0)
    qry = lax.broadcasted_iota(I32, (n_blocks, tq), 1)
    own_block = q_tile * (tq // MOBA_BLOCK) + (qry >> int(math.log2(MOBA_BLOCK)))
    kmean = kmean_ref[0]
    km_head = lax.broadcasted_iota(I32, kmean.shape, 1) >> 6
    pad = jnp.zeros((LANES - n_blocks, tq), BF16)
    lane_k = lax.broadcasted_iota(I32, (ATTN_ROWS, LANES), 1)
    qs = []
    for head in range(HEADS_PER_VREG):
        qh = jnp.where((feat >> 6) == head, q_t, jnp.zeros_like(q_t))
        km = jnp.where(km_head == head, kmean, 0.0)
        km_hi = km.astype(BF16)
        km_lo = (km - km_hi.astype(F32)).astype(BF16)
        gate = _dot(km_hi, qh) + _dot(km_lo, qh)
        gate = jnp.where(blk < own_block, gate, NEG_INF)
        bias = jnp.where(blk == own_block, 0.0, NEG_INF)
        for _ in range(min(MOBA_TOPK, n_blocks)):
            top = jnp.max(gate, axis=0, keepdims=True)
            first = jnp.min(jnp.where(gate == top, blk, n_blocks), axis=0, keepdims=True)
            picked = blk == first
            bias = jnp.where(picked & (blk < own_block), 0.0, bias)
            gate = jnp.where(picked, -jnp.inf, gate)
        qs.append(jnp.concatenate([qh, bias.astype(BF16), pad], axis=0))

    def rhs_of_tile(j, keys):
        return jnp.concatenate([keys, jnp.where(lane_k == j, 1.0, 0.0).astype(BF16)], axis=1)

    v_rows = [slice(h * HEAD_DIM, (h + 1) * HEAD_DIM) for h in range(HEADS_PER_VREG)]
    stats = _attend(qs, rhs_of_tile, k_ref, v_ref, v_rows, q_tile, scratch)
    out_t = jnp.concatenate([acc / l for l, acc in stats], axis=0)
    _store_position_tiles(o_ref, out_t.astype(BF16))


def _moba(qm, km, vm_t, kmean, batch, seq):
    t = batch * seq
    tq = ATTN_Q_ROWS
    nq = seq // tq
    n_blocks = seq // MOBA_BLOCK
    pairs = MOBA_HEADS // HEADS_PER_VREG
    n_kv = seq // ATTN_ROWS
    return pl.pallas_call(
        functools.partial(_moba_kernel, n_blocks=n_blocks),
        grid=(batch, pairs, nq),
        in_specs=[pl.BlockSpec((tq // ATTN_ROWS, LANES, ATTN_ROWS), lambda b, p, i: (b * nq + i, p, 0)),
                  pl.BlockSpec((seq, LANES), lambda b, p, i: (b, p)),
                  pl.BlockSpec((n_kv, LANES, ATTN_ROWS), lambda b, p, i: (b, p, 0)),
                  pl.BlockSpec((1, n_blocks, LANES), lambda b, p, i: (b, 0, p))],
        out_specs=pl.BlockSpec((tq // ATTN_ROWS, LANES, ATTN_ROWS), lambda b, p, i: (b * nq + i, p, 0)),
        out_shape=jax.ShapeDtypeStruct((t // ATTN_ROWS, SEG, ATTN_ROWS), BF16),
        scratch_shapes=_softmax_scratch(HEADS_PER_VREG, HEAD_DIM),
        compiler_params=_params(3),
        name="moba",
    )(qm, km, vm_t, kmean)


def _diff_kernel(q_ref, k_ref, v_ref, lq1_ref, lk1_ref, lq2_ref, lk2_ref, g_ref, o_ref,
                 *scratch, lam_init):
    tq = ATTN_Q_ROWS
    q_t = jnp.concatenate([q_ref[b] for b in range(q_ref.shape[0])], axis=1)
    comp = lax.broadcasted_iota(I32, (LANES, tq), 0) >> 6
    qs = [jnp.where(comp == c, q_t, jnp.zeros_like(q_t)) for c in range(2)]
    v_rows = [slice(0, LANES)] * 2
    (l1, a1), (l2, a2) = _attend(qs, lambda j, keys: keys, k_ref, v_ref, v_rows,
                                 pl.program_id(2), scratch)
    lam = (jnp.exp(jnp.sum(lq1_ref[...] * lk1_ref[...], axis=1, keepdims=True))
           - jnp.exp(jnp.sum(lq2_ref[...] * lk2_ref[...], axis=1, keepdims=True)) + lam_init)
    o = a1 / l1 - lam * (a2 / l2)
    o = o * lax.rsqrt(jnp.mean(o * o, axis=0, keepdims=True) + NORM_EPS) * g_ref[...]
    _store_position_tiles(o_ref, (o * (1.0 - lam_init)).astype(BF16))


def _diff(qd, kd, vd_t, lq1, lk1, lq2, lk2, g, batch, seq, lam_init):
    t = batch * seq
    tq = ATTN_Q_ROWS
    nq = seq // tq
    n_kv = seq // ATTN_ROWS
    vec = lambda a: pl.BlockSpec(a.shape, lambda b, h, i: (0, 0))
    return pl.pallas_call(
        functools.partial(_diff_kernel, lam_init=lam_init),
        grid=(batch, DIFF_HEADS, nq),
        in_specs=[pl.BlockSpec((tq // ATTN_ROWS, LANES, ATTN_ROWS), lambda b, h, i: (b * nq + i, h, 0)),
                  pl.BlockSpec((seq, LANES), lambda b, h, i: (b, h)),
                  pl.BlockSpec((n_kv, LANES, ATTN_ROWS), lambda b, h, i: (b, h, 0)),
                  vec(lq1), vec(lk1), vec(lq2), vec(lk2), vec(g)],
        out_specs=pl.BlockSpec((tq // ATTN_ROWS, LANES, ATTN_ROWS), lambda b, h, i: (b * nq + i, h, 0)),
        out_shape=jax.ShapeDtypeStruct((t // ATTN_ROWS, SEG, ATTN_ROWS), BF16),
        scratch_shapes=_softmax_scratch(2, LANES),
        compiler_params=_params(3),
        name="diff",
    )(qd, kd, vd_t, lq1, lk1, lq2, lk2, g)


def _merge_kernel(x_ref, om_ref, od_ref, mod_ref, g1n_ref, g2n_ref, wg_ref, wpm_ref, wpd_ref,
                  wout_ref, wr_ref, br_ref, x1_ref, h2_ref, eid_ref, ew_ref, counts_ref, merged, counts,
                  *, d):
    x = x_ref[...]
    sh1 = mod_ref[0, :, 0:d]
    sc1 = mod_ref[0, :, d:2 * d]
    g1 = mod_ref[0, :, 2 * d:3 * d]
    sh2 = mod_ref[0, :, 3 * d:4 * d]
    sc2 = mod_ref[0, :, 4 * d:5 * d]
    h = (_rms(x, g1n_ref[...]) * (1.0 + sc1) + sh1).astype(BF16)
    om, od = (jnp.concatenate([ref[b].astype(F32).T for b in range(ref.shape[0])], axis=0).astype(BF16)
              for ref in (om_ref, od_ref))
    chunk = SEG
    for c in range(d // chunk):
        cols = slice(c * chunk, (c + 1) * chunk)
        gm = _dot(h, wg_ref[:, c * chunk:(c + 1) * chunk])
        gd = _dot(h, wg_ref[:, d + c * chunk:d + (c + 1) * chunk])
        ym = _dot(om, wpm_ref[:, cols])
        yd = _dot(od, wpd_ref[:, cols])
        merged[:, cols] = (jax.nn.sigmoid(gm) * ym + jax.nn.sigmoid(gd) * yd).astype(BF16)
    x1 = x + g1 * _dot(merged[...], wout_ref[...])
    x1_ref[...] = x1
    h2 = _rms(x1, g2n_ref[...]) * (1.0 + sc2) + sh2
    _store_token_tiles(h2_ref, h2)

    h2_hi = h2.astype(BF16)
    h2_lo = (h2 - h2_hi.astype(F32)).astype(BF16)
    wr = wr_ref[...]
    wr_hi = wr.astype(BF16)
    wr_lo = (wr - wr_hi.astype(F32)).astype(BF16)
    logits = _dot(h2_hi, wr_hi) + _dot(h2_lo, wr_hi) + _dot(h2_hi, wr_lo) + br_ref[...]
    lane = lax.broadcasted_iota(I32, logits.shape, 1)
    is_group = lane < N_GROUPS
    gl = jnp.where(is_group, logits, -jnp.inf)
    gmax = jnp.max(gl, axis=1, keepdims=True)
    gexp = jnp.exp(gl - gmax)
    g_w = 1.0 / jnp.sum(gexp, axis=1, keepdims=True)
    g_idx = jnp.min(jnp.where(gl == gmax, lane, LANES), axis=1, keepdims=True)
    e_lane = lane - N_GROUPS
    in_group = (e_lane >= g_idx * EXPERTS_PER_GROUP) & (e_lane < (g_idx + 1) * EXPERTS_PER_GROUP)
    el = jnp.where(in_group, logits, -jnp.inf)
    emax = jnp.max(el, axis=1, keepdims=True)
    eexp = jnp.exp(el - emax)
    prob = eexp / jnp.sum(eexp, axis=1, keepdims=True)
    prob = jnp.where(in_group, prob, -1.0)
    p1 = jnp.max(prob, axis=1, keepdims=True)
    i1 = jnp.min(jnp.where(prob == p1, lane, LANES), axis=1, keepdims=True)
    prob2 = jnp.where(lane == i1, -1.0, prob)
    p2 = jnp.max(prob2, axis=1, keepdims=True)
    i2 = jnp.min(jnp.where(prob2 == p2, lane, LANES), axis=1, keepdims=True)
    tot = p1 + p2
    eid = jnp.where(lane == 0, i1 - N_GROUPS, jnp.where(lane == 1, i2 - N_GROUPS, 0))
    eid_ref[...] = eid.T[0:8, :]

    @pl.when(pl.program_id(0) == 0)
    def _():
        counts[...] = jnp.zeros_like(counts)

    chosen = jnp.where((lane == i1) | (lane == i2), 1.0, 0.0)
    per_lane = jnp.broadcast_to(jnp.sum(chosen, axis=0, keepdims=True), counts.shape)
    counts[...] = counts[...] + pltpu.roll(per_lane, LANES - N_GROUPS, 1)
    counts_ref[...] = counts[...].astype(I32)
    ew_ref[...] = jnp.where(lane == 0, g_w * (p1 / tot), jnp.where(lane == 1, g_w * (p2 / tot), 0.0))


def _merge(x2, om, od, mod3, g1n, g2n, wg, wpm, wpd, wout, wr, br, seq):
    t, d = x2.shape
    tm = MERGE_ROWS
    per_batch = seq // tm
    full = lambda a: pl.BlockSpec(a.shape, lambda i: (0,) * a.ndim)
    row = lambda w: pl.BlockSpec((tm, w), lambda i: (i, 0))
    tiles_t = pl.BlockSpec((tm // ATTN_ROWS, SEG, ATTN_ROWS), lambda i: (i, 0, 0))
    return pl.pallas_call(
        functools.partial(_merge_kernel, d=d),
        grid=(t // tm,),
        in_specs=[row(d), tiles_t, tiles_t,
                  pl.BlockSpec((1, 1, mod3.shape[2]), lambda i: (i // per_batch, 0, 0)),
                  full(g1n), full(g2n), full(wg), full(wpm), full(wpd), full(wout), full(wr), full(br)],
        out_specs=[row(d), pl.BlockSpec((tm * d // LANES, LANES), lambda i: (i, 0)),
                   pl.BlockSpec((8, tm), lambda i: (0, i)), row(LANES),
                   pl.BlockSpec((8, LANES), lambda i: (0, 0))],
        out_shape=[jax.ShapeDtypeStruct((t, d), F32), jax.ShapeDtypeStruct((t * d // LANES, LANES), F32),
                   jax.ShapeDtypeStruct((8, t), I32), jax.ShapeDtypeStruct((t, LANES), F32),
                   jax.ShapeDtypeStruct((8, LANES), I32)],
        scratch_shapes=[pltpu.VMEM((tm, d), BF16), pltpu.VMEM((8, LANES), F32)],
        compiler_params=_params(1),
        name="merge",
    )(x2, om, od, mod3, g1n, g2n, wg, wpm, wpd, wout, wr, br)


def _segment_ends(counts):
    lane = lax.broadcasted_iota(I32, counts.shape, 1)
    padded = (counts + (EXPERT_ROWS - 1)) & (-EXPERT_ROWS)
    padded = jnp.where(lane < N_EXPERTS, padded, 0)
    ends = padded
    shift = 1
    while shift < N_EXPERTS:
        ends = ends + jnp.where(lane >= shift, pltpu.roll(ends, shift, 1), 0)
        shift *= 2
    return padded, ends


def _pos_kernel(eid_ref, counts_ref, pos_ref, tile_ref, seg_ref, carry):
    tm = eid_ref.shape[1]

    @pl.when(pl.program_id(0) == 0)
    def _():
        carry[...] = jnp.zeros_like(carry)

    padded, ends = _segment_ends(counts_ref[...])
    starts = (ends - padded).astype(F32).T[0:N_EXPERTS, 0:1]
    eid = eid_ref[...]
    expert = lax.broadcasted_iota(I32, (N_EXPERTS, tm), 0)
    oh0 = expert == eid[0:1, :]
    oh1 = expert == eid[1:2, :]
    used = jnp.where(oh0 | oh1, 1.0, 0.0)
    r = lax.broadcasted_iota(I32, (tm, tm), 0)
    c = lax.broadcasted_iota(I32, (tm, tm), 1)
    earlier = jnp.where(r < c, 1.0, 0.0).astype(BF16)
    base = starts + carry[:, 0:1] + _dot(used.astype(BF16), earlier)
    p0 = jnp.sum(jnp.where(oh0, base, 0.0), axis=0, keepdims=True)
    p1 = jnp.sum(jnp.where(oh1, base, 0.0), axis=0, keepdims=True)
    slot = lax.broadcasted_iota(I32, pos_ref.shape, 0)
    pos_ref[...] = jnp.where(slot == 0, p0, jnp.where(slot == 1, p1, 0.0)).astype(I32)
    carry[...] = carry[...] + jnp.sum(used, axis=1, keepdims=True)

    @pl.when(pl.program_id(0) == 0)
    def _():
        n_tiles = tile_ref.shape[0]
        first_row = lax.broadcasted_iota(I32, (n_tiles, LANES), 0) * EXPERT_ROWS
        elane = lax.broadcasted_iota(I32, (n_tiles, LANES), 1)
        done = jnp.where((ends[0:1, :] <= first_row) & (elane < N_EXPERTS), 1, 0)
        expert = jnp.minimum(jnp.sum(done, axis=1, keepdims=True), N_EXPERTS - 1)
        total = jnp.max(ends[0:1, :], axis=1, keepdims=True)
        live = jnp.where(first_row < total, 1, 0)
        own_end = jnp.sum(jnp.where(elane == expert, ends[0:1, :], 0), axis=1, keepdims=True)
        upcoming = jnp.sum(jnp.where((ends[0:1, :] <= own_end) & (elane < N_EXPERTS), 1, 0),
                           axis=1, keepdims=True)
        upcoming = jnp.where(upcoming < N_EXPERTS, upcoming, -1)
        tile_ref[...] = jnp.where(elane == 0, expert,
                                  jnp.where(elane == 1, live, jnp.where(elane == 2, upcoming, 0)))
        seg_ref[...] = jnp.concatenate([ends, padded], axis=0)


def _pos(eid_t, counts, n_tiles):
    t = eid_t.shape[1]
    tm = SORT_ROWS
    col = pl.BlockSpec((8, tm), lambda i: (0, i))
    return pl.pallas_call(
        _pos_kernel,
        grid=(t // tm,),
        in_specs=[col, pl.BlockSpec((8, LANES), lambda i: (0, 0))],
        out_specs=[col, pl.BlockSpec((n_tiles, LANES), lambda i: (0, 0)),
                   pl.BlockSpec((16, LANES), lambda i: (0, 0))],
        out_shape=[jax.ShapeDtypeStruct((8, t), I32),
                   jax.ShapeDtypeStruct((n_tiles, LANES), I32),
                   jax.ShapeDtypeStruct((16, LANES), I32)],
        scratch_shapes=[pltpu.VMEM((N_EXPERTS, LANES), F32)],
        compiler_params=_params(1),
        name="pos",
    )(eid_t, counts)


def _tokens(ref, first, count, tile):
    start = first * tile
    if not isinstance(start, int):
        start = pl.multiple_of(start, tile)
    return ref.at[pl.ds(start, count * tile)]


def _dispatch_kernel(seg_end_ref, seg_len_ref, pos0_ref, pos1_ref, h2_ref, xs_ref, zeros, sem, *, tile):
    i = pl.program_id(0)
    pos_refs = (pos0_ref, pos1_ref)
    tm = MOVE_ROWS

    def clear_copy(first):
        return pltpu.make_async_copy(zeros, _tokens(xs_ref, first, EXPERT_ROWS, tile), sem)

    @pl.when(i == 0)
    def _():
        zeros[...] = jnp.zeros_like(zeros)
        for e in range(N_EXPERTS):
            @pl.when(seg_len_ref[e] > 0)
            def _():
                clear_copy(pl.multiple_of(seg_end_ref[e] - EXPERT_ROWS, EXPERT_ROWS)).start()
        for e in range(N_EXPERTS):
            @pl.when(seg_len_ref[e] > 0)
            def _():
                clear_copy(0).wait()

        first_unused = seg_end_ref[N_EXPERTS - 1] // EXPERT_ROWS
        n_tiles = xs_ref.shape[0] // (EXPERT_ROWS * tile)

        def clear(t, _):
            clear_copy(pl.multiple_of(t * EXPERT_ROWS, EXPERT_ROWS)).start()
            return 0

        def clear_done(t, _):
            clear_copy(0).wait()
            return 0

        lax.fori_loop(first_unused, n_tiles, clear, 0)
        lax.fori_loop(first_unused, n_tiles, clear_done, 0)

    for r in range(tm):
        for k in range(2):
            pltpu.make_async_copy(_tokens(h2_ref, r, 1, tile), _tokens(xs_ref, pos_refs[k][r], 1, tile),
                                  sem).start(priority=k)

    def drain(r, _):
        pltpu.make_async_copy(_tokens(h2_ref, 0, 1, tile), _tokens(xs_ref, 0, 1, tile), sem).wait()
        return 0

    lax.fori_loop(0, 2 * tm, drain, 0, unroll=8)


def _dispatch(seg_end, seg_len, pos_flat, h2_tiles, n_rows, tile):
    t = h2_tiles.shape[0] // tile
    tm = MOVE_ROWS
    n = t // tm
    grid_spec = pltpu.PrefetchScalarGridSpec(
        num_scalar_prefetch=2,
        grid=(n,),
        in_specs=[pl.BlockSpec((tm,), lambda i, *_: (i,), memory_space=pltpu.SMEM),
                  pl.BlockSpec((tm,), lambda i, *_: (n + i,), memory_space=pltpu.SMEM),
                  pl.BlockSpec((tm * tile, LANES), lambda i, *_: (i, 0))],
        out_specs=pl.BlockSpec(memory_space=pl.ANY),
        scratch_shapes=[pltpu.VMEM((EXPERT_ROWS * tile, LANES), F32), pltpu.SemaphoreType.DMA(())],
    )
    return pl.pallas_call(
        functools.partial(_dispatch_kernel, tile=tile),
        grid_spec=grid_spec,
        out_shape=jax.ShapeDtypeStruct((n_rows * tile, LANES), F32),
        compiler_params=pltpu.CompilerParams(dimension_semantics=("arbitrary",),
                                             vmem_limit_bytes=VMEM_LIMIT),
        name="dispatch",
    )(seg_end, seg_len, pos_flat, pos_flat, h2_tiles)


def _expert_kernel(tile_expert_ref, tile_live_ref, tile_next_ref, xs_hbm, wg_hbm, wu_hbm, wd_hbm, o_ref,
                   wg, wu, wd, xbuf, xsem, wbuf_g, wbuf_u, wbuf_d, wsem, wslot):
    i = pl.program_id(0)
    n = pl.num_programs(0)
    rows = xbuf.shape[1]

    def weight_copies(expert, slot):
        return [pltpu.make_async_copy(src.at[expert], dst.at[slot], wsem.at[slot, k])
                for k, (src, dst) in enumerate(((wg_hbm, wbuf_g), (wu_hbm, wbuf_u), (wd_hbm, wbuf_d)))]

    def fetch(step):
        slot = step % EXPERT_RING
        start = pl.multiple_of(step * rows, rows)
        return pltpu.make_async_copy(xs_hbm.at[pl.ds(start, rows)], xbuf.at[slot], xsem.at[slot])

    @pl.when(i == 0)
    def _():
        for step in range(EXPERT_RING - 1):
            fetch(step).start()

    @pl.when(i + EXPERT_RING - 1 < n)
    def _():
        fetch(i + EXPERT_RING - 1).start()

    fetch(i).wait()
    xs_ref = xbuf.at[i % EXPERT_RING]

    @pl.when(tile_live_ref[i] > 0)
    def _():
        changed = jnp.logical_or(i == 0, tile_expert_ref[i] != tile_expert_ref[jnp.maximum(i - 1, 0)])

        @pl.when(changed)
        def _():
            @pl.when(i == 0)
            def _():
                wslot[0] = 0
                for copy in weight_copies(tile_expert_ref[0], 0):
                    copy.start()

            slot = wslot[0]
            for copy in weight_copies(tile_expert_ref[i], slot):
                copy.wait()
            wg[...] = wbuf_g[slot].astype(BF16)
            wu[...] = wbuf_u[slot].astype(BF16)
            wd[...] = wbuf_d[slot].astype(BF16)
            upcoming = tile_next_ref[i]

            @pl.when(upcoming >= 0)
            def _():
                for copy in weight_copies(upcoming, 1 - slot):
                    copy.start()

            wslot[0] = 1 - slot

        x = _load_token_tiles(xs_ref, EXPERT_ROWS).astype(BF16)
        de = wg.shape[1]
        chunk = 2 * LANES
        pre = [(_dot(x, wg[:, c:c + chunk]), _dot(x, wu[:, c:c + chunk])) for c in range(0, de, chunk)]
        hid = [((gate * jax.nn.sigmoid(gate)) * up).astype(BF16) for gate, up in pre]
        tile = o_ref.shape[0] // EXPERT_ROWS
        for oc in range(0, wd.shape[1], chunk):
            out = None
            for n, h in enumerate(hid):
                part = _dot(h, wd[n * chunk:(n + 1) * chunk, oc:oc + chunk])
                out = part if out is None else out + part
            for c in range(chunk // LANES):
                o_ref[pl.ds(oc // LANES + c, EXPERT_ROWS, stride=tile), :] = out[:, c * LANES:(c + 1) * LANES]

    @pl.when(tile_live_ref[i] == 0)
    def _():
        o_ref[...] = jnp.zeros_like(o_ref)


def _experts(tile_expert, tile_live, tile_next, xs_tiles, w_gate, w_up, w_down):
    d, de = w_gate.shape[1:]
    tile = d // LANES
    n_rows = xs_tiles.shape[0] // tile
    tm = EXPERT_ROWS
    anywhere = pl.BlockSpec(memory_space=pl.ANY)
    grid_spec = pltpu.PrefetchScalarGridSpec(
        num_scalar_prefetch=3,
        grid=(n_rows // tm,),
        in_specs=[anywhere, anywhere, anywhere, anywhere],
        out_specs=pl.BlockSpec((tm * tile, LANES), lambda i, *_: (i, 0)),
        scratch_shapes=[pltpu.VMEM((d, de), BF16), pltpu.VMEM((d, de), BF16), pltpu.VMEM((de, d), BF16),
                        pltpu.VMEM((EXPERT_RING, tm * tile, LANES), F32),
                        pltpu.SemaphoreType.DMA((EXPERT_RING,)),
                        pltpu.VMEM((2, d, de), F32), pltpu.VMEM((2, d, de), F32), pltpu.VMEM((2, de, d), F32),
                        pltpu.SemaphoreType.DMA((2, 3)), pltpu.SMEM((1,), I32)],
    )
    return pl.pallas_call(
        _expert_kernel,
        grid_spec=grid_spec,
        out_shape=jax.ShapeDtypeStruct(xs_tiles.shape, F32),
        compiler_params=_params(1),
        name="experts",
    )(tile_expert, tile_live, tile_next, xs_tiles, w_gate, w_up, w_down)


def _combine_kernel(pos0_ref, pos1_ref, pos0_next_ref, pos1_next_ref, ys_ref, x1_ref, ew_ref, mod_ref, g_ref,
                    o_ref, buf, sem, *, d, final_norm):
    i = pl.program_id(0)
    n = pl.num_programs(0)
    tm = MOVE_ROWS
    tile = d // LANES

    def fetch(prefs, slot):
        for r in range(tm):
            for k in range(2):
                pltpu.make_async_copy(_tokens(ys_ref, prefs[k][r], 1, tile),
                                      _tokens(buf.at[slot, k], r, 1, tile),
                                      sem.at[slot]).start(priority=k)

    @pl.when(i == 0)
    def _():
        fetch((pos0_ref, pos1_ref), 0)

    for parity in range(2):
        @pl.when((i + 1 < n) & ((i + 1) % 2 == parity))
        def _():
            fetch((pos0_next_ref, pos1_next_ref), parity)

    slot = i % 2

    def drain(r, _):
        pltpu.make_async_copy(_tokens(ys_ref, 0, 1, tile), _tokens(buf.at[slot, 0], 0, 1, tile),
                              sem.at[slot]).wait()
        return 0

    lax.fori_loop(0, 2 * tm, drain, 0, unroll=8)

    ew = ew_ref[...]
    y = (ew[:, 0:1] * _load_token_tiles(buf.at[slot, 0], tm)
         + ew[:, 1:2] * _load_token_tiles(buf.at[slot, 1], tm))
    g2 = mod_ref[0, :, 5 * d:6 * d]
    x2 = x1_ref[...] + g2 * y
    o_ref[...] = _rms(x2, g_ref[...]) if final_norm else x2


def _combine(pos_flat, ys, x1, ew, mod3, final_g, seq, final_norm):
    t, d = x1.shape
    tm = MOVE_ROWS
    n = t // tm
    per_batch = seq // tm
    return pl.pallas_call(
        functools.partial(_combine_kernel, d=d, final_norm=final_norm),
        grid=(n,),
        in_specs=[pl.BlockSpec((tm,), lambda i: (i,), memory_space=pltpu.SMEM),
                  pl.BlockSpec((tm,), lambda i: (n + i,), memory_space=pltpu.SMEM),
                  pl.BlockSpec((tm,), lambda i: (jnp.minimum(i + 1, n - 1),), memory_space=pltpu.SMEM),
                  pl.BlockSpec((tm,), lambda i: (n + jnp.minimum(i + 1, n - 1),), memory_space=pltpu.SMEM),
                  pl.BlockSpec(memory_space=pl.ANY),
                  pl.BlockSpec((tm, d), lambda i: (i, 0)),
                  pl.BlockSpec((tm, LANES), lambda i: (i, 0)),
                  pl.BlockSpec((1, 1, mod3.shape[2]), lambda i: (i // per_batch, 0, 0)),
                  pl.BlockSpec((1, d), lambda i: (0, 0))],
        out_specs=pl.BlockSpec((tm, d), lambda i: (i, 0)),
        out_shape=jax.ShapeDtypeStruct((t, d), F32),
        scratch_shapes=[pltpu.VMEM((2, 2, tm * d // LANES, LANES), F32), pltpu.SemaphoreType.DMA((2,))],
        compiler_params=_params(1),
        name="combine",
    )(pos_flat, pos_flat, pos_flat, pos_flat, ys, x1, ew, mod3, final_g)


def _rope_tables(seq):
    inv = 1.0 / (ROPE_THETA ** (jnp.arange(0, HEAD_DIM, 2, dtype=F32) / HEAD_DIM))
    ang = jnp.arange(seq, dtype=F32)[:, None] * inv[None, :]
    cos, sin = jnp.cos(ang), jnp.sin(ang)
    cos_head = jnp.concatenate([cos, cos], axis=1)
    sin_head = jnp.concatenate([-sin, sin], axis=1)
    reps = LANES // HEAD_DIM
    return jnp.tile(cos_head, (1, reps)), jnp.tile(sin_head, (1, reps))


def kernel(x, c, w_ada, b_ada, norm1_g, w_in, lambda_q1, lambda_k1, lambda_q2, lambda_k2,
           diff_subln_g, w_proj_moba, w_proj_diff, w_out, norm2_g, w_group, b_group,
           w_expert, b_expert, w_gate, w_up, w_down, final_g):
    batch, seq, d = x.shape
    depth = w_ada.shape[0]
    t = batch * seq
    assert seq % PROJ_ROWS == 0 and seq % MOBA_BLOCK == 0 and seq // MOBA_BLOCK <= LANES
    assert ATTN_ROWS == MOBA_BLOCK and d % SEG == 0 and t % SORT_ROWS == 0 and batch <= 8
    assert seq % ATTN_Q_ROWS == 0 and ATTN_Q_ROWS % ATTN_ROWS == 0
    assert EXPERT_ROWS & (EXPERT_ROWS - 1) == 0
    n_rows = 2 * t + N_EXPERTS * EXPERT_ROWS
    n_tiles = n_rows // EXPERT_ROWS
    cos, sin = _rope_tables(seq)
    c_pad = jnp.zeros((8, d), F32).at[:batch].set(c)
    xf = x.reshape(t, d)
    row = lambda v: v.reshape(1, -1)
    for l in range(depth):
        mod = _ada(c_pad, w_ada[l], row(b_ada[l]))
        mod3 = mod[:batch].reshape(batch, 1, 6 * d)
        w_qkv = w_in[l][:, :N_QKV_SEGS * SEG].astype(BF16)
        w_gates = w_in[l][:, N_QKV_SEGS * SEG:].astype(BF16)
        qm, km, vm, qd, kd, vd, kmean = _proj(xf, mod3, row(norm1_g[l]), w_qkv, cos, sin, seq)
        kmean = kmean.reshape(batch, seq // MOBA_BLOCK, SEG)
        om = _moba(qm, km, vm, kmean, batch, seq)
        lam_init = 0.8 - 0.6 * math.exp(-0.3 * l)
        od = _diff(qd, kd, vd, row(lambda_q1[l]), row(lambda_k1[l]), row(lambda_q2[l]),
                   row(lambda_k2[l]), diff_subln_g[l].reshape(-1, 1), batch, seq, lam_init)
        w_router = jnp.zeros((d, LANES), F32)
        w_router = w_router.at[:, :N_GROUPS].set(w_group[l])
        w_router = w_router.at[:, N_GROUPS:N_GROUPS + N_EXPERTS].set(w_expert[l])
        b_router = jnp.zeros((1, LANES), F32)
        b_router = b_router.at[0, :N_GROUPS].set(b_group[l])
        b_router = b_router.at[0, N_GROUPS:N_GROUPS + N_EXPERTS].set(b_expert[l])
        x1, h2, eid_t, ew, counts = _merge(xf, om, od, mod3, row(norm1_g[l]), row(norm2_g[l]), w_gates,
                                 w_proj_moba[l].astype(BF16), w_proj_diff[l].astype(BF16),
                                 w_out[l].astype(BF16), w_router, b_router, seq)
        pos_t, tiles, segs = _pos(eid_t, counts, n_tiles)
        pos_flat = pos_t[:2].reshape(2 * t)
        xs = _dispatch(segs[0, :N_EXPERTS], segs[8, :N_EXPERTS], pos_flat, h2, n_rows, d // LANES)
        ys = _experts(tiles[:, 0], tiles[:, 1], tiles[:, 2], xs, w_gate[l], w_up[l], w_down[l])
        xf = _combine(pos_flat, ys, x1, ew, mod3, row(final_g), seq, final_norm=(l == depth - 1))
    return xf.reshape(batch, seq, d)
```

```python
import functools
import math

import jax
import jax.numpy as jnp
from jax import lax
from jax.experimental import pallas as pl
from jax.experimental.pallas import tpu as pltpu

F32 = jnp.float32
BF16 = jnp.bfloat16
I32 = jnp.int32

LANES = 128
HEAD_DIM = 64
HEADS_PER_VREG = LANES // HEAD_DIM
MOBA_HEADS = 8
MOBA_BLOCK = 256
MOBA_TOPK = 3
DIFF_HEADS = 4
ROPE_THETA = 10000.0
N_GROUPS = 4
EXPERTS_PER_GROUP = 8
N_EXPERTS = N_GROUPS * EXPERTS_PER_GROUP
NORM_EPS = 1e-6
NEG_INF = -1e30
SEG = MOBA_HEADS * HEAD_DIM
N_QKV_SEGS = 6
SUM_ROWS = 16

PROJ_ROWS = 512
ATTN_ROWS = 256
ATTN_Q_ROWS = 1024
MERGE_ROWS = 512
SORT_ROWS = 512
EXPERT_ROWS = 512
EXPERT_RING = 3
DISPATCH_ROWS = 512
COMBINE_ROWS = 256
VMEM_LIMIT = 56 * 1024 * 1024


def _params(n_axes, vmem=VMEM_LIMIT):
    return pltpu.CompilerParams(dimension_semantics=("arbitrary",) * n_axes,
                                vmem_limit_bytes=vmem)


def _dot(a, b):
    return jnp.dot(a, b, preferred_element_type=F32)


def _dot_nt(a, b):
    return lax.dot_general(a, b, (((1,), (1,)), ((), ())), preferred_element_type=F32)


def _store_token_tiles(ref, x):
    n, d = x.shape
    chunks = d // LANES
    for c in range(chunks):
        ref[pl.ds(c, n, stride=chunks), :] = x[:, c * LANES:(c + 1) * LANES]


def _load_token_tiles(ref, n):
    chunks = ref.shape[0] // n
    return jnp.concatenate([ref[pl.ds(c, n, stride=chunks), :] for c in range(chunks)], axis=1)


def _rms(x, g):
    return x * lax.rsqrt(jnp.mean(x * x, axis=-1, keepdims=True) + NORM_EPS) * g


def _ada_kernel(c_ref, w_ref, b_ref, o_ref):
    c = c_ref[...]
    o_ref[...] = _dot(c * jax.nn.sigmoid(c), w_ref[...]) + b_ref[...]


def _ada(c_pad, w, b):
    rows, d = c_pad.shape
    n = w.shape[1]
    tn = 1536
    return pl.pallas_call(
        _ada_kernel,
        grid=(n // tn,),
        in_specs=[pl.BlockSpec((rows, d), lambda j: (0, 0)),
                  pl.BlockSpec((d, tn), lambda j: (0, j)),
                  pl.BlockSpec((1, tn), lambda j: (0, j))],
        out_specs=pl.BlockSpec((rows, tn), lambda j: (0, j)),
        out_shape=jax.ShapeDtypeStruct((rows, n), F32),
        compiler_params=_params(1),
        name="ada",
    )(c_pad, w, b)


def _rope(x, cos, sin_signed):
    half = HEAD_DIM // 2
    width = x.shape[1]
    lane = lax.broadcasted_iota(I32, x.shape, 1)
    first = (lane & (HEAD_DIM - 1)) < half
    partner = jnp.where(first, pltpu.roll(x, width - half, 1), pltpu.roll(x, half, 1))
    return x * cos + partner * sin_signed


def _proj_kernel(x_ref, mod_ref, g_ref, w_ref, cos_ref, sin_ref,
                 qm_ref, km_ref, vm_ref, qd_ref, kd_ref, vd_ref, kmean_ref, *, d):
    x = x_ref[...]
    sh = mod_ref[0, :, 0:d]
    sc = mod_ref[0, :, d:2 * d]
    h = (_rms(x, g_ref[...]) * (1.0 + sc) + sh).astype(BF16)
    cos = jnp.concatenate([cos_ref[...]] * (SEG // LANES), axis=1)
    sin = jnp.concatenate([sin_ref[...]] * (SEG // LANES), axis=1)
    scale = HEAD_DIM ** -0.5 * math.log2(math.e)
    outs = (qm_ref, km_ref, vm_ref, qd_ref, kd_ref, vd_ref)
    n_blk = x.shape[0] // ATTN_ROWS
    for seg, o_ref in enumerate(outs):
        y = _dot(h, w_ref[:, seg * SEG:(seg + 1) * SEG])
        if seg in (0, 1, 3, 4):
            y = _rope(y, cos, sin)
        if seg in (0, 3):
            y = y * scale
        if seg == 1:
            for blk in range(n_blk):
                rows = y[blk * MOBA_BLOCK:(blk + 1) * MOBA_BLOCK]
                kmean_ref[0, blk:blk + 1, :] = jnp.mean(rows, axis=0, keepdims=True)
        if seg in (0, 2, 3, 5):
            for blk in range(n_blk):
                for part in range(SEG // LANES):
                    piece = y[blk * ATTN_ROWS:(blk + 1) * ATTN_ROWS, part * LANES:(part + 1) * LANES]
                    o_ref[blk, part * LANES:(part + 1) * LANES, :] = piece.T.astype(BF16)
        else:
            o_ref[...] = y.astype(BF16)


def _proj(x2, mod3, g, w_qkv, cos, sin, seq):
    t, d = x2.shape
    tm = PROJ_ROWS
    per_batch = seq // tm
    row_spec = pl.BlockSpec((tm, SEG), lambda i: (i, 0))
    tab_spec = pl.BlockSpec((tm, LANES), lambda i: (i % per_batch, 0))
    act = jax.ShapeDtypeStruct((t, SEG), BF16)
    act_t = jax.ShapeDtypeStruct((t // ATTN_ROWS, SEG, ATTN_ROWS), BF16)
    t_spec = pl.BlockSpec((tm // ATTN_ROWS, SEG, ATTN_ROWS), lambda i: (i, 0, 0))
    return pl.pallas_call(
        functools.partial(_proj_kernel, d=d),
        grid=(t // tm,),
        in_specs=[pl.BlockSpec((tm, d), lambda i: (i, 0)),
                  pl.BlockSpec((1, 1, mod3.shape[2]), lambda i: (i // per_batch, 0, 0)),
                  pl.BlockSpec((1, d), lambda i: (0, 0)),
                  pl.BlockSpec(w_qkv.shape, lambda i: (0, 0)),
                  tab_spec, tab_spec],
        out_specs=[t_spec, row_spec, t_spec, t_spec, row_spec, t_spec,
                   pl.BlockSpec((1, tm // MOBA_BLOCK, SEG), lambda i: (i, 0, 0))],
        out_shape=[act_t, act, act_t, act_t, act, act_t,
                   jax.ShapeDtypeStruct((t // tm, tm // MOBA_BLOCK, SEG), F32)],
        compiler_params=_params(1),
        name="proj",
    )(x2, mod3, g, w_qkv, cos, sin)


def _attend(qs, keys_of_tile, k_ref, vt_ref, v_rows, q_tile, scratch):
    tq, tk = ATTN_Q_ROWS, ATTN_ROWS
    sub = tq // tk
    ones = jnp.ones((SUM_ROWS, tk), BF16)
    n_soft = len(qs)
    acc_refs, max_refs = scratch[:n_soft], scratch[n_soft:]
    for acc_ref in acc_refs:
        acc_ref[...] = jnp.zeros(acc_ref.shape, F32)

    def scores(j, first_query=0):
        start = pl.multiple_of(j * tk, tk)
        keys = keys_of_tile(j, k_ref[pl.ds(start, tk), :])
        return tuple(_dot(keys, q[:, first_query:]) for q in qs), vt_ref[j]

    def update(tile, mask, maxes, first_query=0):
        s_all, vt = tile
        new = ()
        for n, s in enumerate(s_all):
            if mask is not None:
                own = jnp.where(mask, s[:, :tk], NEG_INF)
                s = own if s.shape[1] == tk else jnp.concatenate([own, s[:, tk:]], axis=1)
            m = max_refs[n][:, first_query:] if maxes is None else maxes[n]
            m_new = jnp.maximum(m, jnp.max(s, axis=0, keepdims=True))
            alpha = jnp.exp2(m - m_new)
            p = jnp.exp2(s - m_new).astype(BF16)
            if maxes is None:
                max_refs[n][:, first_query:] = m_new
            new += (m_new,)
            vt_sum = jnp.concatenate([vt[v_rows[n]], ones], axis=0)
            acc_refs[n][:, first_query:] = alpha * acc_refs[n][:, first_query:] + _dot(vt_sum, p)
        return new

    first_own = q_tile * sub

    def body(group, maxes):
        for tile in [scores(sub * group + b) for b in range(sub)]:
            maxes = update(tile, None, maxes)
        return maxes

    maxes = lax.fori_loop(0, q_tile, body, (jnp.full((1, tq), NEG_INF, F32),) * n_soft)
    for max_ref, m in zip(max_refs, maxes):
        max_ref[...] = m
    own = [scores(first_own + b, b * tk) for b in range(sub)]
    causal = lax.broadcasted_iota(I32, (tk, tk), 0) <= lax.broadcasted_iota(I32, (tk, tk), 1)
    for b in range(sub):
        update(own[b], causal, None, b * tk)
    out = []
    for n, acc_ref in enumerate(acc_refs):
        rows = v_rows[n].stop - v_rows[n].start
        out.append((acc_ref[rows:rows + 1, :], acc_ref[0:rows, :]))
    return out


def _store_position_tiles(ref, x_t):
    for b in range(ref.shape[0]):
        ref[b] = x_t[:, b * ATTN_ROWS:(b + 1) * ATTN_ROWS]


def _softmax_scratch(n_softmax, features):
    return ([pltpu.VMEM((features + SUM_ROWS, ATTN_Q_ROWS), F32)] * n_softmax
            + [pltpu.VMEM((1, ATTN_Q_ROWS), F32)] * n_softmax)


def _moba_kernel(q_ref, k_ref, v_ref, kmean_ref, o_ref, *scratch, n_blocks):
    tq = ATTN_Q_ROWS
    q_tile = pl.program_id(2)
    q_t = jnp.concatenate([q_ref[b] for b in range(q_ref.shape[0])], axis=1)
    feat = lax.broadcasted_iota(I32, (LANES, tq), 0)
    blk = lax.broadcasted_iota(I32, (n_blocks, tq), 0)
    qry = lax.broadcasted_iota(I32, (n_blocks, tq), 1)
    own_block = q_tile * (tq // MOBA_BLOCK) + (qry >> int(math.log2(MOBA_BLOCK)))
    kmean = kmean_ref[0]
    km_head = lax.broadcasted_iota(I32, kmean.shape, 1) >> 6
    pad = jnp.zeros((LANES - n_blocks, tq), BF16)
    lane_k = lax.broadcasted_iota(I32, (ATTN_ROWS, LANES), 1)
    qs = []
    for head in range(HEADS_PER_VREG):
        qh = jnp.where((feat >> 6) == head, q_t, jnp.zeros_like(q_t))
        km = jnp.where(km_head == head, kmean, 0.0)
        km_hi = km.astype(BF16)
        km_lo = (km - km_hi.astype(F32)).astype(BF16)
        gate = _dot(km_hi, qh) + _dot(km_lo, qh)
        gate = jnp.where(blk < own_block, gate, NEG_INF)
        bias = jnp.where(blk == own_block, 0.0, NEG_INF)
        for _ in range(min(MOBA_TOPK, n_blocks)):
            top = jnp.max(gate, axis=0, keepdims=True)
            first = jnp.min(jnp.where(gate == top, blk, n_blocks), axis=0, keepdims=True)
            picked = blk == first
            bias = jnp.where(picked & (blk < own_block), 0.0, bias)
            gate = jnp.where(picked, -jnp.inf, gate)
        qs.append(jnp.concatenate([qh, bias.astype(BF16), pad], axis=0))

    def rhs_of_tile(j, keys):
        return jnp.concatenate([keys, jnp.where(lane_k == j, 1.0, 0.0).astype(BF16)], axis=1)

    v_rows = [slice(h * HEAD_DIM, (h + 1) * HEAD_DIM) for h in range(HEADS_PER_VREG)]
    stats = _attend(qs, rhs_of_tile, k_ref, v_ref, v_rows, q_tile, scratch)
    out_t = jnp.concatenate([acc / l for l, acc in stats], axis=0)
    _store_position_tiles(o_ref, out_t.astype(BF16))


def _moba(qm, km, vm_t, kmean, batch, seq):
    t = batch * seq
    tq = ATTN_Q_ROWS
    nq = seq // tq
    n_blocks = seq // MOBA_BLOCK
    pairs = MOBA_HEADS // HEADS_PER_VREG
    n_kv = seq // ATTN_ROWS
    return pl.pallas_call(
        functools.partial(_moba_kernel, n_blocks=n_blocks),
        grid=(batch, pairs, nq),
        in_specs=[pl.BlockSpec((tq // ATTN_ROWS, LANES, ATTN_ROWS), lambda b, p, i: (b * nq + i, p, 0)),
                  pl.BlockSpec((seq, LANES), lambda b, p, i: (b, p)),
                  pl.BlockSpec((n_kv, LANES, ATTN_ROWS), lambda b, p, i: (b, p, 0)),
                  pl.BlockSpec((1, n_blocks, LANES), lambda b, p, i: (b, 0, p))],
        out_specs=pl.BlockSpec((tq // ATTN_ROWS, LANES, ATTN_ROWS), lambda b, p, i: (b * nq + i, p, 0)),
        out_shape=jax.ShapeDtypeStruct((t // ATTN_ROWS, SEG, ATTN_ROWS), BF16),
        scratch_shapes=_softmax_scratch(HEADS_PER_VREG, HEAD_DIM),
        compiler_params=_params(3),
        name="moba",
    )(qm, km, vm_t, kmean)


def _diff_kernel(q_ref, k_ref, v_ref, lq1_ref, lk1_ref, lq2_ref, lk2_ref, g_ref, o_ref,
                 *scratch, lam_init):
    tq = ATTN_Q_ROWS
    q_t = jnp.concatenate([q_ref[b] for b in range(q_ref.shape[0])], axis=1)
    comp = lax.broadcasted_iota(I32, (LANES, tq), 0) >> 6
    qs = [jnp.where(comp == c, q_t, jnp.zeros_like(q_t)) for c in range(2)]
    v_rows = [slice(0, LANES)] * 2
    (l1, a1), (l2, a2) = _attend(qs, lambda j, keys: keys, k_ref, v_ref, v_rows,
                                 pl.program_id(2), scratch)
    lam = (jnp.exp(jnp.sum(lq1_ref[...] * lk1_ref[...], axis=1, keepdims=True))
           - jnp.exp(jnp.sum(lq2_ref[...] * lk2_ref[...], axis=1, keepdims=True)) + lam_init)
    o = a1 / l1 - lam * (a2 / l2)
    o = o * lax.rsqrt(jnp.mean(o * o, axis=0, keepdims=True) + NORM_EPS) * g_ref[...]
    _store_position_tiles(o_ref, (o * (1.0 - lam_init)).astype(BF16))


def _diff(qd, kd, vd_t, lq1, lk1, lq2, lk2, g, batch, seq, lam_init):
    t = batch * seq
    tq = ATTN_Q_ROWS
    nq = seq // tq
    n_kv = seq // ATTN_ROWS
    vec = lambda a: pl.BlockSpec(a.shape, lambda b, h, i: (0, 0))
    return pl.pallas_call(
        functools.partial(_diff_kernel, lam_init=lam_init),
        grid=(batch, DIFF_HEADS, nq),
        in_specs=[pl.BlockSpec((tq // ATTN_ROWS, LANES, ATTN_ROWS), lambda b, h, i: (b * nq + i, h, 0)),
                  pl.BlockSpec((seq, LANES), lambda b, h, i: (b, h)),
                  pl.BlockSpec((n_kv, LANES, ATTN_ROWS), lambda b, h, i: (b, h, 0)),
                  vec(lq1), vec(lk1), vec(lq2), vec(lk2), vec(g)],
        out_specs=pl.BlockSpec((tq // ATTN_ROWS, LANES, ATTN_ROWS), lambda b, h, i: (b * nq + i, h, 0)),
        out_shape=jax.ShapeDtypeStruct((t // ATTN_ROWS, SEG, ATTN_ROWS), BF16),
        scratch_shapes=_softmax_scratch(2, LANES),
        compiler_params=_params(3),
        name="diff",
    )(qd, kd, vd_t, lq1, lk1, lq2, lk2, g)


def _merge_kernel(x_ref, om_ref, od_ref, mod_ref, g1n_ref, g2n_ref, wg_ref, wpm_ref, wpd_ref,
                  wout_ref, wr_ref, br_ref, x1_ref, h2_ref, eid_ref, ew_ref, counts_ref, merged, counts,
                  *, d):
    x = x_ref[...]
    sh1 = mod_ref[0, :, 0:d]
    sc1 = mod_ref[0, :, d:2 * d]
    g1 = mod_ref[0, :, 2 * d:3 * d]
    sh2 = mod_ref[0, :, 3 * d:4 * d]
    sc2 = mod_ref[0, :, 4 * d:5 * d]
    h = (_rms(x, g1n_ref[...]) * (1.0 + sc1) + sh1).astype(BF16)
    om, od = (jnp.concatenate([ref[b].astype(F32).T for b in range(ref.shape[0])], axis=0).astype(BF16)
              for ref in (om_ref, od_ref))
    chunk = SEG
    for c in range(d // chunk):
        cols = slice(c * chunk, (c + 1) * chunk)
        gm = _dot(h, wg_ref[:, c * chunk:(c + 1) * chunk])
        gd = _dot(h, wg_ref[:, d + c * chunk:d + (c + 1) * chunk])
        ym = _dot(om, wpm_ref[:, cols])
        yd = _dot(od, wpd_ref[:, cols])
        merged[:, cols] = (jax.nn.sigmoid(gm) * ym + jax.nn.sigmoid(gd) * yd).astype(BF16)
    x1 = x + g1 * _dot(merged[...], wout_ref[...])
    x1_ref[...] = x1
    h2 = _rms(x1, g2n_ref[...]) * (1.0 + sc2) + sh2
    _store_token_tiles(h2_ref, h2)

    h2_hi = h2.astype(BF16)
    h2_lo = (h2 - h2_hi.astype(F32)).astype(BF16)
    wr = wr_ref[...]
    wr_hi = wr.astype(BF16)
    wr_lo = (wr - wr_hi.astype(F32)).astype(BF16)
    logits = _dot(h2_hi, wr_hi) + _dot(h2_lo, wr_hi) + _dot(h2_hi, wr_lo) + br_ref[...]
    lane = lax.broadcasted_iota(I32, logits.shape, 1)
    is_group = lane < N_GROUPS
    gl = jnp.where(is_group, logits, -jnp.inf)
    gmax = jnp.max(gl, axis=1, keepdims=True)
    gexp = jnp.exp(gl - gmax)
    g_w = 1.0 / jnp.sum(gexp, axis=1, keepdims=True)
    g_idx = jnp.min(jnp.where(gl == gmax, lane, LANES), axis=1, keepdims=True)
    e_lane = lane - N_GROUPS
    in_group = (e_lane >= g_idx * EXPERTS_PER_GROUP) & (e_lane < (g_idx + 1) * EXPERTS_PER_GROUP)
    el = jnp.where(in_group, logits, -jnp.inf)
    emax = jnp.max(el, axis=1, keepdims=True)
    eexp = jnp.exp(el - emax)
    prob = eexp / jnp.sum(eexp, axis=1, keepdims=True)
    prob = jnp.where(in_group, prob, -1.0)
    p1 = jnp.max(prob, axis=1, keepdims=True)
    i1 = jnp.min(jnp.where(prob == p1, lane, LANES), axis=1, keepdims=True)
    prob2 = jnp.where(lane == i1, -1.0, prob)
    p2 = jnp.max(prob2, axis=1, keepdims=True)
    i2 = jnp.min(jnp.where(prob2 == p2, lane, LANES), axis=1, keepdims=True)
    tot = p1 + p2
    eid = jnp.where(lane == 0, i1 - N_GROUPS, jnp.where(lane == 1, i2 - N_GROUPS, 0))
    eid_ref[...] = eid.T[0:8, :]

    @pl.when(pl.program_id(0) == 0)
    def _():
        counts[...] = jnp.zeros_like(counts)

    chosen = jnp.where((lane == i1) | (lane == i2), 1.0, 0.0)
    per_lane = jnp.broadcast_to(jnp.sum(chosen, axis=0, keepdims=True), counts.shape)
    counts[...] = counts[...] + pltpu.roll(per_lane, LANES - N_GROUPS, 1)
    counts_ref[...] = counts[...].astype(I32)
    ew_ref[...] = jnp.where(lane == 0, g_w * (p1 / tot), jnp.where(lane == 1, g_w * (p2 / tot), 0.0))


def _merge(x2, om, od, mod3, g1n, g2n, wg, wpm, wpd, wout, wr, br, seq):
    t, d = x2.shape
    tm = MERGE_ROWS
    per_batch = seq // tm
    full = lambda a: pl.BlockSpec(a.shape, lambda i: (0,) * a.ndim)
    row = lambda w: pl.BlockSpec((tm, w), lambda i: (i, 0))
    tiles_t = pl.BlockSpec((tm // ATTN_ROWS, SEG, ATTN_ROWS), lambda i: (i, 0, 0))
    return pl.pallas_call(
        functools.partial(_merge_kernel, d=d),
        grid=(t // tm,),
        in_specs=[row(d), tiles_t, tiles_t,
                  pl.BlockSpec((1, 1, mod3.shape[2]), lambda i: (i // per_batch, 0, 0)),
                  full(g1n), full(g2n), full(wg), full(wpm), full(wpd), full(wout), full(wr), full(br)],
        out_specs=[row(d), pl.BlockSpec((tm * d // LANES, LANES), lambda i: (i, 0)),
                   pl.BlockSpec((8, tm), lambda i: (0, i)), row(LANES),
                   pl.BlockSpec((8, LANES), lambda i: (0, 0))],
        out_shape=[jax.ShapeDtypeStruct((t, d), F32), jax.ShapeDtypeStruct((t * d // LANES, LANES), F32),
                   jax.ShapeDtypeStruct((8, t), I32), jax.ShapeDtypeStruct((t, LANES), F32),
                   jax.ShapeDtypeStruct((8, LANES), I32)],
        scratch_shapes=[pltpu.VMEM((tm, d), BF16), pltpu.VMEM((8, LANES), F32)],
        compiler_params=_params(1),
        name="merge",
    )(x2, om, od, mod3, g1n, g2n, wg, wpm, wpd, wout, wr, br)


def _segment_ends(counts):
    lane = lax.broadcasted_iota(I32, counts.shape, 1)
    padded = (counts + (EXPERT_ROWS - 1)) & (-EXPERT_ROWS)
    padded = jnp.where(lane < N_EXPERTS, padded, 0)
    ends = padded
    shift = 1
    while shift < N_EXPERTS:
        ends = ends + jnp.where(lane >= shift, pltpu.roll(ends, shift, 1), 0)
        shift *= 2
    return padded, ends


def _pos_kernel(eid_ref, counts_ref, pos_ref, tile_ref, seg_ref, carry):
    tm = eid_ref.shape[1]

    @pl.when(pl.program_id(0) == 0)
    def _():
        carry[...] = jnp.zeros_like(carry)

    padded, ends = _segment_ends(counts_ref[...])
    starts = (ends - padded).astype(F32).T[0:N_EXPERTS, 0:1]
    eid = eid_ref[...]
    expert = lax.broadcasted_iota(I32, (N_EXPERTS, tm), 0)
    oh0 = expert == eid[0:1, :]
    oh1 = expert == eid[1:2, :]
    used = jnp.where(oh0 | oh1, 1.0, 0.0)
    r = lax.broadcasted_iota(I32, (tm, tm), 0)
    c = lax.broadcasted_iota(I32, (tm, tm), 1)
    earlier = jnp.where(r < c, 1.0, 0.0).astype(BF16)
    base = starts + carry[:, 0:1] + _dot(used.astype(BF16), earlier)
    p0 = jnp.sum(jnp.where(oh0, base, 0.0), axis=0, keepdims=True)
    p1 = jnp.sum(jnp.where(oh1, base, 0.0), axis=0, keepdims=True)
    slot = lax.broadcasted_iota(I32, pos_ref.shape, 0)
    pos_ref[...] = jnp.where(slot == 0, p0, jnp.where(slot == 1, p1, 0.0)).astype(I32)
    carry[...] = carry[...] + jnp.sum(used, axis=1, keepdims=True)

    @pl.when(pl.program_id(0) == 0)
    def _():
        n_tiles = tile_ref.shape[0]
        first_row = lax.broadcasted_iota(I32, (n_tiles, LANES), 0) * EXPERT_ROWS
        elane = lax.broadcasted_iota(I32, (n_tiles, LANES), 1)
        done = jnp.where((ends[0:1, :] <= first_row) & (elane < N_EXPERTS), 1, 0)
        expert = jnp.minimum(jnp.sum(done, axis=1, keepdims=True), N_EXPERTS - 1)
        total = jnp.max(ends[0:1, :], axis=1, keepdims=True)
        live = jnp.where(first_row < total, 1, 0)
        own_end = jnp.sum(jnp.where(elane == expert, ends[0:1, :], 0), axis=1, keepdims=True)
        upcoming = jnp.sum(jnp.where((ends[0:1, :] <= own_end) & (elane < N_EXPERTS), 1, 0),
                           axis=1, keepdims=True)
        upcoming = jnp.where(upcoming < N_EXPERTS, upcoming, -1)
        tile_ref[...] = jnp.where(elane == 0, expert,
                                  jnp.where(elane == 1, live, jnp.where(elane == 2, upcoming, 0)))
        seg_ref[...] = jnp.concatenate([ends, padded], axis=0)


def _pos(eid_t, counts, n_tiles):
    t = eid_t.shape[1]
    tm = SORT_ROWS
    col = pl.BlockSpec((8, tm), lambda i: (0, i))
    return pl.pallas_call(
        _pos_kernel,
        grid=(t // tm,),
        in_specs=[col, pl.BlockSpec((8, LANES), lambda i: (0, 0))],
        out_specs=[col, pl.BlockSpec((n_tiles, LANES), lambda i: (0, 0)),
                   pl.BlockSpec((16, LANES), lambda i: (0, 0))],
        out_shape=[jax.ShapeDtypeStruct((8, t), I32),
                   jax.ShapeDtypeStruct((n_tiles, LANES), I32),
                   jax.ShapeDtypeStruct((16, LANES), I32)],
        scratch_shapes=[pltpu.VMEM((N_EXPERTS, LANES), F32)],
        compiler_params=_params(1),
        name="pos",
    )(eid_t, counts)


def _tokens(ref, first, count, tile):
    start = first * tile
    if not isinstance(start, int):
        start = pl.multiple_of(start, tile)
    return ref.at[pl.ds(start, count * tile)]


def _dispatch_kernel(seg_end_ref, seg_len_ref, pos0_ref, pos1_ref, h2_ref, xs_ref, zeros, sem, *, tile):
    i = pl.program_id(0)
    pos_refs = (pos0_ref, pos1_ref)
    tm = DISPATCH_ROWS

    def clear_copy(first):
        return pltpu.make_async_copy(zeros, _tokens(xs_ref, first, EXPERT_ROWS, tile), sem)

    @pl.when(i == 0)
    def _():
        zeros[...] = jnp.zeros_like(zeros)
        for e in range(N_EXPERTS):
            @pl.when(seg_len_ref[e] > 0)
            def _():
                clear_copy(pl.multiple_of(seg_end_ref[e] - EXPERT_ROWS, EXPERT_ROWS)).start()
        for e in range(N_EXPERTS):
            @pl.when(seg_len_ref[e] > 0)
            def _():
                clear_copy(0).wait()

        first_unused = seg_end_ref[N_EXPERTS - 1] // EXPERT_ROWS
        n_tiles = xs_ref.shape[0] // (EXPERT_ROWS * tile)

        def clear(t, _):
            clear_copy(pl.multiple_of(t * EXPERT_ROWS, EXPERT_ROWS)).start()
            return 0

        def clear_done(t, _):
            clear_copy(0).wait()
            return 0

        lax.fori_loop(first_unused, n_tiles, clear, 0)
        lax.fori_loop(first_unused, n_tiles, clear_done, 0)

    for r in range(tm):
        for k in range(2):
            pltpu.make_async_copy(_tokens(h2_ref, r, 1, tile), _tokens(xs_ref, pos_refs[k][r], 1, tile),
                                  sem).start(priority=k)

    def drain(r, _):
        pltpu.make_async_copy(_tokens(h2_ref, 0, 1, tile), _tokens(xs_ref, 0, 1, tile), sem).wait()
        return 0

    lax.fori_loop(0, 2 * tm, drain, 0, unroll=8)


def _dispatch(seg_end, seg_len, pos_flat, h2_tiles, n_rows, tile):
    t = h2_tiles.shape[0] // tile
    tm = DISPATCH_ROWS
    n = t // tm
    grid_spec = pltpu.PrefetchScalarGridSpec(
        num_scalar_prefetch=2,
        grid=(n,),
        in_specs=[pl.BlockSpec((tm,), lambda i, *_: (i,), memory_space=pltpu.SMEM),
                  pl.BlockSpec((tm,), lambda i, *_: (n + i,), memory_space=pltpu.SMEM),
                  pl.BlockSpec((tm * tile, LANES), lambda i, *_: (i, 0))],
        out_specs=pl.BlockSpec(memory_space=pl.ANY),
        scratch_shapes=[pltpu.VMEM((EXPERT_ROWS * tile, LANES), F32), pltpu.SemaphoreType.DMA(())],
    )
    return pl.pallas_call(
        functools.partial(_dispatch_kernel, tile=tile),
        grid_spec=grid_spec,
        out_shape=jax.ShapeDtypeStruct((n_rows * tile, LANES), F32),
        compiler_params=pltpu.CompilerParams(dimension_semantics=("arbitrary",),
                                             vmem_limit_bytes=VMEM_LIMIT),
        name="dispatch",
    )(seg_end, seg_len, pos_flat, pos_flat, h2_tiles)


def _expert_kernel(tile_expert_ref, tile_live_ref, tile_next_ref, xs_hbm, wg_hbm, wu_hbm, wd_hbm, o_ref,
                   wg, wu, wd, xbuf, xsem, wbuf_g, wbuf_u, wbuf_d, wsem, wslot):
    i = pl.program_id(0)
    n = pl.num_programs(0)
    rows = xbuf.shape[1]

    def weight_copies(expert, slot):
        return [pltpu.make_async_copy(src.at[expert], dst.at[slot], wsem.at[slot, k])
                for k, (src, dst) in enumerate(((wg_hbm, wbuf_g), (wu_hbm, wbuf_u), (wd_hbm, wbuf_d)))]

    def fetch(step):
        slot = step % EXPERT_RING
        start = pl.multiple_of(step * rows, rows)
        return pltpu.make_async_copy(xs_hbm.at[pl.ds(start, rows)], xbuf.at[slot], xsem.at[slot])

    @pl.when(i == 0)
    def _():
        for step in range(EXPERT_RING - 1):
            fetch(step).start()

    @pl.when(i + EXPERT_RING - 1 < n)
    def _():
        fetch(i + EXPERT_RING - 1).start()

    fetch(i).wait()
    xs_ref = xbuf.at[i % EXPERT_RING]

    @pl.when(tile_live_ref[i] > 0)
    def _():
        changed = jnp.logical_or(i == 0, tile_expert_ref[i] != tile_expert_ref[jnp.maximum(i - 1, 0)])

        @pl.when(changed)
        def _():
            @pl.when(i == 0)
            def _():
                wslot[0] = 0
                for copy in weight_copies(tile_expert_ref[0], 0):
                    copy.start()

            slot = wslot[0]
            for copy in weight_copies(tile_expert_ref[i], slot):
                copy.wait()
            wg[...] = wbuf_g[slot].astype(BF16)
            wu[...] = wbuf_u[slot].astype(BF16)
            wd[...] = wbuf_d[slot].astype(BF16)
            upcoming = tile_next_ref[i]

            @pl.when(upcoming >= 0)
            def _():
                for copy in weight_copies(upcoming, 1 - slot):
                    copy.start()

            wslot[0] = 1 - slot

        x = _load_token_tiles(xs_ref, EXPERT_ROWS).astype(BF16)
        de = wg.shape[1]
        chunk = 2 * LANES
        pre = [(_dot(x, wg[:, c:c + chunk]), _dot(x, wu[:, c:c + chunk])) for c in range(0, de, chunk)]
        hid = [((gate * jax.nn.sigmoid(gate)) * up).astype(BF16) for gate, up in pre]
        tile = o_ref.shape[0] // EXPERT_ROWS
        for oc in range(0, wd.shape[1], chunk):
            out = None
            for n, h in enumerate(hid):
                part = _dot(h, wd[n * chunk:(n + 1) * chunk, oc:oc + chunk])
                out = part if out is None else out + part
            for c in range(chunk // LANES):
                o_ref[pl.ds(oc // LANES + c, EXPERT_ROWS, stride=tile), :] = out[:, c * LANES:(c + 1) * LANES]

    @pl.when(tile_live_ref[i] == 0)
    def _():
        o_ref[...] = jnp.zeros_like(o_ref)


def _experts(tile_expert, tile_live, tile_next, xs_tiles, w_gate, w_up, w_down):
    d, de = w_gate.shape[1:]
    tile = d // LANES
    n_rows = xs_tiles.shape[0] // tile
    tm = EXPERT_ROWS
    anywhere = pl.BlockSpec(memory_space=pl.ANY)
    grid_spec = pltpu.PrefetchScalarGridSpec(
        num_scalar_prefetch=3,
        grid=(n_rows // tm,),
        in_specs=[anywhere, anywhere, anywhere, anywhere],
        out_specs=pl.BlockSpec((tm * tile, LANES), lambda i, *_: (i, 0)),
        scratch_shapes=[pltpu.VMEM((d, de), BF16), pltpu.VMEM((d, de), BF16), pltpu.VMEM((de, d), BF16),
                        pltpu.VMEM((EXPERT_RING, tm * tile, LANES), F32),
                        pltpu.SemaphoreType.DMA((EXPERT_RING,)),
                        pltpu.VMEM((2, d, de), F32), pltpu.VMEM((2, d, de), F32), pltpu.VMEM((2, de, d), F32),
                        pltpu.SemaphoreType.DMA((2, 3)), pltpu.SMEM((1,), I32)],
    )
    return pl.pallas_call(
        _expert_kernel,
        grid_spec=grid_spec,
        out_shape=jax.ShapeDtypeStruct(xs_tiles.shape, F32),
        compiler_params=_params(1),
        name="experts",
    )(tile_expert, tile_live, tile_next, xs_tiles, w_gate, w_up, w_down)


def _combine_kernel(pos0_ref, pos1_ref, pos0_next_ref, pos1_next_ref, ys_ref, x1_ref, ew_ref, mod_ref, g_ref,
                    o_ref, buf, sem, *, d, final_norm):
    i = pl.program_id(0)
    n = pl.num_programs(0)
    tm = COMBINE_ROWS
    tile = d // LANES

    def fetch(prefs, slot):
        for r in range(tm):
            for k in range(2):
                pltpu.make_async_copy(_tokens(ys_ref, prefs[k][r], 1, tile),
                                      _tokens(buf.at[slot, k], r, 1, tile),
                                      sem.at[slot]).start(priority=k)

    @pl.when(i == 0)
    def _():
        fetch((pos0_ref, pos1_ref), 0)

    for parity in range(2):
        @pl.when((i + 1 < n) & ((i + 1) % 2 == parity))
        def _():
            fetch((pos0_next_ref, pos1_next_ref), parity)

    slot = i % 2

    def drain(r, _):
        pltpu.make_async_copy(_tokens(ys_ref, 0, 1, tile), _tokens(buf.at[slot, 0], 0, 1, tile),
                              sem.at[slot]).wait()
        return 0

    lax.fori_loop(0, 2 * tm, drain, 0, unroll=8)

    ew = ew_ref[...]
    y = (ew[:, 0:1] * _load_token_tiles(buf.at[slot, 0], tm)
         + ew[:, 1:2] * _load_token_tiles(buf.at[slot, 1], tm))
    g2 = mod_ref[0, :, 5 * d:6 * d]
    x2 = x1_ref[...] + g2 * y
    o_ref[...] = _rms(x2, g_ref[...]) if final_norm else x2


def _combine(pos_flat, ys, x1, ew, mod3, final_g, seq, final_norm):
    t, d = x1.shape
    tm = COMBINE_ROWS
    n = t // tm
    per_batch = seq // tm
    return pl.pallas_call(
        functools.partial(_combine_kernel, d=d, final_norm=final_norm),
        grid=(n,),
        in_specs=[pl.BlockSpec((tm,), lambda i: (i,), memory_space=pltpu.SMEM),
                  pl.BlockSpec((tm,), lambda i: (n + i,), memory_space=pltpu.SMEM),
                  pl.BlockSpec((tm,), lambda i: (jnp.minimum(i + 1, n - 1),), memory_space=pltpu.SMEM),
                  pl.BlockSpec((tm,), lambda i: (n + jnp.minimum(i + 1, n - 1),), memory_space=pltpu.SMEM),
                  pl.BlockSpec(memory_space=pl.ANY),
                  pl.BlockSpec((tm, d), lambda i: (i, 0)),
                  pl.BlockSpec((tm, LANES), lambda i: (i, 0)),
                  pl.BlockSpec((1, 1, mod3.shape[2]), lambda i: (i // per_batch, 0, 0)),
                  pl.BlockSpec((1, d), lambda i: (0, 0))],
        out_specs=pl.BlockSpec((tm, d), lambda i: (i, 0)),
        out_shape=jax.ShapeDtypeStruct((t, d), F32),
        scratch_shapes=[pltpu.VMEM((2, 2, tm * d // LANES, LANES), F32), pltpu.SemaphoreType.DMA((2,))],
        compiler_params=_params(1),
        name="combine",
    )(pos_flat, pos_flat, pos_flat, pos_flat, ys, x1, ew, mod3, final_g)


def _rope_tables(seq):
    inv = 1.0 / (ROPE_THETA ** (jnp.arange(0, HEAD_DIM, 2, dtype=F32) / HEAD_DIM))
    ang = jnp.arange(seq, dtype=F32)[:, None] * inv[None, :]
    cos, sin = jnp.cos(ang), jnp.sin(ang)
    cos_head = jnp.concatenate([cos, cos], axis=1)
    sin_head = jnp.concatenate([-sin, sin], axis=1)
    reps = LANES // HEAD_DIM
    return jnp.tile(cos_head, (1, reps)), jnp.tile(sin_head, (1, reps))


def kernel(x, c, w_ada, b_ada, norm1_g, w_in, lambda_q1, lambda_k1, lambda_q2, lambda_k2,
           diff_subln_g, w_proj_moba, w_proj_diff, w_out, norm2_g, w_group, b_group,
           w_expert, b_expert, w_gate, w_up, w_down, final_g):
    batch, seq, d = x.shape
    depth = w_ada.shape[0]
    t = batch * seq
    assert seq % PROJ_ROWS == 0 and seq % MOBA_BLOCK == 0 and seq // MOBA_BLOCK <= LANES
    assert ATTN_ROWS == MOBA_BLOCK and d % SEG == 0 and t % SORT_ROWS == 0 and batch <= 8
    assert seq % ATTN_Q_ROWS == 0 and ATTN_Q_ROWS % ATTN_ROWS == 0
    assert EXPERT_ROWS & (EXPERT_ROWS - 1) == 0
    n_rows = 2 * t + N_EXPERTS * EXPERT_ROWS
    n_tiles = n_rows // EXPERT_ROWS
    cos, sin = _rope_tables(seq)
    c_pad = jnp.zeros((8, d), F32).at[:batch].set(c)
    xf = x.reshape(t, d)
    row = lambda v: v.reshape(1, -1)
    for l in range(depth):
        mod = _ada(c_pad, w_ada[l], row(b_ada[l]))
        mod3 = mod[:batch].reshape(batch, 1, 6 * d)
        w_qkv = w_in[l][:, :N_QKV_SEGS * SEG].astype(BF16)
        w_gates = w_in[l][:, N_QKV_SEGS * SEG:].astype(BF16)
        qm, km, vm, qd, kd, vd, kmean = _proj(xf, mod3, row(norm1_g[l]), w_qkv, cos, sin, seq)
        kmean = kmean.reshape(batch, seq // MOBA_BLOCK, SEG)
        om = _moba(qm, km, vm, kmean, batch, seq)
        lam_init = 0.8 - 0.6 * math.exp(-0.3 * l)
        od = _diff(qd, kd, vd, row(lambda_q1[l]), row(lambda_k1[l]), row(lambda_q2[l]),
                   row(lambda_k2[l]), diff_subln_g[l].reshape(-1, 1), batch, seq, lam_init)
        w_router = jnp.zeros((d, LANES), F32)
        w_router = w_router.at[:, :N_GROUPS].set(w_group[l])
        w_router = w_router.at[:, N_GROUPS:N_GROUPS + N_EXPERTS].set(w_expert[l])
        b_router = jnp.zeros((1, LANES), F32)
        b_router = b_router.at[0, :N_GROUPS].set(b_group[l])
        b_router = b_router.at[0, N_GROUPS:N_GROUPS + N_EXPERTS].set(b_expert[l])
        x1, h2, eid_t, ew, counts = _merge(xf, om, od, mod3, row(norm1_g[l]), row(norm2_g[l]), w_gates,
                                 w_proj_moba[l].astype(BF16), w_proj_diff[l].astype(BF16),
                                 w_out[l].astype(BF16), w_router, b_router, seq)
        pos_t, tiles, segs = _pos(eid_t, counts, n_tiles)
        pos_flat = pos_t[:2].reshape(2 * t)
        xs = _dispatch(segs[0, :N_EXPERTS], segs[8, :N_EXPERTS], pos_flat, h2, n_rows, d // LANES)
        ys = _experts(tiles[:, 0], tiles[:, 1], tiles[:, 2], xs, w_gate[l], w_up[l], w_down[l])
        xf = _combine(pos_flat, ys, x1, ew, mod3, row(final_g), seq, final_norm=(l == depth - 1))
    return xf.reshape(batch, seq, d)
```

```python
import functools
import math

import jax
import jax.numpy as jnp
from jax import lax
from jax.experimental import pallas as pl
from jax.experimental.pallas import tpu as pltpu

F32 = jnp.float32
BF16 = jnp.bfloat16
I32 = jnp.int32

LANES = 128
HEAD_DIM = 64
HEADS_PER_VREG = LANES // HEAD_DIM
MOBA_HEADS = 8
MOBA_BLOCK = 256
MOBA_TOPK = 3
DIFF_HEADS = 4
ROPE_THETA = 10000.0
N_GROUPS = 4
EXPERTS_PER_GROUP = 8
N_EXPERTS = N_GROUPS * EXPERTS_PER_GROUP
NORM_EPS = 1e-6
NEG_INF = -1e30
SEG = MOBA_HEADS * HEAD_DIM
N_QKV_SEGS = 6
SUM_ROWS = 16

PROJ_ROWS = 512
ATTN_ROWS = 256
ATTN_Q_ROWS = 1024
MERGE_ROWS = 512
SORT_ROWS = 512
EXPERT_ROWS = 512
EXPERT_RING = 3
DISPATCH_ROWS = 1024
COMBINE_ROWS = 256
VMEM_LIMIT = 56 * 1024 * 1024


def _params(n_axes, vmem=VMEM_LIMIT):
    return pltpu.CompilerParams(dimension_semantics=("arbitrary",) * n_axes,
                                vmem_limit_bytes=vmem)


def _dot(a, b):
    return jnp.dot(a, b, preferred_element_type=F32)


def _dot_nt(a, b):
    return lax.dot_general(a, b, (((1,), (1,)), ((), ())), preferred_element_type=F32)


def _store_token_tiles(ref, x):
    n, d = x.shape
    chunks = d // LANES
    for c in range(chunks):
        ref[pl.ds(c, n, stride=chunks), :] = x[:, c * LANES:(c + 1) * LANES]


def _load_token_tiles(ref, n):
    chunks = ref.shape[0] // n
    return jnp.concatenate([ref[pl.ds(c, n, stride=chunks), :] for c in range(chunks)], axis=1)


def _rms(x, g):
    return x * lax.rsqrt(jnp.mean(x * x, axis=-1, keepdims=True) + NORM_EPS) * g


def _ada_kernel(c_ref, w_ref, b_ref, o_ref):
    c = c_ref[...]
    o_ref[...] = _dot(c * jax.nn.sigmoid(c), w_ref[...]) + b_ref[...]


def _ada(c_pad, w, b):
    rows, d = c_pad.shape
    n = w.shape[1]
    tn = 1536
    return pl.pallas_call(
        _ada_kernel,
        grid=(n // tn,),
        in_specs=[pl.BlockSpec((rows, d), lambda j: (0, 0)),
                  pl.BlockSpec((d, tn), lambda j: (0, j)),
                  pl.BlockSpec((1, tn), lambda j: (0, j))],
        out_specs=pl.BlockSpec((rows, tn), lambda j: (0, j)),
        out_shape=jax.ShapeDtypeStruct((rows, n), F32),
        compiler_params=_params(1),
        name="ada",
    )(c_pad, w, b)


def _rope(x, cos, sin_signed):
    half = HEAD_DIM // 2
    width = x.shape[1]
    lane = lax.broadcasted_iota(I32, x.shape, 1)
    first = (lane & (HEAD_DIM - 1)) < half
    partner = jnp.where(first, pltpu.roll(x, width - half, 1), pltpu.roll(x, half, 1))
    return x * cos + partner * sin_signed


def _proj_kernel(x_ref, mod_ref, g_ref, w_ref, cos_ref, sin_ref,
                 qm_ref, km_ref, vm_ref, qd_ref, kd_ref, vd_ref, kmean_ref, *, d):
    x = x_ref[...]
    sh = mod_ref[0, :, 0:d]
    sc = mod_ref[0, :, d:2 * d]
    h = (_rms(x, g_ref[...]) * (1.0 + sc) + sh).astype(BF16)
    cos = jnp.concatenate([cos_ref[...]] * (SEG // LANES), axis=1)
    sin = jnp.concatenate([sin_ref[...]] * (SEG // LANES), axis=1)
    scale = HEAD_DIM ** -0.5 * math.log2(math.e)
    outs = (qm_ref, km_ref, vm_ref, qd_ref, kd_ref, vd_ref)
    n_blk = x.shape[0] // ATTN_ROWS
    for seg, o_ref in enumerate(outs):
        y = _dot(h, w_ref[:, seg * SEG:(seg + 1) * SEG])
        if seg in (0, 1, 3, 4):
            y = _rope(y, cos, sin)
        if seg in (0, 3):
            y = y * scale
        if seg == 1:
            for blk in range(n_blk):
                rows = y[blk * MOBA_BLOCK:(blk + 1) * MOBA_BLOCK]
                kmean_ref[0, blk:blk + 1, :] = jnp.mean(rows, axis=0, keepdims=True)
        if seg in (0, 2, 3, 5):
            for blk in range(n_blk):
                for part in range(SEG // LANES):
                    piece = y[blk * ATTN_ROWS:(blk + 1) * ATTN_ROWS, part * LANES:(part + 1) * LANES]
                    o_ref[blk, part * LANES:(part + 1) * LANES, :] = piece.T.astype(BF16)
        else:
            o_ref[...] = y.astype(BF16)


def _proj(x2, mod3, g, w_qkv, cos, sin, seq):
    t, d = x2.shape
    tm = PROJ_ROWS
    per_batch = seq // tm
    row_spec = pl.BlockSpec((tm, SEG), lambda i: (i, 0))
    tab_spec = pl.BlockSpec((tm, LANES), lambda i: (i % per_batch, 0))
    act = jax.ShapeDtypeStruct((t, SEG), BF16)
    act_t = jax.ShapeDtypeStruct((t // ATTN_ROWS, SEG, ATTN_ROWS), BF16)
    t_spec = pl.BlockSpec((tm // ATTN_ROWS, SEG, ATTN_ROWS), lambda i: (i, 0, 0))
    return pl.pallas_call(
        functools.partial(_proj_kernel, d=d),
        grid=(t // tm,),
        in_specs=[pl.BlockSpec((tm, d), lambda i: (i, 0)),
                  pl.BlockSpec((1, 1, mod3.shape[2]), lambda i: (i // per_batch, 0, 0)),
                  pl.BlockSpec((1, d), lambda i: (0, 0)),
                  pl.BlockSpec(w_qkv.shape, lambda i: (0, 0)),
                  tab_spec, tab_spec],
        out_specs=[t_spec, row_spec, t_spec, t_spec, row_spec, t_spec,
                   pl.BlockSpec((1, tm // MOBA_BLOCK, SEG), lambda i: (i, 0, 0))],
        out_shape=[act_t, act, act_t, act_t, act, act_t,
                   jax.ShapeDtypeStruct((t // tm, tm // MOBA_BLOCK, SEG), F32)],
        compiler_params=_params(1),
        name="proj",
    )(x2, mod3, g, w_qkv, cos, sin)


def _attend(qs, keys_of_tile, k_ref, vt_ref, v_rows, q_tile, scratch):
    tq, tk = ATTN_Q_ROWS, ATTN_ROWS
    sub = tq // tk
    ones = jnp.ones((SUM_ROWS, tk), BF16)
    n_soft = len(qs)
    acc_refs, max_refs = scratch[:n_soft], scratch[n_soft:]
    for acc_ref in acc_refs:
        acc_ref[...] = jnp.zeros(acc_ref.shape, F32)

    def scores(j, first_query=0):
        start = pl.multiple_of(j * tk, tk)
        keys = keys_of_tile(j, k_ref[pl.ds(start, tk), :])
        return tuple(_dot(keys, q[:, first_query:]) for q in qs), vt_ref[j]

    def update(tile, mask, maxes, first_query=0):
        s_all, vt = tile
        new = ()
        for n, s in enumerate(s_all):
            if mask is not None:
                own = jnp.where(mask, s[:, :tk], NEG_INF)
                s = own if s.shape[1] == tk else jnp.concatenate([own, s[:, tk:]], axis=1)
            m = max_refs[n][:, first_query:] if maxes is None else maxes[n]
            m_new = jnp.maximum(m, jnp.max(s, axis=0, keepdims=True))
            alpha = jnp.exp2(m - m_new)
            p = jnp.exp2(s - m_new).astype(BF16)
            if maxes is None:
                max_refs[n][:, first_query:] = m_new
            new += (m_new,)
            vt_sum = jnp.concatenate([vt[v_rows[n]], ones], axis=0)
            acc_refs[n][:, first_query:] = alpha * acc_refs[n][:, first_query:] + _dot(vt_sum, p)
        return new

    first_own = q_tile * sub

    def body(group, maxes):
        for tile in [scores(sub * group + b) for b in range(sub)]:
            maxes = update(tile, None, maxes)
        return maxes

    maxes = lax.fori_loop(0, q_tile, body, (jnp.full((1, tq), NEG_INF, F32),) * n_soft)
    for max_ref, m in zip(max_refs, maxes):
        max_ref[...] = m
    own = [scores(first_own + b, b * tk) for b in range(sub)]
    causal = lax.broadcasted_iota(I32, (tk, tk), 0) <= lax.broadcasted_iota(I32, (tk, tk), 1)
    for b in range(sub):
        update(own[b], causal, None, b * tk)
    out = []
    for n, acc_ref in enumerate(acc_refs):
        rows = v_rows[n].stop - v_rows[n].start
        out.append((acc_ref[rows:rows + 1, :], acc_ref[0:rows, :]))
    return out


def _store_position_tiles(ref, x_t):
    for b in range(ref.shape[0]):
        ref[b] = x_t[:, b * ATTN_ROWS:(b + 1) * ATTN_ROWS]


def _softmax_scratch(n_softmax, features):
    return ([pltpu.VMEM((features + SUM_ROWS, ATTN_Q_ROWS), F32)] * n_softmax
            + [pltpu.VMEM((1, ATTN_Q_ROWS), F32)] * n_softmax)


def _moba_kernel(q_ref, k_ref, v_ref, kmean_ref, o_ref, *scratch, n_blocks):
    tq = ATTN_Q_ROWS
    q_tile = pl.program_id(2)
    q_t = jnp.concatenate([q_ref[b] for b in range(q_ref.shape[0])], axis=1)
    feat = lax.broadcasted_iota(I32, (LANES, tq), 0)
    blk = lax.broadcasted_iota(I32, (n_blocks, tq), 0)
    qry = lax.broadcasted_iota(I32, (n_blocks, tq), 1)
    own_block = q_tile * (tq // MOBA_BLOCK) + (qry >> int(math.log2(MOBA_BLOCK)))
    kmean = kmean_ref[0]
    km_head = lax.broadcasted_iota(I32, kmean.shape, 1) >> 6
    pad = jnp.zeros((LANES - n_blocks, tq), BF16)
    lane_k = lax.broadcasted_iota(I32, (ATTN_ROWS, LANES), 1)
    qs = []
    for head in range(HEADS_PER_VREG):
        qh = jnp.where((feat >> 6) == head, q_t, jnp.zeros_like(q_t))
        km = jnp.where(km_head == head, kmean, 0.0)
        km_hi = km.astype(BF16)
        km_lo = (km - km_hi.astype(F32)).astype(BF16)
        gate = _dot(km_hi, qh) + _dot(km_lo, qh)
        gate = jnp.where(blk < own_block, gate, NEG_INF)
        bias = jnp.where(blk == own_block, 0.0, NEG_INF)
        for _ in range(min(MOBA_TOPK, n_blocks)):
            top = jnp.max(gate, axis=0, keepdims=True)
            first = jnp.min(jnp.where(gate == top, blk, n_blocks), axis=0, keepdims=True)
            picked = blk == first
            bias = jnp.where(picked & (blk < own_block), 0.0, bias)
            gate = jnp.where(picked, -jnp.inf, gate)
        qs.append(jnp.concatenate([qh, bias.astype(BF16), pad], axis=0))

    def rhs_of_tile(j, keys):
        return jnp.concatenate([keys, jnp.where(lane_k == j, 1.0, 0.0).astype(BF16)], axis=1)

    v_rows = [slice(h * HEAD_DIM, (h + 1) * HEAD_DIM) for h in range(HEADS_PER_VREG)]
    stats = _attend(qs, rhs_of_tile, k_ref, v_ref, v_rows, q_tile, scratch)
    out_t = jnp.concatenate([acc / l for l, acc in stats], axis=0)
    _store_position_tiles(o_ref, out_t.astype(BF16))


def _moba(qm, km, vm_t, kmean, batch, seq):
    t = batch * seq
    tq = ATTN_Q_ROWS
    nq = seq // tq
    n_blocks = seq // MOBA_BLOCK
    pairs = MOBA_HEADS // HEADS_PER_VREG
    n_kv = seq // ATTN_ROWS
    return pl.pallas_call(
        functools.partial(_moba_kernel, n_blocks=n_blocks),
        grid=(batch, pairs, nq),
        in_specs=[pl.BlockSpec((tq // ATTN_ROWS, LANES, ATTN_ROWS), lambda b, p, i: (b * nq + i, p, 0)),
                  pl.BlockSpec((seq, LANES), lambda b, p, i: (b, p)),
                  pl.BlockSpec((n_kv, LANES, ATTN_ROWS), lambda b, p, i: (b, p, 0)),
                  pl.BlockSpec((1, n_blocks, LANES), lambda b, p, i: (b, 0, p))],
        out_specs=pl.BlockSpec((tq // ATTN_ROWS, LANES, ATTN_ROWS), lambda b, p, i: (b * nq + i, p, 0)),
        out_shape=jax.ShapeDtypeStruct((t // ATTN_ROWS, SEG, ATTN_ROWS), BF16),
        scratch_shapes=_softmax_scratch(HEADS_PER_VREG, HEAD_DIM),
        compiler_params=_params(3),
        name="moba",
    )(qm, km, vm_t, kmean)


def _diff_kernel(q_ref, k_ref, v_ref, lq1_ref, lk1_ref, lq2_ref, lk2_ref, g_ref, o_ref,
                 *scratch, lam_init):
    tq = ATTN_Q_ROWS
    q_t = jnp.concatenate([q_ref[b] for b in range(q_ref.shape[0])], axis=1)
    comp = lax.broadcasted_iota(I32, (LANES, tq), 0) >> 6
    qs = [jnp.where(comp == c, q_t, jnp.zeros_like(q_t)) for c in range(2)]
    v_rows = [slice(0, LANES)] * 2
    (l1, a1), (l2, a2) = _attend(qs, lambda j, keys: keys, k_ref, v_ref, v_rows,
                                 pl.program_id(2), scratch)
    lam = (jnp.exp(jnp.sum(lq1_ref[...] * lk1_ref[...], axis=1, keepdims=True))
           - jnp.exp(jnp.sum(lq2_ref[...] * lk2_ref[...], axis=1, keepdims=True)) + lam_init)
    o = a1 / l1 - lam * (a2 / l2)
    o = o * lax.rsqrt(jnp.mean(o * o, axis=0, keepdims=True) + NORM_EPS) * g_ref[...]
    _store_position_tiles(o_ref, (o * (1.0 - lam_init)).astype(BF16))


def _diff(qd, kd, vd_t, lq1, lk1, lq2, lk2, g, batch, seq, lam_init):
    t = batch * seq
    tq = ATTN_Q_ROWS
    nq = seq // tq
    n_kv = seq // ATTN_ROWS
    vec = lambda a: pl.BlockSpec(a.shape, lambda b, h, i: (0, 0))
    return pl.pallas_call(
        functools.partial(_diff_kernel, lam_init=lam_init),
        grid=(batch, DIFF_HEADS, nq),
        in_specs=[pl.BlockSpec((tq // ATTN_ROWS, LANES, ATTN_ROWS), lambda b, h, i: (b * nq + i, h, 0)),
                  pl.BlockSpec((seq, LANES), lambda b, h, i: (b, h)),
                  pl.BlockSpec((n_kv, LANES, ATTN_ROWS), lambda b, h, i: (b, h, 0)),
                  vec(lq1), vec(lk1), vec(lq2), vec(lk2), vec(g)],
        out_specs=pl.BlockSpec((tq // ATTN_ROWS, LANES, ATTN_ROWS), lambda b, h, i: (b * nq + i, h, 0)),
        out_shape=jax.ShapeDtypeStruct((t // ATTN_ROWS, SEG, ATTN_ROWS), BF16),
        scratch_shapes=_softmax_scratch(2, LANES),
        compiler_params=_params(3),
        name="diff",
    )(qd, kd, vd_t, lq1, lk1, lq2, lk2, g)


def _merge_kernel(x_ref, om_ref, od_ref, mod_ref, g1n_ref, g2n_ref, wg_ref, wpm_ref, wpd_ref,
                  wout_ref, wr_ref, br_ref, x1_ref, h2_ref, eid_ref, ew_ref, counts_ref, merged, counts,
                  *, d):
    x = x_ref[...]
    sh1 = mod_ref[0, :, 0:d]
    sc1 = mod_ref[0, :, d:2 * d]
    g1 = mod_ref[0, :, 2 * d:3 * d]
    sh2 = mod_ref[0, :, 3 * d:4 * d]
    sc2 = mod_ref[0, :, 4 * d:5 * d]
    h = (_rms(x, g1n_ref[...]) * (1.0 + sc1) + sh1).astype(BF16)
    om, od = (jnp.concatenate([ref[b].astype(F32).T for b in range(ref.shape[0])], axis=0).astype(BF16)
              for ref in (om_ref, od_ref))
    chunk = SEG
    for c in range(d // chunk):
        cols = slice(c * chunk, (c + 1) * chunk)
        gm = _dot(h, wg_ref[:, c * chunk:(c + 1) * chunk])
        gd = _dot(h, wg_ref[:, d + c * chunk:d + (c + 1) * chunk])
        ym = _dot(om, wpm_ref[:, cols])
        yd = _dot(od, wpd_ref[:, cols])
        merged[:, cols] = (jax.nn.sigmoid(gm) * ym + jax.nn.sigmoid(gd) * yd).astype(BF16)
    x1 = x + g1 * _dot(merged[...], wout_ref[...])
    x1_ref[...] = x1
    h2 = _rms(x1, g2n_ref[...]) * (1.0 + sc2) + sh2
    _store_token_tiles(h2_ref, h2)

    h2_hi = h2.astype(BF16)
    h2_lo = (h2 - h2_hi.astype(F32)).astype(BF16)
    wr = wr_ref[...]
    wr_hi = wr.astype(BF16)
    wr_lo = (wr - wr_hi.astype(F32)).astype(BF16)
    logits = _dot(h2_hi, wr_hi) + _dot(h2_lo, wr_hi) + _dot(h2_hi, wr_lo) + br_ref[...]
    lane = lax.broadcasted_iota(I32, logits.shape, 1)
    is_group = lane < N_GROUPS
    gl = jnp.where(is_group, logits, -jnp.inf)
    gmax = jnp.max(gl, axis=1, keepdims=True)
    gexp = jnp.exp(gl - gmax)
    g_w = 1.0 / jnp.sum(gexp, axis=1, keepdims=True)
    g_idx = jnp.min(jnp.where(gl == gmax, lane, LANES), axis=1, keepdims=True)
    e_lane = lane - N_GROUPS
    in_group = (e_lane >= g_idx * EXPERTS_PER_GROUP) & (e_lane < (g_idx + 1) * EXPERTS_PER_GROUP)
    el = jnp.where(in_group, logits, -jnp.inf)
    emax = jnp.max(el, axis=1, keepdims=True)
    eexp = jnp.exp(el - emax)
    prob = eexp / jnp.sum(eexp, axis=1, keepdims=True)
    prob = jnp.where(in_group, prob, -1.0)
    p1 = jnp.max(prob, axis=1, keepdims=True)
    i1 = jnp.min(jnp.where(prob == p1, lane, LANES), axis=1, keepdims=True)
    prob2 = jnp.where(lane == i1, -1.0, prob)
    p2 = jnp.max(prob2, axis=1, keepdims=True)
    i2 = jnp.min(jnp.where(prob2 == p2, lane, LANES), axis=1, keepdims=True)
    tot = p1 + p2
    eid = jnp.where(lane == 0, i1 - N_GROUPS, jnp.where(lane == 1, i2 - N_GROUPS, 0))
    eid_ref[...] = eid.T[0:8, :]

    @pl.when(pl.program_id(0) == 0)
    def _():
        counts[...] = jnp.zeros_like(counts)

    chosen = jnp.where((lane == i1) | (lane == i2), 1.0, 0.0)
    per_lane = jnp.broadcast_to(jnp.sum(chosen, axis=0, keepdims=True), counts.shape)
    counts[...] = counts[...] + pltpu.roll(per_lane, LANES - N_GROUPS, 1)
    counts_ref[...] = counts[...].astype(I32)
    ew_ref[...] = jnp.where(lane == 0, g_w * (p1 / tot), jnp.where(lane == 1, g_w * (p2 / tot), 0.0))


def _merge(x2, om, od, mod3, g1n, g2n, wg, wpm, wpd, wout, wr, br, seq):
    t, d = x2.shape
    tm = MERGE_ROWS
    per_batch = seq // tm
    full = lambda a: pl.BlockSpec(a.shape, lambda i: (0,) * a.ndim)
    row = lambda w: pl.BlockSpec((tm, w), lambda i: (i, 0))
    tiles_t = pl.BlockSpec((tm // ATTN_ROWS, SEG, ATTN_ROWS), lambda i: (i, 0, 0))
    return pl.pallas_call(
        functools.partial(_merge_kernel, d=d),
        grid=(t // tm,),
        in_specs=[row(d), tiles_t, tiles_t,
                  pl.BlockSpec((1, 1, mod3.shape[2]), lambda i: (i // per_batch, 0, 0)),
                  full(g1n), full(g2n), full(wg), full(wpm), full(wpd), full(wout), full(wr), full(br)],
        out_specs=[row(d), pl.BlockSpec((tm * d // LANES, LANES), lambda i: (i, 0)),
                   pl.BlockSpec((8, tm), lambda i: (0, i)), row(LANES),
                   pl.BlockSpec((8, LANES), lambda i: (0, 0))],
        out_shape=[jax.ShapeDtypeStruct((t, d), F32), jax.ShapeDtypeStruct((t * d // LANES, LANES), F32),
                   jax.ShapeDtypeStruct((8, t), I32), jax.ShapeDtypeStruct((t, LANES), F32),
                   jax.ShapeDtypeStruct((8, LANES), I32)],
        scratch_shapes=[pltpu.VMEM((tm, d), BF16), pltpu.VMEM((8, LANES), F32)],
        compiler_params=_params(1),
        name="merge",
    )(x2, om, od, mod3, g1n, g2n, wg, wpm, wpd, wout, wr, br)


def _segment_ends(counts):
    lane = lax.broadcasted_iota(I32, counts.shape, 1)
    padded = (counts + (EXPERT_ROWS - 1)) & (-EXPERT_ROWS)
    padded = jnp.where(lane < N_EXPERTS, padded, 0)
    ends = padded
    shift = 1
    while shift < N_EXPERTS:
        ends = ends + jnp.where(lane >= shift, pltpu.roll(ends, shift, 1), 0)
        shift *= 2
    return padded, ends


def _pos_kernel(eid_ref, counts_ref, pos_ref, tile_ref, seg_ref, carry):
    tm = eid_ref.shape[1]

    @pl.when(pl.program_id(0) == 0)
    def _():
        carry[...] = jnp.zeros_like(carry)

    padded, ends = _segment_ends(counts_ref[...])
    starts = (ends - padded).astype(F32).T[0:N_EXPERTS, 0:1]
    eid = eid_ref[...]
    expert = lax.broadcasted_iota(I32, (N_EXPERTS, tm), 0)
    oh0 = expert == eid[0:1, :]
    oh1 = expert == eid[1:2, :]
    used = jnp.where(oh0 | oh1, 1.0, 0.0)
    r = lax.broadcasted_iota(I32, (tm, tm), 0)
    c = lax.broadcasted_iota(I32, (tm, tm), 1)
    earlier = jnp.where(r < c, 1.0, 0.0).astype(BF16)
    base = starts + carry[:, 0:1] + _dot(used.astype(BF16), earlier)
    p0 = jnp.sum(jnp.where(oh0, base, 0.0), axis=0, keepdims=True)
    p1 = jnp.sum(jnp.where(oh1, base, 0.0), axis=0, keepdims=True)
    slot = lax.broadcasted_iota(I32, pos_ref.shape, 0)
    pos_ref[...] = jnp.where(slot == 0, p0, jnp.where(slot == 1, p1, 0.0)).astype(I32)
    carry[...] = carry[...] + jnp.sum(used, axis=1, keepdims=True)

    @pl.when(pl.program_id(0) == 0)
    def _():
        n_tiles = tile_ref.shape[0]
        first_row = lax.broadcasted_iota(I32, (n_tiles, LANES), 0) * EXPERT_ROWS
        elane = lax.broadcasted_iota(I32, (n_tiles, LANES), 1)
        done = jnp.where((ends[0:1, :] <= first_row) & (elane < N_EXPERTS), 1, 0)
        expert = jnp.minimum(jnp.sum(done, axis=1, keepdims=True), N_EXPERTS - 1)
        total = jnp.max(ends[0:1, :], axis=1, keepdims=True)
        live = jnp.where(first_row < total, 1, 0)
        own_end = jnp.sum(jnp.where(elane == expert, ends[0:1, :], 0), axis=1, keepdims=True)
        upcoming = jnp.sum(jnp.where((ends[0:1, :] <= own_end) & (elane < N_EXPERTS), 1, 0),
                           axis=1, keepdims=True)
        upcoming = jnp.where(upcoming < N_EXPERTS, upcoming, -1)
        tile_ref[...] = jnp.where(elane == 0, expert,
                                  jnp.where(elane == 1, live, jnp.where(elane == 2, upcoming, 0)))
        seg_ref[...] = jnp.concatenate([ends, padded], axis=0)


def _pos(eid_t, counts, n_tiles):
    t = eid_t.shape[1]
    tm = SORT_ROWS
    col = pl.BlockSpec((8, tm), lambda i: (0, i))
    return pl.pallas_call(
        _pos_kernel,
        grid=(t // tm,),
        in_specs=[col, pl.BlockSpec((8, LANES), lambda i: (0, 0))],
        out_specs=[col, pl.BlockSpec((n_tiles, LANES), lambda i: (0, 0)),
                   pl.BlockSpec((16, LANES), lambda i: (0, 0))],
        out_shape=[jax.ShapeDtypeStruct((8, t), I32),
                   jax.ShapeDtypeStruct((n_tiles, LANES), I32),
                   jax.ShapeDtypeStruct((16, LANES), I32)],
        scratch_shapes=[pltpu.VMEM((N_EXPERTS, LANES), F32)],
        compiler_params=_params(1),
        name="pos",
    )(eid_t, counts)


def _tokens(ref, first, count, tile):
    start = first * tile
    if not isinstance(start, int):
        start = pl.multiple_of(start, tile)
    return ref.at[pl.ds(start, count * tile)]


def _dispatch_kernel(seg_end_ref, seg_len_ref, pos0_ref, pos1_ref, h2_ref, xs_ref, zeros, sem, *, tile):
    i = pl.program_id(0)
    pos_refs = (pos0_ref, pos1_ref)
    tm = DISPATCH_ROWS

    def clear_copy(first):
        return pltpu.make_async_copy(zeros, _tokens(xs_ref, first, EXPERT_ROWS, tile), sem)

    @pl.when(i == 0)
    def _():
        zeros[...] = jnp.zeros_like(zeros)
        for e in range(N_EXPERTS):
            @pl.when(seg_len_ref[e] > 0)
            def _():
                clear_copy(pl.multiple_of(seg_end_ref[e] - EXPERT_ROWS, EXPERT_ROWS)).start()
        for e in range(N_EXPERTS):
            @pl.when(seg_len_ref[e] > 0)
            def _():
                clear_copy(0).wait()

        first_unused = seg_end_ref[N_EXPERTS - 1] // EXPERT_ROWS
        n_tiles = xs_ref.shape[0] // (EXPERT_ROWS * tile)

        def clear(t, _):
            clear_copy(pl.multiple_of(t * EXPERT_ROWS, EXPERT_ROWS)).start()
            return 0

        def clear_done(t, _):
            clear_copy(0).wait()
            return 0

        lax.fori_loop(first_unused, n_tiles, clear, 0)
        lax.fori_loop(first_unused, n_tiles, clear_done, 0)

    for r in range(tm):
        for k in range(2):
            pltpu.make_async_copy(_tokens(h2_ref, r, 1, tile), _tokens(xs_ref, pos_refs[k][r], 1, tile),
                                  sem).start(priority=k)

    def drain(r, _):
        pltpu.make_async_copy(_tokens(h2_ref, 0, 1, tile), _tokens(xs_ref, 0, 1, tile), sem).wait()
        return 0

    lax.fori_loop(0, 2 * tm, drain, 0, unroll=8)


def _dispatch(seg_end, seg_len, pos_flat, h2_tiles, n_rows, tile):
    t = h2_tiles.shape[0] // tile
    tm = DISPATCH_ROWS
    n = t // tm
    grid_spec = pltpu.PrefetchScalarGridSpec(
        num_scalar_prefetch=2,
        grid=(n,),
        in_specs=[pl.BlockSpec((tm,), lambda i, *_: (i,), memory_space=pltpu.SMEM),
                  pl.BlockSpec((tm,), lambda i, *_: (n + i,), memory_space=pltpu.SMEM),
                  pl.BlockSpec((tm * tile, LANES), lambda i, *_: (i, 0))],
        out_specs=pl.BlockSpec(memory_space=pl.ANY),
        scratch_shapes=[pltpu.VMEM((EXPERT_ROWS * tile, LANES), F32), pltpu.SemaphoreType.DMA(())],
    )
    return pl.pallas_call(
        functools.partial(_dispatch_kernel, tile=tile),
        grid_spec=grid_spec,
        out_shape=jax.ShapeDtypeStruct((n_rows * tile, LANES), F32),
        compiler_params=pltpu.CompilerParams(dimension_semantics=("arbitrary",),
                                             vmem_limit_bytes=VMEM_LIMIT),
        name="dispatch",
    )(seg_end, seg_len, pos_flat, pos_flat, h2_tiles)


def _expert_kernel(tile_expert_ref, tile_live_ref, tile_next_ref, xs_hbm, wg_hbm, wu_hbm, wd_hbm, o_ref,
                   wg, wu, wd, xbuf, xsem, wbuf_g, wbuf_u, wbuf_d, wsem, wslot):
    i = pl.program_id(0)
    n = pl.num_programs(0)
    rows = xbuf.shape[1]

    def weight_copies(expert, slot):
        return [pltpu.make_async_copy(src.at[expert], dst.at[slot], wsem.at[slot, k])
                for k, (src, dst) in enumerate(((wg_hbm, wbuf_g), (wu_hbm, wbuf_u), (wd_hbm, wbuf_d)))]

    def fetch(step):
        slot = step % EXPERT_RING
        start = pl.multiple_of(step * rows, rows)
        return pltpu.make_async_copy(xs_hbm.at[pl.ds(start, rows)], xbuf.at[slot], xsem.at[slot])

    @pl.when(i == 0)
    def _():
        for step in range(EXPERT_RING - 1):
            fetch(step).start()

    @pl.when(i + EXPERT_RING - 1 < n)
    def _():
        fetch(i + EXPERT_RING - 1).start()

    fetch(i).wait()
    xs_ref = xbuf.at[i % EXPERT_RING]

    @pl.when(tile_live_ref[i] > 0)
    def _():
        changed = jnp.logical_or(i == 0, tile_expert_ref[i] != tile_expert_ref[jnp.maximum(i - 1, 0)])

        @pl.when(changed)
        def _():
            @pl.when(i == 0)
            def _():
                wslot[0] = 0
                for copy in weight_copies(tile_expert_ref[0], 0):
                    copy.start()

            slot = wslot[0]
            for copy in weight_copies(tile_expert_ref[i], slot):
                copy.wait()
            wg[...] = wbuf_g[slot].astype(BF16)
            wu[...] = wbuf_u[slot].astype(BF16)
            wd[...] = wbuf_d[slot].astype(BF16)
            upcoming = tile_next_ref[i]

            @pl.when(upcoming >= 0)
            def _():
                for copy in weight_copies(upcoming, 1 - slot):
                    copy.start()

            wslot[0] = 1 - slot

        x = _load_token_tiles(xs_ref, EXPERT_ROWS).astype(BF16)
        de = wg.shape[1]
        chunk = 2 * LANES
        pre = [(_dot(x, wg[:, c:c + chunk]), _dot(x, wu[:, c:c + chunk])) for c in range(0, de, chunk)]
        hid = [((gate * jax.nn.sigmoid(gate)) * up).astype(BF16) for gate, up in pre]
        tile = o_ref.shape[0] // EXPERT_ROWS
        for oc in range(0, wd.shape[1], chunk):
            out = None
            for n, h in enumerate(hid):
                part = _dot(h, wd[n * chunk:(n + 1) * chunk, oc:oc + chunk])
                out = part if out is None else out + part
            for c in range(chunk // LANES):
                o_ref[pl.ds(oc // LANES + c, EXPERT_ROWS, stride=tile), :] = out[:, c * LANES:(c + 1) * LANES]

    @pl.when(tile_live_ref[i] == 0)
    def _():
        o_ref[...] = jnp.zeros_like(o_ref)


def _experts(tile_expert, tile_live, tile_next, xs_tiles, w_gate, w_up, w_down):
    d, de = w_gate.shape[1:]
    tile = d // LANES
    n_rows = xs_tiles.shape[0] // tile
    tm = EXPERT_ROWS
    anywhere = pl.BlockSpec(memory_space=pl.ANY)
    grid_spec = pltpu.PrefetchScalarGridSpec(
        num_scalar_prefetch=3,
        grid=(n_rows // tm,),
        in_specs=[anywhere, anywhere, anywhere, anywhere],
        out_specs=pl.BlockSpec((tm * tile, LANES), lambda i, *_: (i, 0)),
        scratch_shapes=[pltpu.VMEM((d, de), BF16), pltpu.VMEM((d, de), BF16), pltpu.VMEM((de, d), BF16),
                        pltpu.VMEM((EXPERT_RING, tm * tile, LANES), F32),
                        pltpu.SemaphoreType.DMA((EXPERT_RING,)),
                        pltpu.VMEM((2, d, de), F32), pltpu.VMEM((2, d, de), F32), pltpu.VMEM((2, de, d), F32),
                        pltpu.SemaphoreType.DMA((2, 3)), pltpu.SMEM((1,), I32)],
    )
    return pl.pallas_call(
        _expert_kernel,
        grid_spec=grid_spec,
        out_shape=jax.ShapeDtypeStruct(xs_tiles.shape, F32),
        compiler_params=_params(1),
        name="experts",
    )(tile_expert, tile_live, tile_next, xs_tiles, w_gate, w_up, w_down)


def _combine_kernel(pos0_ref, pos1_ref, pos0_next_ref, pos1_next_ref, ys_ref, x1_ref, ew_ref, mod_ref, g_ref,
                    o_ref, buf, sem, *, d, final_norm):
    i = pl.program_id(0)
    n = pl.num_programs(0)
    tm = COMBINE_ROWS
    tile = d // LANES

    def fetch(prefs, slot):
        for r in range(tm):
            for k in range(2):
                pltpu.make_async_copy(_tokens(ys_ref, prefs[k][r], 1, tile),
                                      _tokens(buf.at[slot, k], r, 1, tile),
                                      sem.at[slot]).start(priority=k)

    @pl.when(i == 0)
    def _():
        fetch((pos0_ref, pos1_ref), 0)

    for parity in range(2):
        @pl.when((i + 1 < n) & ((i + 1) % 2 == parity))
        def _():
            fetch((pos0_next_ref, pos1_next_ref), parity)

    slot = i % 2

    def drain(r, _):
        pltpu.make_async_copy(_tokens(ys_ref, 0, 1, tile), _tokens(buf.at[slot, 0], 0, 1, tile),
                              sem.at[slot]).wait()
        return 0

    lax.fori_loop(0, 2 * tm, drain, 0, unroll=8)

    ew = ew_ref[...]
    y = (ew[:, 0:1] * _load_token_tiles(buf.at[slot, 0], tm)
         + ew[:, 1:2] * _load_token_tiles(buf.at[slot, 1], tm))
    g2 = mod_ref[0, :, 5 * d:6 * d]
    x2 = x1_ref[...] + g2 * y
    o_ref[...] = _rms(x2, g_ref[...]) if final_norm else x2


def _combine(pos_flat, ys, x1, ew, mod3, final_g, seq, final_norm):
    t, d = x1.shape
    tm = COMBINE_ROWS
    n = t // tm
    per_batch = seq // tm
    return pl.pallas_call(
        functools.partial(_combine_kernel, d=d, final_norm=final_norm),
        grid=(n,),
        in_specs=[pl.BlockSpec((tm,), lambda i: (i,), memory_space=pltpu.SMEM),
                  pl.BlockSpec((tm,), lambda i: (n + i,), memory_space=pltpu.SMEM),
                  pl.BlockSpec((tm,), lambda i: (jnp.minimum(i + 1, n - 1),), memory_space=pltpu.SMEM),
                  pl.BlockSpec((tm,), lambda i: (n + jnp.minimum(i + 1, n - 1),), memory_space=pltpu.SMEM),
                  pl.BlockSpec(memory_space=pl.ANY),
                  pl.BlockSpec((tm, d), lambda i: (i, 0)),
                  pl.BlockSpec((tm, LANES), lambda i: (i, 0)),
                  pl.BlockSpec((1, 1, mod3.shape[2]), lambda i: (i // per_batch, 0, 0)),
                  pl.BlockSpec((1, d), lambda i: (0, 0))],
        out_specs=pl.BlockSpec((tm, d), lambda i: (i, 0)),
        out_shape=jax.ShapeDtypeStruct((t, d), F32),
        scratch_shapes=[pltpu.VMEM((2, 2, tm * d // LANES, LANES), F32), pltpu.SemaphoreType.DMA((2,))],
        compiler_params=_params(1),
        name="combine",
    )(pos_flat, pos_flat, pos_flat, pos_flat, ys, x1, ew, mod3, final_g)


def _rope_tables(seq):
    inv = 1.0 / (ROPE_THETA ** (jnp.arange(0, HEAD_DIM, 2, dtype=F32) / HEAD_DIM))
    ang = jnp.arange(seq, dtype=F32)[:, None] * inv[None, :]
    cos, sin = jnp.cos(ang), jnp.sin(ang)
    cos_head = jnp.concatenate([cos, cos], axis=1)
    sin_head = jnp.concatenate([-sin, sin], axis=1)
    reps = LANES // HEAD_DIM
    return jnp.tile(cos_head, (1, reps)), jnp.tile(sin_head, (1, reps))


def kernel(x, c, w_ada, b_ada, norm1_g, w_in, lambda_q1, lambda_k1, lambda_q2, lambda_k2,
           diff_subln_g, w_proj_moba, w_proj_diff, w_out, norm2_g, w_group, b_group,
           w_expert, b_expert, w_gate, w_up, w_down, final_g):
    batch, seq, d = x.shape
    depth = w_ada.shape[0]
    t = batch * seq
    assert seq % PROJ_ROWS == 0 and seq % MOBA_BLOCK == 0 and seq // MOBA_BLOCK <= LANES
    assert ATTN_ROWS == MOBA_BLOCK and d % SEG == 0 and t % SORT_ROWS == 0 and batch <= 8
    assert seq % ATTN_Q_ROWS == 0 and ATTN_Q_ROWS % ATTN_ROWS == 0
    assert EXPERT_ROWS & (EXPERT_ROWS - 1) == 0
    n_rows = 2 * t + N_EXPERTS * EXPERT_ROWS
    n_tiles = n_rows // EXPERT_ROWS
    cos, sin = _rope_tables(seq)
    c_pad = jnp.zeros((8, d), F32).at[:batch].set(c)
    xf = x.reshape(t, d)
    row = lambda v: v.reshape(1, -1)
    for l in range(depth):
        mod = _ada(c_pad, w_ada[l], row(b_ada[l]))
        mod3 = mod[:batch].reshape(batch, 1, 6 * d)
        w_qkv = w_in[l][:, :N_QKV_SEGS * SEG].astype(BF16)
        w_gates = w_in[l][:, N_QKV_SEGS * SEG:].astype(BF16)
        qm, km, vm, qd, kd, vd, kmean = _proj(xf, mod3, row(norm1_g[l]), w_qkv, cos, sin, seq)
        kmean = kmean.reshape(batch, seq // MOBA_BLOCK, SEG)
        om = _moba(qm, km, vm, kmean, batch, seq)
        lam_init = 0.8 - 0.6 * math.exp(-0.3 * l)
        od = _diff(qd, kd, vd, row(lambda_q1[l]), row(lambda_k1[l]), row(lambda_q2[l]),
                   row(lambda_k2[l]), diff_subln_g[l].reshape(-1, 1), batch, seq, lam_init)
        w_router = jnp.zeros((d, LANES), F32)
        w_router = w_router.at[:, :N_GROUPS].set(w_group[l])
        w_router = w_router.at[:, N_GROUPS:N_GROUPS + N_EXPERTS].set(w_expert[l])
        b_router = jnp.zeros((1, LANES), F32)
        b_router = b_router.at[0, :N_GROUPS].set(b_group[l])
        b_router = b_router.at[0, N_GROUPS:N_GROUPS + N_EXPERTS].set(b_expert[l])
        x1, h2, eid_t, ew, counts = _merge(xf, om, od, mod3, row(norm1_g[l]), row(norm2_g[l]), w_gates,
                                 w_proj_moba[l].astype(BF16), w_proj_diff[l].astype(BF16),
                                 w_out[l].astype(BF16), w_router, b_router, seq)
        pos_t, tiles, segs = _pos(eid_t, counts, n_tiles)
        pos_flat = pos_t[:2].reshape(2 * t)
        xs = _dispatch(segs[0, :N_EXPERTS], segs[8, :N_EXPERTS], pos_flat, h2, n_rows, d // LANES)
        ys = _experts(tiles[:, 0], tiles[:, 1], tiles[:, 2], xs, w_gate[l], w_up[l], w_down[l])
        xf = _combine(pos_flat, ys, x1, ew, mod3, row(final_g), seq, final_norm=(l == depth - 1))
    return xf.reshape(batch, seq, d)
```
